```python
import math
import jax, jax.numpy as jnp
from jax import lax
import numpy as np

D_MODEL = 1024
BATCH = 16
SEQ = 4096
DEPTH = 1

GRID_W = 64
CTX_LEN = 256
N_HEADS = 8
HEAD_DIM = 64
V_DIM = 2 * HEAD_DIM
ATTN_WIDTH = N_HEADS * V_DIM
S5_WIDTH = D_MODEL // 2
S5_GROUP = 16
S5_GROUPS = S5_WIDTH // S5_GROUP
S5_STATE = 64
D_FF = ((8 * D_MODEL // 3 + 127) // 128) * 128
N_MOD = 9
ROPE_BASE = 10000.0
Q_BLOCK = 128
EPS = 1e-6
DT_MIN = 1e-3
DT_MAX = 1e-1
Q_COLS = N_HEADS * 2 * HEAD_DIM
K_COLS = N_HEADS * 2 * HEAD_DIM
V_COLS = N_HEADS * V_DIM
U_COLS = S5_WIDTH
IN_COLS = Q_COLS + K_COLS + V_COLS + U_COLS + 2 * D_MODEL
SPLIT_POINTS = (Q_COLS, Q_COLS + K_COLS, Q_COLS + K_COLS + V_COLS,
                Q_COLS + K_COLS + V_COLS + U_COLS,
                Q_COLS + K_COLS + V_COLS + U_COLS + D_MODEL)

kernel_name = "hybrid_diffattn_s5_macaron_block"


def rms_norm(x, g):
    xf = x.astype(jnp.float32)
    y = xf * lax.rsqrt(jnp.mean(xf * xf, axis=-1, keepdims=True) + EPS)
    return (y * g.astype(jnp.float32)).astype(x.dtype)


def modulate(x, g, shift, scale):
    return rms_norm(x, g) * (1.0 + scale) + shift


def swiglu(h, w13, w2):
    a, b = jnp.split(h @ w13, 2, axis=-1)
    return (jax.nn.silu(a) * b) @ w2


def split_proj(p):
    B, L, _ = p.shape
    q, k, v, u, ga, gs = jnp.split(p, SPLIT_POINTS, axis=-1)
    q = q.reshape(B, L, N_HEADS, 2, HEAD_DIM)
    k = k.reshape(B, L, N_HEADS, 2, HEAD_DIM)
    v = v.reshape(B, L, N_HEADS, V_DIM)
    return q, k, v, u, ga, gs


def axial_rope_tables(rows):
    n_freq = HEAD_DIM // 4
    inv = ROPE_BASE ** (-jnp.arange(n_freq, dtype=jnp.float32) / n_freq)
    row = jnp.repeat(jnp.arange(rows, dtype=jnp.float32), GRID_W)
    col = jnp.tile(jnp.arange(GRID_W, dtype=jnp.float32), rows)
    ang = jnp.concatenate([row[:, None] * inv, col[:, None] * inv], axis=-1)
    return jnp.cos(ang), jnp.sin(ang)


def apply_rope(t, cos, sin):
    half = HEAD_DIM // 2
    tf = t.astype(jnp.float32)
    t1, t2 = tf[..., :half], tf[..., half:]
    cs = cos[None, :, None, None, :]
    sn = sin[None, :, None, None, :]
    return jnp.concatenate([t1 * cs - t2 * sn, t2 * cs + t1 * sn], axis=-1).astype(t.dtype)


def diff_lambda(lq1, lk1, lq2, lk2, lam_init):
    f = lambda a: a.astype(jnp.float32)
    return jnp.exp(jnp.sum(f(lq1) * f(lk1))) - jnp.exp(jnp.sum(f(lq2) * f(lk2))) + lam_init


def diff_attend(q, k, v, lam):
    s = jnp.einsum('bqhid,bkhid->bhiqk', q, k,
                   preferred_element_type=jnp.float32) / math.sqrt(HEAD_DIM)
    p = jax.nn.softmax(s, axis=-1)
    a = p[:, :, 0] - lam * p[:, :, 1]
    return jnp.einsum('bhqk,bkhe->bqhe', a.astype(v.dtype), v)


def diff_attend_blocked(q, k, v, lam):
    B, L = q.shape[:2]
    nb = L // Q_BLOCK
    qb = jnp.moveaxis(q.reshape(B, nb, Q_BLOCK, N_HEADS, 2, HEAD_DIM), 1, 0)
    ob = lax.map(lambda qq: diff_attend(qq, k, v, lam), qb)
    return jnp.moveaxis(ob, 0, 1).reshape(B, L, N_HEADS, V_DIM)


def s5_discretize(lam_re, lam_im, log_dt, b_re, b_im):
    dt = jnp.exp(log_dt)[:, None]
    mag = jnp.exp(lam_re * dt)
    ar = mag * jnp.cos(lam_im * dt)
    ai = mag * jnp.sin(lam_im * dt)
    den = lam_re * lam_re + lam_im * lam_im
    nr = ar - 1.0
    fr = (nr * lam_re + ai * lam_im) / den
    fi = (ai * lam_re - nr * lam_im) / den
    bbr = fr[..., None] * b_re - fi[..., None] * b_im
    bbi = fr[..., None] * b_im + fi[..., None] * b_re
    return ar, ai, bbr, bbi


def _lin_combine(left, right):
    ar_l, ai_l, br_l, bi_l = left
    ar_r, ai_r, br_r, bi_r = right
    ar = ar_r * ar_l - ai_r * ai_l
    ai = ar_r * ai_l + ai_r * ar_l
    br = ar_r * br_l - ai_r * bi_l + br_r
    bi = ar_r * bi_l + ai_r * br_l + bi_r
    return ar, ai, br, bi


def s5_scan(u, ar, ai, bbr, bbi, h0r, h0i):
    br = jnp.einsum('blgc,gpc->blgp', u, bbr)
    bi = jnp.einsum('blgc,gpc->blgp', u, bbi)
    br = br.at[:, 0].add(ar * h0r - ai * h0i)
    bi = bi.at[:, 0].add(ar * h0i + ai * h0r)
    L = u.shape[1]
    arL = jnp.broadcast_to(ar, (1, L) + ar.shape)
    aiL = jnp.broadcast_to(ai, (1, L) + ai.shape)
    _, _, hr, hi = lax.associative_scan(_lin_combine, (arL, aiL, br, bi), axis=1)
    return hr, hi


def s5_readout(hr, hi, c_re, c_im):
    return (jnp.einsum('gcp,blgp->blgc', c_re, hr)
            - jnp.einsum('gcp,blgp->blgc', c_im, hi))


def s5_output(y, u, d_skip, w_glu, b_glu, dtype):
    B, L = y.shape[:2]
    y = y.reshape(B, L, S5_WIDTH) + d_skip.astype(jnp.float32) * u.reshape(B, L, S5_WIDTH)
    y = jax.nn.gelu(y.astype(dtype))
    return y * jax.nn.sigmoid(y @ w_glu + b_glu)


def s5_branch(u_x, u_c, lam_re, lam_im, log_dt, b_re, b_im, c_re, c_im,
              d_skip, w_glu, b_glu, with_ctx):
    f32 = jnp.float32
    B = u_x.shape[0]
    gx = u_x.astype(f32).reshape(B, u_x.shape[1], S5_GROUPS, S5_GROUP)
    gc = u_c.astype(f32).reshape(B, u_c.shape[1], S5_GROUPS, S5_GROUP)
    zeros = jnp.zeros((B, S5_GROUPS, S5_STATE), f32)
    y_x = 0.0
    y_c = 0.0
    for d in range(2):
        rev = d == 1
        ar, ai, bbr, bbi = s5_discretize(lam_re[d].astype(f32), lam_im[d].astype(f32),
                                         log_dt[d].astype(f32), b_re[d].astype(f32),
                                         b_im[d].astype(f32))
        ux_d = jnp.flip(gx, axis=1) if rev else gx
        uc_d = jnp.flip(gc, axis=1) if rev else gc
        hcr, hci = s5_scan(uc_d, ar, ai, bbr, bbi, zeros, zeros)
        hxr, hxi = s5_scan(ux_d, ar, ai, bbr, bbi, hcr[:, -1], hci[:, -1])
        cr, ci = c_re[d].astype(f32), c_im[d].astype(f32)
        yx = s5_readout(hxr, hxi, cr, ci)
        y_x = y_x + (jnp.flip(yx, axis=1) if rev else yx)
        if with_ctx:
            yc = s5_readout(hcr, hci, cr, ci)
            y_c = y_c + (jnp.flip(yc, axis=1) if rev else yc)
    out_x = s5_output(y_x, gx, d_skip, w_glu, b_glu, u_x.dtype)
    out_c = s5_output(y_c, gc, d_skip, w_glu, b_glu, u_c.dtype) if with_ctx else None
    return out_x, out_c


def merge_branches(o_attn, o_s5, g_attn, g_s5, w_pa, w_ps, w_out):
    m = jax.nn.sigmoid(g_attn) * (o_attn @ w_pa) + jax.nn.sigmoid(g_s5) * (o_s5 @ w_ps)
    return m @ w_out


def setup_inputs(seed: int = 0) -> dict:
    key = jax.random.key(seed)
    ks = iter(jax.random.split(key, 48))
    f32 = jnp.float32
    nrm = lambda shape, s: s * jax.random.normal(next(ks), shape, f32)
    gain = lambda shape: 1.0 + 0.02 * jax.random.normal(next(ks), shape, f32)
    Lz = DEPTH
    s5_shape = (Lz, 2, S5_GROUPS, S5_STATE)
    return {
        "x": nrm((BATCH, SEQ, D_MODEL), 1.0),
        "c": nrm((BATCH, D_MODEL), 1.0),
        "ctx": nrm((BATCH, CTX_LEN, D_MODEL), 1.0),
        "c_ctx": nrm((D_MODEL,), 1.0),
        "w_mod": nrm((Lz, D_MODEL, N_MOD * D_MODEL), 0.5 * D_MODEL ** -0.5),
        "b_mod": nrm((Lz, N_MOD * D_MODEL), 0.02),
        "norm_ffn1": gain((Lz, D_MODEL)),
        "w13_ffn1": nrm((Lz, D_MODEL, 2 * D_FF), D_MODEL ** -0.5),
        "w2_ffn1": nrm((Lz, D_FF, D_MODEL), D_FF ** -0.5),
        "norm_mix": gain((Lz, D_MODEL)),
        "w_in": nrm((Lz, D_MODEL, IN_COLS), D_MODEL ** -0.5),
        "q_norm": gain((Lz, HEAD_DIM)),
        "k_norm": gain((Lz, HEAD_DIM)),
        "lam_q1": nrm((Lz, HEAD_DIM), 0.1),
        "lam_k1": nrm((Lz, HEAD_DIM), 0.1),
        "lam_q2": nrm((Lz, HEAD_DIM), 0.1),
        "lam_k2": nrm((Lz, HEAD_DIM), 0.1),
        "subln": gain((Lz, V_DIM)),
        "s5_lam_re": -0.5 + nrm(s5_shape, 0.01),
        "s5_lam_im": jnp.pi * jnp.arange(S5_STATE, dtype=f32) + nrm(s5_shape, 0.01),
        "s5_log_dt": jax.random.uniform(next(ks), (Lz, 2, S5_GROUPS), f32,
                                        minval=math.log(DT_MIN), maxval=math.log(DT_MAX)),
        "s5_b_re": nrm((Lz, 2, S5_GROUPS, S5_STATE, S5_GROUP), (2 * S5_GROUP) ** -0.5),
        "s5_b_im": nrm((Lz, 2, S5_GROUPS, S5_STATE, S5_GROUP), (2 * S5_GROUP) ** -0.5),
        "s5_c_re": nrm((Lz, 2, S5_GROUPS, S5_GROUP, S5_STATE), (2 * S5_STATE) ** -0.5),
        "s5_c_im": nrm((Lz, 2, S5_GROUPS, S5_GROUP, S5_STATE), (2 * S5_STATE) ** -0.5),
        "s5_d": nrm((Lz, S5_WIDTH), 0.5),
        "w_glu": nrm((Lz, S5_WIDTH, S5_WIDTH), S5_WIDTH ** -0.5),
        "b_glu": nrm((Lz, S5_WIDTH), 0.02),
        "w_pa": nrm((Lz, ATTN_WIDTH, D_MODEL), ATTN_WIDTH ** -0.5),
        "w_ps": nrm((Lz, S5_WIDTH, D_MODEL), S5_WIDTH ** -0.5),
        "w_out": nrm((Lz, D_MODEL, D_MODEL), D_MODEL ** -0.5),
        "norm_ffn2": gain((Lz, D_MODEL)),
        "w13_ffn2": nrm((Lz, D_MODEL, 2 * D_FF), D_MODEL ** -0.5),
        "w2_ffn2": nrm((Lz, D_FF, D_MODEL), D_FF ** -0.5),
    }


def reference(x, c, ctx, c_ctx, w_mod, b_mod, norm_ffn1, w13_ffn1, w2_ffn1, norm_mix,
              w_in, q_norm, k_norm, lam_q1, lam_k1, lam_q2, lam_k2, subln,
              s5_lam_re, s5_lam_im, s5_log_dt, s5_b_re, s5_b_im, s5_c_re, s5_c_im,
              s5_d, w_glu, b_glu, w_pa, w_ps, w_out, norm_ffn2, w13_ffn2, w2_ffn2):
    B, L, _ = x.shape
    rows = L // GRID_W
    cos, sin = axial_rope_tables(rows)
    Lc = ctx.shape[1]
    for l in range(DEPTH):
        last = l == DEPTH - 1
        lam_init = 0.8 - 0.6 * math.exp(-0.3 * l)
        mod_x = jnp.split((jax.nn.silu(c) @ w_mod[l] + b_mod[l])[:, None, :], N_MOD, axis=-1)
        mod_c = jnp.split((jax.nn.silu(c_ctx) @ w_mod[l] + b_mod[l])[None, None, :], N_MOD, axis=-1)
        shx1, scx1, gx1, shx2, scx2, gx2, shx3, scx3, gx3 = mod_x
        shc1, scc1, gc1, shc2, scc2, gc2, shc3, scc3, gc3 = mod_c

        x = x + 0.5 * gx1 * swiglu(modulate(x, norm_ffn1[l], shx1, scx1), w13_ffn1[l], w2_ffn1[l])
        ctx = ctx + 0.5 * gc1 * swiglu(modulate(ctx, norm_ffn1[l], shc1, scc1), w13_ffn1[l], w2_ffn1[l])

        qx, kx, vx, ux, gax, gsx = split_proj(modulate(x, norm_mix[l], shx2, scx2) @ w_in[l])
        qc, kc, vc, uc, gac, gsc = split_proj(modulate(ctx, norm_mix[l], shc2, scc2) @ w_in[l])

        qx = apply_rope(rms_norm(qx, q_norm[l]), cos, sin)
        kx = apply_rope(rms_norm(kx, k_norm[l]), cos, sin)
        qc = rms_norm(qc, q_norm[l])
        kc = rms_norm(kc, k_norm[l])
        lam = diff_lambda(lam_q1[l], lam_k1[l], lam_q2[l], lam_k2[l], lam_init)
        k_all = jnp.concatenate([kc, kx], axis=1)
        v_all = jnp.concatenate([vc, vx], axis=1)
        ox = diff_attend_blocked(qx, k_all, v_all, lam)
        ox = (rms_norm(ox, subln[l]) * (1.0 - lam_init)).reshape(B, L, ATTN_WIDTH)

        sx, s5c = s5_branch(ux, uc, s5_lam_re[l], s5_lam_im[l], s5_log_dt[l], s5_b_re[l],
                            s5_b_im[l], s5_c_re[l], s5_c_im[l], s5_d[l], w_glu[l], b_glu[l],
                            with_ctx=not last)
        x_mix = merge_branches(ox, sx, gax, gsx, w_pa[l], w_ps[l], w_out[l])
        if not last:
            oc = diff_attend(qc, kc, vc, lam)
            oc = (rms_norm(oc, subln[l]) * (1.0 - lam_init)).reshape(B, Lc, ATTN_WIDTH)
            ctx = ctx + gc2 * merge_branches(oc, s5c, gac, gsc, w_pa[l], w_ps[l], w_out[l])
        x = x + gx2 * x_mix

        x = x + 0.5 * gx3 * swiglu(modulate(x, norm_ffn2[l], shx3, scx3), w13_ffn2[l], w2_ffn2[l])
        if not last:
            ctx = ctx + 0.5 * gc3 * swiglu(modulate(ctx, norm_ffn2[l], shc3, scc3), w13_ffn2[l], w2_ffn2[l])
    return x
```

```python
import functools
import math

import jax
import jax.numpy as jnp
from jax import lax
from jax.experimental import pallas as pl
from jax.experimental.pallas import tpu as pltpu

F32 = jnp.float32
BF16 = jnp.bfloat16

EPS = 1e-6
ROPE_BASE = 10000.0
GRID_W = 64
N_HEADS = 8
HEAD_DIM = 64
V_DIM = 2 * HEAD_DIM
S5_GROUP = 16
S5_STATE = 64
N_MOD = 9
LANES = 128
CHUNK = 8
SUPER = LANES // S5_GROUP
VMEM_LIMIT = 56 * 1024 * 1024


def _cparams(sem):
    return pltpu.CompilerParams(dimension_semantics=sem, vmem_limit_bytes=VMEM_LIMIT)


def _resident(shape):
    nd = len(shape)
    return pl.BlockSpec(shape, lambda *_: (0,) * nd, pipeline_mode=pl.Buffered(1))


def _silu(a):
    return a * jax.nn.sigmoid(a)


def _modulated(x, g, mod, base):
    ms = jnp.mean(x * x, axis=-1, keepdims=True)
    xn = x * lax.rsqrt(ms + EPS) * g
    return xn * (1.0 + mod[base + 1:base + 2, :]) + mod[base:base + 1, :]


def _mod_kernel(c_ref, w_ref, b_ref, o_ref):
    a = _silu(c_ref[...]).astype(BF16)
    o_ref[...] = jnp.dot(a, w_ref[...].astype(BF16), preferred_element_type=F32) + b_ref[...]


def _mod_call(cc, w_mod, b_mod):
    rows, d = cc.shape
    n = w_mod.shape[1]
    tn = d
    return pl.pallas_call(
        _mod_kernel,
        out_shape=jax.ShapeDtypeStruct((rows, n), F32),
        grid=(n // tn,),
        in_specs=[pl.BlockSpec((rows, d), lambda j: (0, 0)),
                  pl.BlockSpec((d, tn), lambda j: (0, j)),
                  pl.BlockSpec((1, tn), lambda j: (0, j))],
        out_specs=pl.BlockSpec((rows, tn), lambda j: (0, j)),
        compiler_params=_cparams(("arbitrary",)),
        name="mod",
    )(cc, w_mod, b_mod)


def _ffn_kernel(x_ref, mod_ref, g_ref, w13_ref, w2_ref, o_ref, *, base):
    x = x_ref[0]
    mod = mod_ref[0]
    xm = _modulated(x, g_ref[...], mod, base).astype(BF16)
    h = jnp.dot(xm, w13_ref[...], preferred_element_type=F32)
    dff = w2_ref.shape[0]
    act = (_silu(h[:, :dff]) * h[:, dff:]).astype(BF16)
    y = jnp.dot(act, w2_ref[...], preferred_element_type=F32)
    o_ref[0] = x + (0.5 * mod[base + 2:base + 3, :]) * y


def _ffn_call(x, mod, g, w13, w2, base, tm, name):
    b, l, d = x.shape
    mod_map = (lambda bi, i: (bi, 0, 0)) if mod.shape[0] == b else (lambda bi, i: (0, 0, 0))
    return pl.pallas_call(
        functools.partial(_ffn_kernel, base=base),
        out_shape=jax.ShapeDtypeStruct(x.shape, F32),
        grid=(b, l // tm),
        in_specs=[pl.BlockSpec((1, tm, d), lambda bi, i: (bi, i, 0)),
                  pl.BlockSpec((1, N_MOD, d), mod_map),
                  _resident(g.shape), _resident(w13.shape), _resident(w2.shape)],
        out_specs=pl.BlockSpec((1, tm, d), lambda bi, i: (bi, i, 0)),
        compiler_params=_cparams(("parallel", "parallel")),
        name=name,
    )(x, mod, g, w13, w2)


def _qk_norm(t, gain, e1_ref, e2_ref):
    ss = jnp.dot((t * t).astype(BF16), e1_ref[...], preferred_element_type=F32)
    r = lax.rsqrt(ss * (1.0 / HEAD_DIM) + EPS)
    r_hi = r.astype(BF16)
    r_lo = (r - r_hi.astype(F32)).astype(BF16)
    rb = jnp.dot(jnp.concatenate([r_hi, r_lo], axis=-1), e2_ref[...], preferred_element_type=F32)
    return t * rb * gain


def _rope(t, cos, sin_signed):
    n = t.shape[-1]
    half = HEAD_DIM // 2
    lane = lax.broadcasted_iota(jnp.int32, t.shape, 1)
    swapped = jnp.where(lane % HEAD_DIM < half, pltpu.roll(t, n - half, 1), pltpu.roll(t, half, 1))
    reps = n // cos.shape[-1]
    cosf = jnp.concatenate([cos] * reps, axis=-1)
    sinf = jnp.concatenate([sin_signed] * reps, axis=-1)
    return t * cosf + swapped * sinf


def _inproj_kernel(ctx_ref, x_ref, modc_ref, modx_ref, g_ref, w_ref, qg_ref, kg_ref, e1_ref, e2_ref,
                   cos_ref, sin_ref, q_ref, k_ref, vt_ref, u_ref, ga_ref, gs_ref, *, nc, base):
    i = pl.program_id(1)
    d = x_ref.shape[-1]
    qc, kc, vc = d, 2 * d, 3 * d
    sg = u_ref.shape[1]
    uc = vc + sg * LANES

    def store_u(u):
        for g in range(sg):
            u_ref[0, g] = u[:, g * LANES:(g + 1) * LANES]

    @pl.when(i < nc)
    def _context():
        xm = _modulated(ctx_ref[0], g_ref[...], modc_ref[0], base).astype(BF16)
        p = jnp.dot(xm, w_ref[:, qc:uc], preferred_element_type=F32)
        k_ref[0] = _qk_norm(p[:, :d], kg_ref[...], e1_ref, e2_ref).astype(BF16)
        vt_ref[0] = p[:, d:2 * d].T.astype(BF16)
        store_u(p[:, 2 * d:])

    @pl.when(i >= nc)
    def _latent():
        xm = _modulated(x_ref[0], g_ref[...], modx_ref[0], base).astype(BF16)
        p = jnp.dot(xm, w_ref[...], preferred_element_type=F32)
        cos = cos_ref[...]
        sin = sin_ref[...]
        q = _rope(_qk_norm(p[:, :qc], qg_ref[...], e1_ref, e2_ref), cos, sin)
        k = _rope(_qk_norm(p[:, qc:kc], kg_ref[...], e1_ref, e2_ref), cos, sin)
        q_ref[0] = q.astype(BF16)
        k_ref[0] = k.astype(BF16)
        vt_ref[0] = p[:, kc:vc].T.astype(BF16)
        store_u(p[:, vc:uc])
        ga_ref[0] = jax.nn.sigmoid(p[:, uc:uc + d]).astype(BF16)
        gs_ref[0] = jax.nn.sigmoid(p[:, uc + d:]).astype(BF16)


def _inproj_call(ctx, x, modc, modx, g, w_in, qg, kg, e1, e2, cos, sin, s5w, base, tm):
    b, l, d = x.shape
    lc = ctx.shape[1]
    nc, nx = lc // tm, l // tm
    la = lc + l
    lat = lambda bi, i: (bi, jnp.maximum(i - nc, 0), 0)
    every = lambda bi, i: (bi, i, 0)
    out_shape = (jax.ShapeDtypeStruct((b, l, d), BF16),
                 jax.ShapeDtypeStruct((b, la, d), BF16),
                 jax.ShapeDtypeStruct((b, d, la), BF16),
                 jax.ShapeDtypeStruct((b, s5w // LANES, la, LANES), F32),
                 jax.ShapeDtypeStruct((b, l, d), BF16),
                 jax.ShapeDtypeStruct((b, l, d), BF16))
    return pl.pallas_call(
        functools.partial(_inproj_kernel, nc=nc, base=base),
        out_shape=out_shape,
        grid=(b, nc + nx),
        in_specs=[pl.BlockSpec((1, tm, d), lambda bi, i: (bi, jnp.minimum(i, nc - 1), 0)),
                  pl.BlockSpec((1, tm, d), lat),
                  pl.BlockSpec((1, N_MOD, d), lambda bi, i: (0, 0, 0)),
                  pl.BlockSpec((1, N_MOD, d), lambda bi, i: (bi, 0, 0)),
                  _resident(g.shape), _resident(w_in.shape), _resident(qg.shape), _resident(kg.shape),
                  _resident(e1.shape), _resident(e2.shape),
                  pl.BlockSpec((tm, LANES), lambda bi, i: (jnp.maximum(i - nc, 0), 0)),
                  pl.BlockSpec((tm, LANES), lambda bi, i: (jnp.maximum(i - nc, 0), 0))],
        out_specs=(pl.BlockSpec((1, tm, d), lat),
                   pl.BlockSpec((1, tm, d), every),
                   pl.BlockSpec((1, d, tm), lambda bi, i: (bi, 0, i)),
                   pl.BlockSpec((1, s5w // LANES, tm, LANES), lambda bi, i: (bi, 0, i, 0)),
                   pl.BlockSpec((1, tm, d), lat),
                   pl.BlockSpec((1, tm, d), lat)),
        compiler_params=_cparams(("parallel", "arbitrary")),
        name="inproj",
    )(ctx, x, modc, modx, g, w_in, qg, kg, e1, e2, cos, sin)


def _attn_kernel(lam_ref, subln_ref, q_ref, k_ref, vt_ref, o_ref, *, lc, tk, lam_init):
    q = q_ref[0]
    la = k_ref.shape[1]
    lane = lax.broadcasted_iota(jnp.int32, q.shape, 1)
    zero = jnp.zeros_like(q)
    q_comp = (jnp.where(lane < HEAD_DIM, q, zero), jnp.where(lane >= HEAD_DIM, q, zero))
    lv = lam_ref[...]
    lam = (jnp.exp(jnp.sum(lv[0:1] * lv[1:2], axis=-1, keepdims=True))
           - jnp.exp(jnp.sum(lv[2:3] * lv[3:4], axis=-1, keepdims=True)) + lam_init)
    blocks = [(0, lc)] + [(st, tk) for st in range(lc, la, tk)]
    heads = []
    for qz in q_comp:
        m = l = acc = None
        for st, sz in blocks:
            kb = k_ref[0, st:st + sz, :]
            vb = vt_ref[0, :, st:st + sz]
            s = lax.dot_general(kb, qz, (((1,), (1,)), ((), ())), preferred_element_type=F32)
            bm = jnp.max(s, axis=0, keepdims=True)
            if m is None:
                m_new = bm
                p = jnp.exp(s - m_new)
                l = jnp.sum(p, axis=0, keepdims=True)
                acc = jnp.dot(vb, p.astype(BF16), preferred_element_type=F32)
            else:
                m_new = jnp.maximum(m, bm)
                alpha = jnp.exp(m - m_new)
                p = jnp.exp(s - m_new)
                l = alpha * l + jnp.sum(p, axis=0, keepdims=True)
                acc = alpha * acc + jnp.dot(vb, p.astype(BF16), preferred_element_type=F32)
            m = m_new
        heads.append(acc / l)
    o = heads[0] - lam * heads[1]
    ms = jnp.mean(o * o, axis=0, keepdims=True)
    on = o * lax.rsqrt(ms + EPS) * (subln_ref[...] * (1.0 - lam_init))
    o_ref[0] = on.T.astype(BF16)


def _attn_call(lamv, subln, q, k, vt, lc, lam_init, tq, tk):
    b, l, d = q.shape
    la = k.shape[1]
    return pl.pallas_call(
        functools.partial(_attn_kernel, lc=lc, tk=tk, lam_init=lam_init),
        out_shape=jax.ShapeDtypeStruct((b, l, d), BF16),
        grid=(b, N_HEADS, l // tq),
        in_specs=[pl.BlockSpec(lamv.shape, lambda bi, h, i: (0, 0)),
                  pl.BlockSpec(subln.shape, lambda bi, h, i: (0, 0)),
                  pl.BlockSpec((1, tq, V_DIM), lambda bi, h, i: (bi, i, h)),
                  pl.BlockSpec((1, la, V_DIM), lambda bi, h, i: (bi, 0, h)),
                  pl.BlockSpec((1, V_DIM, la), lambda bi, h, i: (bi, h, 0))],
        out_specs=pl.BlockSpec((1, tq, V_DIM), lambda bi, h, i: (bi, i, h)),
        compiler_params=_cparams(("parallel", "parallel", "arbitrary")),
        name="attn",
    )(lamv, subln, q, k, vt)


def _s5_matrices(lam_re, lam_im, log_dt, b_re, b_im, c_re, c_im, d_skip):
    t = CHUNK
    g_all, p_n = lam_re.shape[1:]
    c_n = b_re.shape[-1]
    sg = g_all // SUPER
    eye = jnp.eye(SUPER, dtype=F32)
    ks = jnp.arange(t + 1, dtype=F32)[:, None, None]
    mats = []
    for dr in range(2):
        lre, lim = lam_re[dr], lam_im[dr]
        dt = jnp.exp(log_dt[dr])[:, None]
        mag = jnp.exp(ks * (lre * dt))
        akr = mag * jnp.cos(ks * (lim * dt))
        aki = mag * jnp.sin(ks * (lim * dt))
        ar, ai = akr[1], aki[1]
        den = lre * lre + lim * lim
        nr = ar - 1.0
        fr = (nr * lre + ai * lim) / den
        fi = (ai * lre - nr * lim) / den
        bbr = fr[..., None] * b_re[dr] - fi[..., None] * b_im[dr]
        bbi = fr[..., None] * b_im[dr] + fi[..., None] * b_re[dr]
        abr = akr[..., None] * bbr - aki[..., None] * bbi
        abi = akr[..., None] * bbi + aki[..., None] * bbr
        kern = (jnp.einsum('gdp,kgpc->kgdc', c_re[dr], abr)
                - jnp.einsum('gdp,kgpc->kgdc', c_im[dr], abi))
        car = c_re[dr][None] * akr[:, :, None, :] - c_im[dr][None] * aki[:, :, None, :]
        cai = c_re[dr][None] * aki[:, :, None, :] + c_im[dr][None] * akr[:, :, None, :]
        mats.append((akr, aki, abr, abi, kern, car, cai))

    s_idx = jnp.arange(t)
    lag = s_idx[None, :] - s_idx[:, None]
    kf, kb = mats[0][4], mats[1][4]
    d_diag = jnp.eye(c_n, dtype=F32)[None] * d_skip.reshape(g_all, 1, c_n)
    kf_l = kf[jnp.clip(lag, 0, t)]
    kb_l = kb[jnp.clip(-lag, 0, t)]
    fwd_on = (lag >= 0)[..., None, None, None]
    bwd_on = (lag <= 0)[..., None, None, None]
    same = (lag == 0)[..., None, None, None]
    mg = jnp.where(fwd_on, kf_l, 0.0) + jnp.where(bwd_on, kb_l, 0.0) + jnp.where(same, d_diag[None, None], 0.0)
    mg = mg.reshape(t, t, sg, SUPER, c_n, c_n)
    m = jnp.einsum('stGldc,lm->Gslctmd', mg, eye)
    m = m.reshape(sg, t * SUPER * c_n, t * SUPER * c_n)

    def state_in(abr, abi, order):
        wr = abr[order].reshape(t, sg, SUPER, p_n, c_n)
        wi = abi[order].reshape(t, sg, SUPER, p_n, c_n)
        w = jnp.stack([wr, wi], axis=0)
        w = jnp.einsum('rsGlpc,lm->Gslcrmp', w, eye)
        return w.reshape(sg, t * SUPER * c_n, 2 * SUPER * p_n)

    def state_out(car, cai, order):
        vr = car[order].reshape(t, sg, SUPER, c_n, p_n)
        vi = -cai[order].reshape(t, sg, SUPER, c_n, p_n)
        v = jnp.stack([vr, vi], axis=0)
        v = jnp.einsum('rtGldp,lm->Grlptmd', v, eye)
        return v.reshape(sg, 2 * SUPER * p_n, t * SUPER * c_n)

    wf = state_in(mats[0][2], mats[0][3], (t - 1) - s_idx)
    wb = state_in(mats[1][2], mats[1][3], s_idx)
    vf = state_out(mats[0][5], mats[0][6], s_idx + 1)
    vb = state_out(mats[1][5], mats[1][6], t - s_idx)

    def decay(akr, aki):
        return jnp.stack([akr[t].reshape(sg, SUPER * p_n), aki[t].reshape(sg, SUPER * p_n)], axis=1)

    bf = lambda a: a.astype(BF16)
    return (bf(m), bf(wf), bf(wb), bf(vf), bf(vb),
            decay(mats[0][0], mats[0][1]), decay(mats[1][0], mats[1][1]))


def _s5_kernel(z_ref, m_ref, wf_ref, wb_ref, vf_ref, vb_ref, af_ref, ab_ref, y_ref, sf_ref, sb_ref, *,
               nchunk_c):
    n = sf_ref.shape[0]
    half = sf_ref.shape[1] // 2
    z = z_ref[0, 0].astype(BF16)
    sf_ref[...] = jnp.dot(z, wf_ref[0], preferred_element_type=F32)
    sb_ref[...] = jnp.dot(z, wb_ref[0], preferred_element_type=F32)
    afr, afi = af_ref[0, 0:1, :], af_ref[0, 1:2, :]
    abr, abi = ab_ref[0, 0:1, :], ab_ref[0, 1:2, :]

    def step(k, carry):
        hr, hi, gr, gi = carry
        row = sf_ref[pl.ds(k, 1), :]
        sf_ref[pl.ds(k, 1), :] = jnp.concatenate([hr, hi], axis=-1)
        hr, hi = afr * hr - afi * hi + row[:, :half], afr * hi + afi * hr + row[:, half:]
        kb = jnp.where(k < nchunk_c, nchunk_c - 1 - k, n + nchunk_c - 1 - k)
        row = sb_ref[pl.ds(kb, 1), :]
        sb_ref[pl.ds(kb, 1), :] = jnp.concatenate([gr, gi], axis=-1)
        gr, gi = abr * gr - abi * gi + row[:, :half], abr * gi + abi * gr + row[:, half:]
        return hr, hi, gr, gi

    zero = jnp.zeros((1, half), F32)
    lax.fori_loop(0, n, step, (zero, zero, zero, zero))

    y_ref[0, 0] = (jnp.dot(z[nchunk_c:], m_ref[0], preferred_element_type=F32)
                   + jnp.dot(sf_ref[nchunk_c:, :].astype(BF16), vf_ref[0], preferred_element_type=F32)
                   + jnp.dot(sb_ref[nchunk_c:, :].astype(BF16), vb_ref[0], preferred_element_type=F32))


def _s5_call(u, mats, lc):
    b, sg, la, _ = u.shape
    t = CHUNK
    n, nchunk_c = la // t, lc // t
    nx = n - nchunk_c
    m, wf, wb, vf, vb, a8f, a8b = mats
    uz = u.reshape(b, sg, n, t * LANES)
    wspec = lambda a: pl.BlockSpec((1,) + a.shape[1:], lambda g, bi: (g, 0, 0))
    y = pl.pallas_call(
        functools.partial(_s5_kernel, nchunk_c=nchunk_c),
        out_shape=jax.ShapeDtypeStruct((b, sg, nx, t * LANES), F32),
        grid=(sg, b),
        in_specs=[pl.BlockSpec((1, 1, n, t * LANES), lambda g, bi: (bi, g, 0, 0))]
                 + [wspec(a) for a in (m, wf, wb, vf, vb, a8f, a8b)],
        out_specs=pl.BlockSpec((1, 1, nx, t * LANES), lambda g, bi: (bi, g, 0, 0)),
        scratch_shapes=[pltpu.VMEM((n, t * LANES), F32), pltpu.VMEM((n, t * LANES), F32)],
        compiler_params=_cparams(("arbitrary", "arbitrary")),
        name="s5",
    )(uz, m, wf, wb, vf, vb, a8f, a8b)
    return y.reshape(b, sg, nx * t, LANES)


def _gelu_tanh(y):
    return 0.5 * y * (1.0 + jnp.tanh(math.sqrt(2.0 / math.pi) * (y + 0.044715 * (y * y * y))))


def _merge_kernel(x_ref, mod_ref, y_ref, ox_ref, ga_ref, gs_ref, wglu_ref, bglu_ref, wpa_ref, wps_ref,
                  wout_ref, o_ref, *, base):
    ge = _gelu_tanh(jnp.concatenate([y_ref[0, g] for g in range(y_ref.shape[1])], axis=-1))
    z = jnp.dot(ge.astype(BF16), wglu_ref[...], preferred_element_type=F32) + bglu_ref[...]
    sx = (ge * jax.nn.sigmoid(z)).astype(BF16)
    mix = (ga_ref[0].astype(F32) * jnp.dot(ox_ref[0], wpa_ref[...], preferred_element_type=F32)
           + gs_ref[0].astype(F32) * jnp.dot(sx, wps_ref[...], preferred_element_type=F32))
    out = jnp.dot(mix.astype(BF16), wout_ref[...], preferred_element_type=F32)
    o_ref[0] = x_ref[0] + mod_ref[0][base + 2:base + 3, :] * out


def _merge_call(x, mod, y, ox, ga, gs, w_glu, b_glu, w_pa, w_ps, w_out, base, tm):
    b, l, d = x.shape
    sg = y.shape[1]
    tok = lambda width: pl.BlockSpec((1, tm, width), lambda bi, i: (bi, i, 0))
    return pl.pallas_call(
        functools.partial(_merge_kernel, base=base),
        out_shape=jax.ShapeDtypeStruct(x.shape, F32),
        grid=(b, l // tm),
        in_specs=[tok(d), pl.BlockSpec((1, N_MOD, d), lambda bi, i: (bi, 0, 0)),
                  pl.BlockSpec((1, sg, tm, LANES), lambda bi, i: (bi, 0, i, 0)), tok(d), tok(d), tok(d),
                  _resident(w_glu.shape), _resident(b_glu.shape), _resident(w_pa.shape),
                  _resident(w_ps.shape), _resident(w_out.shape)],
        out_specs=tok(d),
        compiler_params=_cparams(("parallel", "parallel")),
        name="merge",
    )(x, mod, y, ox, ga, gs, w_glu, b_glu, w_pa, w_ps, w_out)


def _rope_tables(l):
    n_freq = HEAD_DIM // 4
    inv = ROPE_BASE ** (-jnp.arange(n_freq, dtype=F32) / n_freq)
    pos = jnp.arange(l, dtype=jnp.int32)
    row = (pos // GRID_W).astype(F32)
    col = (pos % GRID_W).astype(F32)
    ang = jnp.concatenate([row[:, None] * inv, col[:, None] * inv], axis=-1)
    cos, sin = jnp.cos(ang), jnp.sin(ang)
    return jnp.concatenate([cos] * 4, axis=-1), jnp.concatenate([-sin, sin, -sin, sin], axis=-1)


def _group_sum_matrices(d):
    grp = jnp.arange(d) // HEAD_DIM
    e1 = (grp[:, None] == jnp.arange(LANES)[None, :]).astype(BF16)
    e2 = jnp.concatenate([e1.T, e1.T], axis=0)
    return e1, e2


def kernel(x, c, ctx, c_ctx, w_mod, b_mod, norm_ffn1, w13_ffn1, w2_ffn1, norm_mix, w_in, q_norm, k_norm, lam_q1, lam_k1, lam_q2, lam_k2, subln, s5_lam_re, s5_lam_im, s5_log_dt, s5_b_re, s5_b_im, s5_c_re, s5_c_im, s5_d, w_glu, b_glu, w_pa, w_ps, w_out, norm_ffn2, w13_ffn2, w2_ffn2):
    assert w_mod.shape[0] == 1, "single-layer block"
    b, l, d = x.shape
    lc = ctx.shape[1]
    lam_init = 0.8 - 0.6 * math.exp(-0.3 * 0)
    bf = lambda a: a.astype(BF16)
    row = lambda a: a.reshape(1, -1)

    rows = b + 1
    pad = (-rows) % 8
    cc = jnp.concatenate([c, c_ctx[None, :], jnp.zeros((pad, d), F32)], axis=0)
    mod = _mod_call(cc, w_mod[0], row(b_mod[0]))
    modx = mod[:b].reshape(b, N_MOD, d)
    modc = mod[b:b + 1].reshape(1, N_MOD, d)

    w13_1, w2_1 = bf(w13_ffn1[0]), bf(w2_ffn1[0])
    x1 = _ffn_call(x, modx, row(norm_ffn1[0]), w13_1, w2_1, 0, 512, "ffn1_x")
    ctx1 = _ffn_call(ctx, modc, row(norm_ffn1[0]), w13_1, w2_1, 0, lc, "ffn1_ctx")

    e1, e2 = _group_sum_matrices(d)
    cos, sin = _rope_tables(l)
    qg = row(jnp.tile(q_norm[0], d // HEAD_DIM)) * (1.0 / math.sqrt(HEAD_DIM))
    kg = row(jnp.tile(k_norm[0], d // HEAD_DIM))
    q, k, vt, u, ga, gs = _inproj_call(ctx1, x1, modc, modx, row(norm_mix[0]), bf(w_in[0]), qg, kg, e1, e2,
                                       cos, sin, s5_d.shape[-1], 3, 256)

    lamv = jnp.stack([lam_q1[0], lam_k1[0], lam_q2[0], lam_k2[0]], axis=0)
    ox = _attn_call(lamv, subln[0].reshape(-1, 1), q, k, vt, lc, lam_init, 256, 1024)

    mats = _s5_matrices(s5_lam_re[0], s5_lam_im[0], s5_log_dt[0], s5_b_re[0], s5_b_im[0],
                        s5_c_re[0], s5_c_im[0], s5_d[0])
    y = _s5_call(u, mats, lc)

    x2 = _merge_call(x1, modx, y, ox, ga, gs, bf(w_glu[0]), row(b_glu[0]), bf(w_pa[0]), bf(w_ps[0]),
                     bf(w_out[0]), 3, 256)
    return _ffn_call(x2, modx, row(norm_ffn2[0]), bf(w13_ffn2[0]), bf(w2_ffn2[0]), 6, 512, "ffn2_x")
```

```python
import functools
import math

import jax
import jax.numpy as jnp
from jax import lax
from jax.experimental import pallas as pl
from jax.experimental.pallas import tpu as pltpu

F32 = jnp.float32
BF16 = jnp.bfloat16

EPS = 1e-6
ROPE_BASE = 10000.0
GRID_W = 64
N_HEADS = 8
HEAD_DIM = 64
V_DIM = 2 * HEAD_DIM
S5_GROUP = 16
S5_STATE = 64
N_MOD = 9
LANES = 128
CHUNK = 8
SUPER = LANES // S5_GROUP
VMEM_LIMIT = 56 * 1024 * 1024
LOG2E = math.log2(math.e)
MAX_EXP2_SPAN = 100.0


def _cparams(sem):
    return pltpu.CompilerParams(dimension_semantics=sem, vmem_limit_bytes=VMEM_LIMIT)


def _resident(shape):
    nd = len(shape)
    return pl.BlockSpec(shape, lambda *_: (0,) * nd, pipeline_mode=pl.Buffered(1))


def _silu(a):
    return a * jax.nn.sigmoid(a)


def _modulated(x, g, mod, base):
    ms = jnp.mean(x * x, axis=-1, keepdims=True)
    xn = x * lax.rsqrt(ms + EPS) * g
    return xn * (1.0 + mod[base + 1:base + 2, :]) + mod[base:base + 1, :]


def _mod_kernel(c_ref, w_ref, b_ref, o_ref):
    a = _silu(c_ref[...]).astype(BF16)
    o_ref[...] = jnp.dot(a, w_ref[...].astype(BF16), preferred_element_type=F32) + b_ref[...]


def _mod_call(cc, w_mod, b_mod):
    rows, d = cc.shape
    n = w_mod.shape[1]
    tn = d
    return pl.pallas_call(
        _mod_kernel,
        out_shape=jax.ShapeDtypeStruct((rows, n), F32),
        grid=(n // tn,),
        in_specs=[pl.BlockSpec((rows, d), lambda j: (0, 0)),
                  pl.BlockSpec((d, tn), lambda j: (0, j)),
                  pl.BlockSpec((1, tn), lambda j: (0, j))],
        out_specs=pl.BlockSpec((rows, tn), lambda j: (0, j)),
        compiler_params=_cparams(("arbitrary",)),
        name="mod",
    )(cc, w_mod, b_mod)


def _ffn_kernel(x_ref, mod_ref, g_ref, w13_ref, w2_ref, o_ref, *, base):
    x = x_ref[0]
    mod = mod_ref[0]
    xm = _modulated(x, g_ref[...], mod, base).astype(BF16)
    h = jnp.dot(xm, w13_ref[...], preferred_element_type=F32)
    dff = w2_ref.shape[0]
    act = (_silu(h[:, :dff]) * h[:, dff:]).astype(BF16)
    y = jnp.dot(act, w2_ref[...], preferred_element_type=F32)
    o_ref[0] = x + (0.5 * mod[base + 2:base + 3, :]) * y


def _ffn_call(x, mod, g, w13, w2, base, tm, name):
    b, l, d = x.shape
    mod_map = (lambda bi, i: (bi, 0, 0)) if mod.shape[0] == b else (lambda bi, i: (0, 0, 0))
    return pl.pallas_call(
        functools.partial(_ffn_kernel, base=base),
        out_shape=jax.ShapeDtypeStruct(x.shape, F32),
        grid=(b, l // tm),
        in_specs=[pl.BlockSpec((1, tm, d), lambda bi, i: (bi, i, 0)),
                  pl.BlockSpec((1, N_MOD, d), mod_map),
                  _resident(g.shape), _resident(w13.shape), _resident(w2.shape)],
        out_specs=pl.BlockSpec((1, tm, d), lambda bi, i: (bi, i, 0)),
        compiler_params=_cparams(("parallel", "parallel")),
        name=name,
    )(x, mod, g, w13, w2)


def _qk_norm(t, gain, e1_ref, e2_ref):
    ss = jnp.dot((t * t).astype(BF16), e1_ref[...], preferred_element_type=F32)
    r = lax.rsqrt(ss * (1.0 / HEAD_DIM) + EPS)
    r_hi = r.astype(BF16)
    r_lo = (r - r_hi.astype(F32)).astype(BF16)
    rb = jnp.dot(jnp.concatenate([r_hi, r_lo], axis=-1), e2_ref[...], preferred_element_type=F32)
    return t * rb * gain


def _rope(t, cos, sin_signed):
    n = t.shape[-1]
    half = HEAD_DIM // 2
    lane = lax.broadcasted_iota(jnp.int32, t.shape, 1)
    swapped = jnp.where(lane % HEAD_DIM < half, pltpu.roll(t, n - half, 1), pltpu.roll(t, half, 1))
    reps = n // cos.shape[-1]
    cosf = jnp.concatenate([cos] * reps, axis=-1)
    sinf = jnp.concatenate([sin_signed] * reps, axis=-1)
    return t * cosf + swapped * sinf


def _inproj_kernel(ctx_ref, x_ref, modc_ref, modx_ref, g_ref, w_ref, qg_ref, kg_ref, e1_ref, e2_ref,
                   cos_ref, sin_ref, q_ref, k_ref, vt_ref, u_ref, ga_ref, gs_ref, *, nc, base):
    i = pl.program_id(1)
    d = x_ref.shape[-1]
    qc, kc, vc = d, 2 * d, 3 * d
    sg = u_ref.shape[1]
    uc = vc + sg * LANES

    def store_u(u):
        for g in range(sg):
            u_ref[0, g] = u[:, g * LANES:(g + 1) * LANES]

    @pl.when(i < nc)
    def _context():
        xm = _modulated(ctx_ref[0], g_ref[...], modc_ref[0], base).astype(BF16)
        p = jnp.dot(xm, w_ref[:, qc:uc], preferred_element_type=F32)
        k_ref[0] = _qk_norm(p[:, :d], kg_ref[...], e1_ref, e2_ref).astype(BF16)
        vt_ref[0] = p[:, d:2 * d].T.astype(BF16)
        store_u(p[:, 2 * d:])

    @pl.when(i >= nc)
    def _latent():
        xm = _modulated(x_ref[0], g_ref[...], modx_ref[0], base).astype(BF16)
        p = jnp.dot(xm, w_ref[...], preferred_element_type=F32)
        cos = cos_ref[...]
        sin = sin_ref[...]
        q = _rope(_qk_norm(p[:, :qc], qg_ref[...], e1_ref, e2_ref), cos, sin)
        k = _rope(_qk_norm(p[:, qc:kc], kg_ref[...], e1_ref, e2_ref), cos, sin)
        q_ref[0] = q.astype(BF16)
        k_ref[0] = k.astype(BF16)
        vt_ref[0] = p[:, kc:vc].T.astype(BF16)
        store_u(p[:, vc:uc])
        ga_ref[0] = jax.nn.sigmoid(p[:, uc:uc + d]).astype(BF16)
        gs_ref[0] = jax.nn.sigmoid(p[:, uc + d:]).astype(BF16)


def _inproj_call(ctx, x, modc, modx, g, w_in, qg, kg, e1, e2, cos, sin, s5w, base, tm):
    b, l, d = x.shape
    lc = ctx.shape[1]
    nc, nx = lc // tm, l // tm
    la = lc + l
    lat = lambda bi, i: (bi, jnp.maximum(i - nc, 0), 0)
    every = lambda bi, i: (bi, i, 0)
    out_shape = (jax.ShapeDtypeStruct((b, l, d), BF16),
                 jax.ShapeDtypeStruct((b, la, d), BF16),
                 jax.ShapeDtypeStruct((b, d, la), BF16),
                 jax.ShapeDtypeStruct((b, s5w // LANES, la, LANES), F32),
                 jax.ShapeDtypeStruct((b, l, d), BF16),
                 jax.ShapeDtypeStruct((b, l, d), BF16))
    return pl.pallas_call(
        functools.partial(_inproj_kernel, nc=nc, base=base),
        out_shape=out_shape,
        grid=(b, nc + nx),
        in_specs=[pl.BlockSpec((1, tm, d), lambda bi, i: (bi, jnp.minimum(i, nc - 1), 0)),
                  pl.BlockSpec((1, tm, d), lat),
                  pl.BlockSpec((1, N_MOD, d), lambda bi, i: (0, 0, 0)),
                  pl.BlockSpec((1, N_MOD, d), lambda bi, i: (bi, 0, 0)),
                  _resident(g.shape), _resident(w_in.shape), _resident(qg.shape), _resident(kg.shape),
                  _resident(e1.shape), _resident(e2.shape),
                  pl.BlockSpec((tm, LANES), lambda bi, i: (jnp.maximum(i - nc, 0), 0)),
                  pl.BlockSpec((tm, LANES), lambda bi, i: (jnp.maximum(i - nc, 0), 0))],
        out_specs=(pl.BlockSpec((1, tm, d), lat),
                   pl.BlockSpec((1, tm, d), every),
                   pl.BlockSpec((1, d, tm), lambda bi, i: (bi, 0, i)),
                   pl.BlockSpec((1, s5w // LANES, tm, LANES), lambda bi, i: (bi, 0, i, 0)),
                   pl.BlockSpec((1, tm, d), lat),
                   pl.BlockSpec((1, tm, d), lat)),
        compiler_params=_cparams(("parallel", "arbitrary")),
        name="inproj",
    )(ctx, x, modc, modx, g, w_in, qg, kg, e1, e2, cos, sin)


def _attn_kernel(bound_ref, lam_ref, subln_ref, q_ref, k_ref, vt_ref, o_ref, *, lc, tk, lam_init, bounded):
    q = q_ref[0]
    la = k_ref.shape[1]
    lane = lax.broadcasted_iota(jnp.int32, q.shape, 1)
    zero = jnp.zeros_like(q)
    q_comp = (jnp.where(lane < HEAD_DIM, q, zero), jnp.where(lane >= HEAD_DIM, q, zero))
    lv = lam_ref[...]
    lam = (jnp.exp(jnp.sum(lv[0:1] * lv[1:2], axis=-1, keepdims=True))
           - jnp.exp(jnp.sum(lv[2:3] * lv[3:4], axis=-1, keepdims=True)) + lam_init)
    blocks = [(0, lc)] + [(st, tk) for st in range(lc, la, tk)]
    heads = []
    for qz in q_comp:
        m = l = acc = None
        for st, sz in blocks:
            kb = k_ref[0, st:st + sz, :]
            vb = vt_ref[0, :, st:st + sz]
            s = lax.dot_general(kb, qz, (((1,), (1,)), ((), ())), preferred_element_type=F32)
            if bounded:
                p = jnp.exp2(s - bound_ref[...])
                lb = jnp.sum(p, axis=0, keepdims=True)
                ab = jnp.dot(vb, p.astype(BF16), preferred_element_type=F32)
                l, acc = (lb, ab) if l is None else (l + lb, acc + ab)
                continue
            bm = jnp.max(s, axis=0, keepdims=True)
            if m is None:
                m_new = bm
                p = jnp.exp2(s - m_new)
                l = jnp.sum(p, axis=0, keepdims=True)
                acc = jnp.dot(vb, p.astype(BF16), preferred_element_type=F32)
            else:
                m_new = jnp.maximum(m, bm)
                alpha = jnp.exp2(m - m_new)
                p = jnp.exp2(s - m_new)
                l = alpha * l + jnp.sum(p, axis=0, keepdims=True)
                acc = alpha * acc + jnp.dot(vb, p.astype(BF16), preferred_element_type=F32)
            m = m_new
        heads.append(acc / l)
    o = heads[0] - lam * heads[1]
    ms = jnp.mean(o * o, axis=0, keepdims=True)
    on = o * lax.rsqrt(ms + EPS) * (subln_ref[...] * (1.0 - lam_init))
    o_ref[0] = on.T.astype(BF16)


def _attn_call(bound, lamv, subln, q, k, vt, lc, lam_init, tq, tk, bounded):
    b, l, d = q.shape
    la = k.shape[1]
    return pl.pallas_call(
        functools.partial(_attn_kernel, lc=lc, tk=tk, lam_init=lam_init, bounded=bounded),
        out_shape=jax.ShapeDtypeStruct((b, l, d), BF16),
        grid=(b, N_HEADS, l // tq),
        in_specs=[pl.BlockSpec(bound.shape, lambda bi, h, i: (0, 0)),
                  pl.BlockSpec(lamv.shape, lambda bi, h, i: (0, 0)),
                  pl.BlockSpec(subln.shape, lambda bi, h, i: (0, 0)),
                  pl.BlockSpec((1, tq, V_DIM), lambda bi, h, i: (bi, i, h)),
                  pl.BlockSpec((1, la, V_DIM), lambda bi, h, i: (bi, 0, h)),
                  pl.BlockSpec((1, V_DIM, la), lambda bi, h, i: (bi, h, 0))],
        out_specs=pl.BlockSpec((1, tq, V_DIM), lambda bi, h, i: (bi, i, h)),
        compiler_params=_cparams(("parallel", "parallel", "arbitrary")),
        name="attn_bounded" if bounded else "attn_online",
    )(bound, lamv, subln, q, k, vt)


def _s5_matrices(lam_re, lam_im, log_dt, b_re, b_im, c_re, c_im, d_skip):
    t = CHUNK
    g_all, p_n = lam_re.shape[1:]
    c_n = b_re.shape[-1]
    sg = g_all // SUPER
    eye = jnp.eye(SUPER, dtype=F32)
    ks = jnp.arange(t + 1, dtype=F32)[:, None, None]
    mats = []
    for dr in range(2):
        lre, lim = lam_re[dr], lam_im[dr]
        dt = jnp.exp(log_dt[dr])[:, None]
        mag = jnp.exp(ks * (lre * dt))
        akr = mag * jnp.cos(ks * (lim * dt))
        aki = mag * jnp.sin(ks * (lim * dt))
        ar, ai = akr[1], aki[1]
        den = lre * lre + lim * lim
        nr = ar - 1.0
        fr = (nr * lre + ai * lim) / den
        fi = (ai * lre - nr * lim) / den
        bbr = fr[..., None] * b_re[dr] - fi[..., None] * b_im[dr]
        bbi = fr[..., None] * b_im[dr] + fi[..., None] * b_re[dr]
        abr = akr[..., None] * bbr - aki[..., None] * bbi
        abi = akr[..., None] * bbi + aki[..., None] * bbr
        kern = (jnp.einsum('gdp,kgpc->kgdc', c_re[dr], abr)
                - jnp.einsum('gdp,kgpc->kgdc', c_im[dr], abi))
        car = c_re[dr][None] * akr[:, :, None, :] - c_im[dr][None] * aki[:, :, None, :]
        cai = c_re[dr][None] * aki[:, :, None, :] + c_im[dr][None] * akr[:, :, None, :]
        mats.append((akr, aki, abr, abi, kern, car, cai))

    s_idx = jnp.arange(t)
    lag = s_idx[None, :] - s_idx[:, None]
    kf, kb = mats[0][4], mats[1][4]
    d_diag = jnp.eye(c_n, dtype=F32)[None] * d_skip.reshape(g_all, 1, c_n)
    kf_l = kf[jnp.clip(lag, 0, t)]
    kb_l = kb[jnp.clip(-lag, 0, t)]
    fwd_on = (lag >= 0)[..., None, None, None]
    bwd_on = (lag <= 0)[..., None, None, None]
    same = (lag == 0)[..., None, None, None]
    mg = jnp.where(fwd_on, kf_l, 0.0) + jnp.where(bwd_on, kb_l, 0.0) + jnp.where(same, d_diag[None, None], 0.0)
    mg = mg.reshape(t, t, sg, SUPER, c_n, c_n)
    m = jnp.einsum('stGldc,lm->Gslctmd', mg, eye)
    m = m.reshape(sg, t * SUPER * c_n, t * SUPER * c_n)

    def state_in(abr, abi, order):
        wr = abr[order].reshape(t, sg, SUPER, p_n, c_n)
        wi = abi[order].reshape(t, sg, SUPER, p_n, c_n)
        w = jnp.stack([wr, wi], axis=0)
        w = jnp.einsum('rsGlpc,lm->Gslcrmp', w, eye)
        return w.reshape(sg, t * SUPER * c_n, 2 * SUPER * p_n)

    def state_out(car, cai, order):
        vr = car[order].reshape(t, sg, SUPER, c_n, p_n)
        vi = -cai[order].reshape(t, sg, SUPER, c_n, p_n)
        v = jnp.stack([vr, vi], axis=0)
        v = jnp.einsum('rtGldp,lm->Grlptmd', v, eye)
        return v.reshape(sg, 2 * SUPER * p_n, t * SUPER * c_n)

    wf = state_in(mats[0][2], mats[0][3], (t - 1) - s_idx)
    wb = state_in(mats[1][2], mats[1][3], s_idx)
    vf = state_out(mats[0][5], mats[0][6], s_idx + 1)
    vb = state_out(mats[1][5], mats[1][6], t - s_idx)

    def decay(akr, aki):
        return jnp.stack([akr[t].reshape(sg, SUPER * p_n), aki[t].reshape(sg, SUPER * p_n)], axis=1)

    bf = lambda a: a.astype(BF16)
    return (bf(m), bf(wf), bf(wb), bf(vf), bf(vb),
            decay(mats[0][0], mats[0][1]), decay(mats[1][0], mats[1][1]))


def _s5_kernel(u_ref, m_ref, wf_ref, wb_ref, vf_ref, vb_ref, af_ref, ab_ref, y_ref, sf_ref, sb_ref, *,
               nchunk_c):
    n = sf_ref.shape[0]
    half = sf_ref.shape[1] // 2
    z = jnp.concatenate([u_ref[0, 0, pl.ds(t, n, stride=CHUNK), :] for t in range(CHUNK)], axis=-1).astype(BF16)
    sf_ref[...] = jnp.dot(z, wf_ref[0], preferred_element_type=F32)
    sb_ref[...] = jnp.dot(z, wb_ref[0], preferred_element_type=F32)
    afr, afi = af_ref[0, 0:1, :], af_ref[0, 1:2, :]
    abr, abi = ab_ref[0, 0:1, :], ab_ref[0, 1:2, :]

    def step(k, carry):
        hr, hi, gr, gi = carry
        row = sf_ref[pl.ds(k, 1), :]
        sf_ref[pl.ds(k, 1), :] = jnp.concatenate([hr, hi], axis=-1)
        hr, hi = afr * hr - afi * hi + row[:, :half], afr * hi + afi * hr + row[:, half:]
        kb = jnp.where(k < nchunk_c, nchunk_c - 1 - k, n + nchunk_c - 1 - k)
        row = sb_ref[pl.ds(kb, 1), :]
        sb_ref[pl.ds(kb, 1), :] = jnp.concatenate([gr, gi], axis=-1)
        gr, gi = abr * gr - abi * gi + row[:, :half], abr * gi + abi * gr + row[:, half:]
        return hr, hi, gr, gi

    zero = jnp.zeros((1, half), F32)
    lax.fori_loop(0, n, step, (zero, zero, zero, zero))

    y = (jnp.dot(z[nchunk_c:], m_ref[0], preferred_element_type=F32)
         + jnp.dot(sf_ref[nchunk_c:, :].astype(BF16), vf_ref[0], preferred_element_type=F32)
         + jnp.dot(sb_ref[nchunk_c:, :].astype(BF16), vb_ref[0], preferred_element_type=F32))
    for t in range(CHUNK):
        y_ref[0, 0, pl.ds(t, n - nchunk_c, stride=CHUNK), :] = y[:, t * LANES:(t + 1) * LANES]


def _s5_call(u, mats, lc):
    b, sg, la, _ = u.shape
    t = CHUNK
    n, nchunk_c = la // t, lc // t
    nx = n - nchunk_c
    m, wf, wb, vf, vb, a8f, a8b = mats
    wspec = lambda a: pl.BlockSpec((1,) + a.shape[1:], lambda g, bi: (g, 0, 0))
    y = pl.pallas_call(
        functools.partial(_s5_kernel, nchunk_c=nchunk_c),
        out_shape=jax.ShapeDtypeStruct((b, sg, nx * t, LANES), F32),
        grid=(sg, b),
        in_specs=[pl.BlockSpec((1, 1, la, LANES), lambda g, bi: (bi, g, 0, 0))]
                 + [wspec(a) for a in (m, wf, wb, vf, vb, a8f, a8b)],
        out_specs=pl.BlockSpec((1, 1, nx * t, LANES), lambda g, bi: (bi, g, 0, 0)),
        scratch_shapes=[pltpu.VMEM((n, t * LANES), F32), pltpu.VMEM((n, t * LANES), F32)],
        compiler_params=_cparams(("arbitrary", "arbitrary")),
        name="s5",
    )(u, m, wf, wb, vf, vb, a8f, a8b)
    return y


def _gelu_tanh(y):
    return 0.5 * y * (1.0 + jnp.tanh(math.sqrt(2.0 / math.pi) * (y + 0.044715 * (y * y * y))))


def _merge_kernel(x_ref, mod_ref, y_ref, ox_ref, ga_ref, gs_ref, wglu_ref, bglu_ref, wpa_ref, wps_ref,
                  wout_ref, o_ref, *, base):
    ge = _gelu_tanh(jnp.concatenate([y_ref[0, g] for g in range(y_ref.shape[1])], axis=-1))
    z = jnp.dot(ge.astype(BF16), wglu_ref[...], preferred_element_type=F32) + bglu_ref[...]
    sx = (ge * jax.nn.sigmoid(z)).astype(BF16)
    mix = (ga_ref[0].astype(F32) * jnp.dot(ox_ref[0], wpa_ref[...], preferred_element_type=F32)
           + gs_ref[0].astype(F32) * jnp.dot(sx, wps_ref[...], preferred_element_type=F32))
    out = jnp.dot(mix.astype(BF16), wout_ref[...], preferred_element_type=F32)
    o_ref[0] = x_ref[0] + mod_ref[0][base + 2:base + 3, :] * out


def _merge_call(x, mod, y, ox, ga, gs, w_glu, b_glu, w_pa, w_ps, w_out, base, tm):
    b, l, d = x.shape
    sg = y.shape[1]
    tok = lambda width: pl.BlockSpec((1, tm, width), lambda bi, i: (bi, i, 0))
    return pl.pallas_call(
        functools.partial(_merge_kernel, base=base),
        out_shape=jax.ShapeDtypeStruct(x.shape, F32),
        grid=(b, l // tm),
        in_specs=[tok(d), pl.BlockSpec((1, N_MOD, d), lambda bi, i: (bi, 0, 0)),
                  pl.BlockSpec((1, sg, tm, LANES), lambda bi, i: (bi, 0, i, 0)), tok(d), tok(d), tok(d),
                  _resident(w_glu.shape), _resident(b_glu.shape), _resident(w_pa.shape),
                  _resident(w_ps.shape), _resident(w_out.shape)],
        out_specs=tok(d),
        compiler_params=_cparams(("parallel", "parallel")),
        name="merge",
    )(x, mod, y, ox, ga, gs, w_glu, b_glu, w_pa, w_ps, w_out)


def _rope_tables(l):
    n_freq = HEAD_DIM // 4
    inv = ROPE_BASE ** (-jnp.arange(n_freq, dtype=F32) / n_freq)
    pos = jnp.arange(l, dtype=jnp.int32)
    row = (pos // GRID_W).astype(F32)
    col = (pos % GRID_W).astype(F32)
    ang = jnp.concatenate([row[:, None] * inv, col[:, None] * inv], axis=-1)
    cos, sin = jnp.cos(ang), jnp.sin(ang)
    return jnp.concatenate([cos] * 4, axis=-1), jnp.concatenate([-sin, sin, -sin, sin], axis=-1)


def _group_sum_matrices(d):
    grp = jnp.arange(d) // HEAD_DIM
    e1 = (grp[:, None] == jnp.arange(LANES)[None, :]).astype(BF16)
    e2 = jnp.concatenate([e1.T, e1.T], axis=0)
    return e1, e2


def kernel(x, c, ctx, c_ctx, w_mod, b_mod, norm_ffn1, w13_ffn1, w2_ffn1, norm_mix, w_in, q_norm, k_norm, lam_q1, lam_k1, lam_q2, lam_k2, subln, s5_lam_re, s5_lam_im, s5_log_dt, s5_b_re, s5_b_im, s5_c_re, s5_c_im, s5_d, w_glu, b_glu, w_pa, w_ps, w_out, norm_ffn2, w13_ffn2, w2_ffn2):
    assert w_mod.shape[0] == 1, "single-layer block"
    b, l, d = x.shape
    lc = ctx.shape[1]
    lam_init = 0.8 - 0.6 * math.exp(-0.3 * 0)
    bf = lambda a: a.astype(BF16)
    row = lambda a: a.reshape(1, -1)

    rows = b + 1
    pad = (-rows) % 8
    cc = jnp.concatenate([c, c_ctx[None, :], jnp.zeros((pad, d), F32)], axis=0)
    mod = _mod_call(cc, w_mod[0], row(b_mod[0]))
    modx = mod[:b].reshape(b, N_MOD, d)
    modc = mod[b:b + 1].reshape(1, N_MOD, d)

    w13_1, w2_1 = bf(w13_ffn1[0]), bf(w2_ffn1[0])
    x1 = _ffn_call(x, modx, row(norm_ffn1[0]), w13_1, w2_1, 0, 512, "ffn1_x")
    ctx1 = _ffn_call(ctx, modc, row(norm_ffn1[0]), w13_1, w2_1, 0, lc, "ffn1_ctx")

    e1, e2 = _group_sum_matrices(d)
    cos, sin = _rope_tables(l)
    qg = row(jnp.tile(q_norm[0], d // HEAD_DIM)) * (LOG2E / math.sqrt(HEAD_DIM))
    kg = row(jnp.tile(k_norm[0], d // HEAD_DIM))
    q, k, vt, u, ga, gs = _inproj_call(ctx1, x1, modc, modx, row(norm_mix[0]), bf(w_in[0]), qg, kg, e1, e2,
                                       cos, sin, s5_d.shape[-1], 3, 256)

    lamv = jnp.stack([lam_q1[0], lam_k1[0], lam_q2[0], lam_k2[0]], axis=0)
    bound = (LOG2E * math.sqrt(HEAD_DIM)) * jnp.max(jnp.abs(q_norm[0])) * jnp.max(jnp.abs(k_norm[0]))
    attn = lambda bounded, tq: functools.partial(
        _attn_call, lamv=lamv, subln=subln[0].reshape(-1, 1), q=q, k=k, vt=vt, lc=lc, lam_init=lam_init,
        tq=tq, tk=1024, bounded=bounded)
    ox = lax.cond(2.0 * bound <= MAX_EXP2_SPAN, attn(True, 512), attn(False, 256), bound.reshape(1, 1))

    mats = _s5_matrices(s5_lam_re[0], s5_lam_im[0], s5_log_dt[0], s5_b_re[0], s5_b_im[0],
                        s5_c_re[0], s5_c_im[0], s5_d[0])
    y = _s5_call(u, mats, lc)

    x2 = _merge_call(x1, modx, y, ox, ga, gs, bf(w_glu[0]), row(b_glu[0]), bf(w_pa[0]), bf(w_ps[0]),
                     bf(w_out[0]), 3, 256)
    return _ffn_call(x2, modx, row(norm_ffn2[0]), bf(w13_ffn2[0]), bf(w2_ffn2[0]), 6, 512, "ffn2_x")
```

```python
import functools
import math

import jax
import jax.numpy as jnp
from jax import lax
from jax.experimental import pallas as pl
from jax.experimental.pallas import tpu as pltpu

F32 = jnp.float32
BF16 = jnp.bfloat16

EPS = 1e-6
ROPE_BASE = 10000.0
GRID_W = 64
N_HEADS = 8
HEAD_DIM = 64
V_DIM = 2 * HEAD_DIM
S5_GROUP = 16
S5_STATE = 64
N_MOD = 9
LANES = 128
CHUNK = 8
SUPER = LANES // S5_GROUP
VMEM_LIMIT = 56 * 1024 * 1024
LOG2E = math.log2(math.e)
MAX_EXP2_SPAN = 100.0


def _cparams(sem):
    return pltpu.CompilerParams(dimension_semantics=sem, vmem_limit_bytes=VMEM_LIMIT)


def _resident(shape):
    nd = len(shape)
    return pl.BlockSpec(shape, lambda *_: (0,) * nd, pipeline_mode=pl.Buffered(1))


def _silu(a):
    return a * jax.nn.sigmoid(a)


def _modulated(x, g, mod, base):
    ms = jnp.mean(x * x, axis=-1, keepdims=True)
    xn = x * lax.rsqrt(ms + EPS) * g
    return xn * (1.0 + mod[base + 1:base + 2, :]) + mod[base:base + 1, :]


def _mod_kernel(c_ref, w_ref, b_ref, o_ref):
    a = _silu(c_ref[...]).astype(BF16)
    o_ref[...] = jnp.dot(a, w_ref[...].astype(BF16), preferred_element_type=F32) + b_ref[...]


def _mod_call(cc, w_mod, b_mod):
    rows, d = cc.shape
    n = w_mod.shape[1]
    tn = d
    return pl.pallas_call(
        _mod_kernel,
        out_shape=jax.ShapeDtypeStruct((rows, n), F32),
        grid=(n // tn,),
        in_specs=[pl.BlockSpec((rows, d), lambda j: (0, 0)),
                  pl.BlockSpec((d, tn), lambda j: (0, j)),
                  pl.BlockSpec((1, tn), lambda j: (0, j))],
        out_specs=pl.BlockSpec((rows, tn), lambda j: (0, j)),
        compiler_params=_cparams(("arbitrary",)),
        name="mod",
    )(cc, w_mod, b_mod)


def _ffn_kernel(x_ref, mod_ref, g_ref, w13_ref, w2_ref, o_ref, *, base):
    x = x_ref[0]
    mod = mod_ref[0]
    xm = _modulated(x, g_ref[...], mod, base).astype(BF16)
    h = jnp.dot(xm, w13_ref[...], preferred_element_type=F32)
    dff = w2_ref.shape[0]
    act = (_silu(h[:, :dff]) * h[:, dff:]).astype(BF16)
    y = jnp.dot(act, w2_ref[...], preferred_element_type=F32)
    o_ref[0] = x + (0.5 * mod[base + 2:base + 3, :]) * y


def _ffn_call(x, mod, g, w13, w2, base, tm, name):
    b, l, d = x.shape
    mod_map = (lambda bi, i: (bi, 0, 0)) if mod.shape[0] == b else (lambda bi, i: (0, 0, 0))
    return pl.pallas_call(
        functools.partial(_ffn_kernel, base=base),
        out_shape=jax.ShapeDtypeStruct(x.shape, F32),
        grid=(b, l // tm),
        in_specs=[pl.BlockSpec((1, tm, d), lambda bi, i: (bi, i, 0)),
                  pl.BlockSpec((1, N_MOD, d), mod_map),
                  _resident(g.shape), _resident(w13.shape), _resident(w2.shape)],
        out_specs=pl.BlockSpec((1, tm, d), lambda bi, i: (bi, i, 0)),
        compiler_params=_cparams(("parallel", "parallel")),
        name=name,
    )(x, mod, g, w13, w2)


def _qk_norm(t, gain, e1_ref, e2_ref):
    ss = jnp.dot((t * t).astype(BF16), e1_ref[...], preferred_element_type=F32)
    r = lax.rsqrt(ss * (1.0 / HEAD_DIM) + EPS)
    r_hi = r.astype(BF16)
    r_lo = (r - r_hi.astype(F32)).astype(BF16)
    rb = jnp.dot(jnp.concatenate([r_hi, r_lo], axis=-1), e2_ref[...], preferred_element_type=F32)
    return t * rb * gain


def _rope(t, cos, sin_signed):
    n = t.shape[-1]
    half = HEAD_DIM // 2
    lane = lax.broadcasted_iota(jnp.int32, t.shape, 1)
    swapped = jnp.where(lane % HEAD_DIM < half, pltpu.roll(t, n - half, 1), pltpu.roll(t, half, 1))
    reps = n // cos.shape[-1]
    cosf = jnp.concatenate([cos] * reps, axis=-1)
    sinf = jnp.concatenate([sin_signed] * reps, axis=-1)
    return t * cosf + swapped * sinf


def _store_supergroups(u_ref, u):
    for g in range(u_ref.shape[1]):
        u_ref[0, g] = u[:, g * LANES:(g + 1) * LANES]


def _inproj_kernel(x_ref, mod_ref, g_ref, w_ref, qg_ref, kg_ref, e1_ref, e2_ref, cos_ref, sin_ref,
                   q_ref, k_ref, vt_ref, u_ref, ga_ref, gs_ref, *, base):
    d = x_ref.shape[-1]
    qc, kc, vc = d, 2 * d, 3 * d
    uc = vc + u_ref.shape[1] * LANES
    xm = _modulated(x_ref[0], g_ref[...], mod_ref[0], base).astype(BF16)
    proj = lambda lo, hi: jnp.dot(xm, w_ref[:, lo:hi], preferred_element_type=F32)
    cos = cos_ref[...]
    sin = sin_ref[...]
    q_ref[0] = _rope(_qk_norm(proj(0, qc), qg_ref[...], e1_ref, e2_ref), cos, sin).astype(BF16)
    k_ref[0] = _rope(_qk_norm(proj(qc, kc), kg_ref[...], e1_ref, e2_ref), cos, sin).astype(BF16)
    vt_ref[0] = proj(kc, vc).astype(BF16).T
    _store_supergroups(u_ref, proj(vc, uc))
    ga_ref[0] = jax.nn.sigmoid(proj(uc, uc + d)).astype(BF16)
    gs_ref[0] = jax.nn.sigmoid(proj(uc + d, uc + 2 * d)).astype(BF16)


def _inproj_call(x, lc, modx, g, w_in, qg, kg, e1, e2, cos, sin, s5w, base, tm):
    b, l, d = x.shape
    la = l + lc
    tok = lambda bi, i: (bi, i, 0)
    out_shape = (jax.ShapeDtypeStruct((b, l, d), BF16),
                 jax.ShapeDtypeStruct((b, la, d), BF16),
                 jax.ShapeDtypeStruct((b, d, la), BF16),
                 jax.ShapeDtypeStruct((b, s5w // LANES, la, LANES), F32),
                 jax.ShapeDtypeStruct((b, l, d), BF16),
                 jax.ShapeDtypeStruct((b, l, d), BF16))
    return pl.pallas_call(
        functools.partial(_inproj_kernel, base=base),
        out_shape=out_shape,
        grid=(b, l // tm),
        in_specs=[pl.BlockSpec((1, tm, d), tok),
                  pl.BlockSpec((1, N_MOD, d), lambda bi, i: (bi, 0, 0)),
                  _resident(g.shape), _resident(w_in.shape), _resident(qg.shape), _resident(kg.shape),
                  _resident(e1.shape), _resident(e2.shape),
                  pl.BlockSpec((tm, LANES), lambda bi, i: (i, 0)),
                  pl.BlockSpec((tm, LANES), lambda bi, i: (i, 0))],
        out_specs=(pl.BlockSpec((1, tm, d), tok),
                   pl.BlockSpec((1, tm, d), tok),
                   pl.BlockSpec((1, d, tm), lambda bi, i: (bi, 0, i)),
                   pl.BlockSpec((1, s5w // LANES, tm, LANES), lambda bi, i: (bi, 0, i, 0)),
                   pl.BlockSpec((1, tm, d), tok),
                   pl.BlockSpec((1, tm, d), tok)),
        compiler_params=_cparams(("parallel", "parallel")),
        name="inproj",
    )(x, modx, g, w_in, qg, kg, e1, e2, cos, sin)


def _ctx_kernel(ctx_ref, mod_ref, g1_ref, w13_ref, w2_ref, gm_ref, w_ref, kg_ref, e1_ref, e2_ref,
                k_in, vt_in, u_in, k_ref, vt_ref, u_ref, *, base):
    del k_in, vt_in, u_in
    d = ctx_ref.shape[-1]
    mod = mod_ref[0]
    x = ctx_ref[0]
    xm = _modulated(x, g1_ref[...], mod, 0).astype(BF16)
    h = jnp.dot(xm, w13_ref[...], preferred_element_type=F32)
    dff = w2_ref.shape[0]
    act = (_silu(h[:, :dff]) * h[:, dff:]).astype(BF16)
    x1 = x + (0.5 * mod[2:3, :]) * jnp.dot(act, w2_ref[...], preferred_element_type=F32)
    xm = _modulated(x1, gm_ref[...], mod, base).astype(BF16)
    p = jnp.dot(xm, w_ref[...], preferred_element_type=F32)
    k_ref[0] = _qk_norm(p[:, :d], kg_ref[...], e1_ref, e2_ref).astype(BF16)
    vt_ref[0] = p[:, d:2 * d].astype(BF16).T
    _store_supergroups(u_ref, p[:, 2 * d:])


def _ctx_call(ctx, modc, g1, w13, w2, gm, w_kvu, kg, e1, e2, k, vt, u, base):
    b, lc, d = ctx.shape
    sg = u.shape[1]
    last = k.shape[1] // lc - 1
    anyspec = pl.BlockSpec(memory_space=pl.ANY)
    return pl.pallas_call(
        functools.partial(_ctx_kernel, base=base),
        out_shape=(jax.ShapeDtypeStruct(k.shape, k.dtype), jax.ShapeDtypeStruct(vt.shape, vt.dtype),
                   jax.ShapeDtypeStruct(u.shape, u.dtype)),
        grid=(b,),
        in_specs=[pl.BlockSpec((1, lc, d), lambda bi: (bi, 0, 0)),
                  pl.BlockSpec((1, N_MOD, d), lambda bi: (0, 0, 0)),
                  _resident(g1.shape), _resident(w13.shape), _resident(w2.shape), _resident(gm.shape),
                  _resident(w_kvu.shape), _resident(kg.shape), _resident(e1.shape), _resident(e2.shape),
                  anyspec, anyspec, anyspec],
        out_specs=(pl.BlockSpec((1, lc, d), lambda bi: (bi, last, 0)),
                   pl.BlockSpec((1, d, lc), lambda bi: (bi, 0, last)),
                   pl.BlockSpec((1, sg, lc, LANES), lambda bi: (bi, 0, last, 0))),
        input_output_aliases={10: 0, 11: 1, 12: 2},
        compiler_params=_cparams(("parallel",)),
        name="ctx",
    )(ctx, modc, g1, w13, w2, gm, w_kvu, kg, e1, e2, k, vt, u)


def _attn_kernel(bound_ref, lam_ref, subln_ref, q_ref, k_ref, vt_ref, o_ref, *, lc, tk, lam_init, bounded):
    q = q_ref[0]
    la = k_ref.shape[1]
    lane = lax.broadcasted_iota(jnp.int32, q.shape, 1)
    zero = jnp.zeros_like(q)
    q_comp = (jnp.where(lane < HEAD_DIM, q, zero), jnp.where(lane >= HEAD_DIM, q, zero))
    lv = lam_ref[...]
    lam = (jnp.exp(jnp.sum(lv[0:1] * lv[1:2], axis=-1, keepdims=True))
           - jnp.exp(jnp.sum(lv[2:3] * lv[3:4], axis=-1, keepdims=True)) + lam_init)
    blocks = [(0, la)] if bounded else [(st, tk) for st in range(0, la - lc, tk)] + [(la - lc, lc)]
    nt_dims = (((1,), (1,)), ((), ()))
    heads = []
    for qz in q_comp:
        m = l = acc = None
        for st, sz in blocks:
            kb = k_ref[0, st:st + sz, :]
            vb = vt_ref[0, :, st:st + sz]
            s = lax.dot_general(kb, qz, nt_dims, preferred_element_type=F32)
            if bounded:
                p = jnp.exp2(s - bound_ref[...])
                l = jnp.sum(p, axis=0, keepdims=True)
                acc = jnp.dot(vb, p.astype(BF16), preferred_element_type=F32)
                continue
            bm = jnp.max(s, axis=0, keepdims=True)
            if m is None:
                m_new = bm
                p = jnp.exp2(s - m_new)
                l = jnp.sum(p, axis=0, keepdims=True)
                acc = jnp.dot(vb, p.astype(BF16), preferred_element_type=F32)
            else:
                m_new = jnp.maximum(m, bm)
                alpha = jnp.exp2(m - m_new)
                p = jnp.exp2(s - m_new)
                l = alpha * l + jnp.sum(p, axis=0, keepdims=True)
                acc = alpha * acc + jnp.dot(vb, p.astype(BF16), preferred_element_type=F32)
            m = m_new
        heads.append(acc / l)
    o = heads[0] - lam * heads[1]
    ms = jnp.mean(o * o, axis=0, keepdims=True)
    on = o * lax.rsqrt(ms + EPS) * (subln_ref[...] * (1.0 - lam_init))
    o_ref[0] = on.T.astype(BF16)


def _attn_call(bound, lamv, subln, q, k, vt, lc, lam_init, tq, tk, bounded):
    b, l, d = q.shape
    la = k.shape[1]
    return pl.pallas_call(
        functools.partial(_attn_kernel, lc=lc, tk=tk, lam_init=lam_init, bounded=bounded),
        out_shape=jax.ShapeDtypeStruct((b, l, d), BF16),
        grid=(b, N_HEADS, l // tq),
        in_specs=[pl.BlockSpec(bound.shape, lambda bi, h, i: (0, 0)),
                  pl.BlockSpec(lamv.shape, lambda bi, h, i: (0, 0)),
                  pl.BlockSpec(subln.shape, lambda bi, h, i: (0, 0)),
                  pl.BlockSpec((1, tq, V_DIM), lambda bi, h, i: (bi, i, h)),
                  pl.BlockSpec((1, la, V_DIM), lambda bi, h, i: (bi, 0, h)),
                  pl.BlockSpec((1, V_DIM, la), lambda bi, h, i: (bi, h, 0))],
        out_specs=pl.BlockSpec((1, tq, V_DIM), lambda bi, h, i: (bi, i, h)),
        compiler_params=_cparams(("parallel", "parallel", "arbitrary")),
        name="attn_bounded" if bounded else "attn_online",
    )(bound, lamv, subln, q, k, vt)


def _s5_matrices(lam_re, lam_im, log_dt, b_re, b_im, c_re, c_im, d_skip, seg):
    t = CHUNK
    g_all, p_n = lam_re.shape[1:]
    c_n = b_re.shape[-1]
    sg = g_all // SUPER
    eye = jnp.eye(SUPER, dtype=F32)
    ks = jnp.arange(t + 1, dtype=F32)[:, None, None]
    mats = []
    for dr in range(2):
        lre, lim = lam_re[dr], lam_im[dr]
        dt = jnp.exp(log_dt[dr])[:, None]
        mag = jnp.exp(ks * (lre * dt))
        akr = mag * jnp.cos(ks * (lim * dt))
        aki = mag * jnp.sin(ks * (lim * dt))
        ar, ai = akr[1], aki[1]
        den = lre * lre + lim * lim
        nr = ar - 1.0
        fr = (nr * lre + ai * lim) / den
        fi = (ai * lre - nr * lim) / den
        bbr = fr[..., None] * b_re[dr] - fi[..., None] * b_im[dr]
        bbi = fr[..., None] * b_im[dr] + fi[..., None] * b_re[dr]
        abr = akr[..., None] * bbr - aki[..., None] * bbi
        abi = akr[..., None] * bbi + aki[..., None] * bbr
        kern = (jnp.einsum('gdp,kgpc->kgdc', c_re[dr], abr)
                - jnp.einsum('gdp,kgpc->kgdc', c_im[dr], abi))
        car = c_re[dr][None] * akr[:, :, None, :] - c_im[dr][None] * aki[:, :, None, :]
        cai = c_re[dr][None] * aki[:, :, None, :] + c_im[dr][None] * akr[:, :, None, :]
        mats.append((akr, aki, abr, abi, kern, car, cai))

    s_idx = jnp.arange(t)
    lag = s_idx[None, :] - s_idx[:, None]
    kf, kb = mats[0][4], mats[1][4]
    d_diag = jnp.eye(c_n, dtype=F32)[None] * d_skip.reshape(g_all, 1, c_n)
    kf_l = kf[jnp.clip(lag, 0, t)]
    kb_l = kb[jnp.clip(-lag, 0, t)]
    fwd_on = (lag >= 0)[..., None, None, None]
    bwd_on = (lag <= 0)[..., None, None, None]
    same = (lag == 0)[..., None, None, None]
    mg = jnp.where(fwd_on, kf_l, 0.0) + jnp.where(bwd_on, kb_l, 0.0) + jnp.where(same, d_diag[None, None], 0.0)
    mg = mg.reshape(t, t, sg, SUPER, c_n, c_n)
    m = jnp.einsum('stGldc,lm->Gslctmd', mg, eye)
    m = m.reshape(sg, t * SUPER * c_n, t * SUPER * c_n)

    def state_in(abr, abi, order):
        wr = abr[order].reshape(t, sg, SUPER, p_n, c_n)
        wi = abi[order].reshape(t, sg, SUPER, p_n, c_n)
        w = jnp.stack([wr, wi], axis=0)
        w = jnp.einsum('rsGlpc,lm->Gslcrmp', w, eye)
        return w.reshape(sg, t * SUPER * c_n, 2 * SUPER * p_n)

    def state_out(car, cai, order):
        vr = car[order].reshape(t, sg, SUPER, c_n, p_n)
        vi = -cai[order].reshape(t, sg, SUPER, c_n, p_n)
        v = jnp.stack([vr, vi], axis=0)
        v = jnp.einsum('rtGldp,lm->Grlptmd', v, eye)
        return v.reshape(sg, 2 * SUPER * p_n, t * SUPER * c_n)

    wf = state_in(mats[0][2], mats[0][3], (t - 1) - s_idx)
    wb = state_in(mats[1][2], mats[1][3], s_idx)
    vf = state_out(mats[0][5], mats[0][6], s_idx + 1)
    vb = state_out(mats[1][5], mats[1][6], t - s_idx)

    def decay_table(dr, powers):
        lre, lim = lam_re[dr], lam_im[dr]
        dt = jnp.exp(log_dt[dr])[:, None]
        kk = (t * powers).astype(F32)[:, None, None]
        mag = jnp.exp(kk * (lre * dt))
        tab = jnp.stack([mag * jnp.cos(kk * (lim * dt)), mag * jnp.sin(kk * (lim * dt))], axis=0)
        return tab.reshape(2, -1, sg, SUPER * p_n).transpose(2, 0, 1, 3)

    j = jnp.arange(seg)
    bf = lambda a: a.astype(BF16)
    return (bf(m), bf(wf), bf(wb), bf(vf), bf(vb), decay_table(0, j + 1), decay_table(1, seg - j))


def _s5_segment(n):
    return 8 * (-(-n // 64))


def _cmul_add(ar, ai, xr, xi, br, bi):
    return ar * xr - ai * xi + br, ar * xi + ai * xr + bi


def _s5_kernel(u_ref, m_ref, wf_ref, wb_ref, vf_ref, vb_ref, pf_ref, pb_ref, y_ref, sf_ref, sb_ref, *,
               nx, nc):
    n = nx + nc
    tiles, npad = sf_ref.shape[1:3]
    seg = npad // 8
    pad = npad - n
    half = tiles * LANES

    def load(ref, rows):
        return tuple(jnp.concatenate([ref[ri, c, rows, :] for c in range(tiles)], axis=-1) for ri in range(2))

    def store(ref, rows, re, im):
        for c in range(tiles):
            ref[0, c, rows, :] = re[:, c * LANES:(c + 1) * LANES]
            ref[1, c, rows, :] = im[:, c * LANES:(c + 1) * LANES]

    z = jnp.concatenate([u_ref[0, 0, pl.ds(t, n, stride=CHUNK), :] for t in range(CHUNK)], axis=-1).astype(BF16)
    zl, zc = z[:nx], z[nx:]
    zpad = [jnp.zeros((pad, z.shape[1]), BF16)] if pad else []
    s = jnp.dot(jnp.concatenate([zc, zl] + zpad, axis=0), wf_ref[0], preferred_element_type=F32)
    store(sf_ref, slice(None), s[:, :half], s[:, half:])
    s = jnp.dot(jnp.concatenate(zpad + [zl, zc], axis=0), wb_ref[0], preferred_element_type=F32)
    store(sb_ref, slice(None), s[:, :half], s[:, half:])

    afr, afi = pf_ref[0, 0, 0:1, :], pf_ref[0, 1, 0:1, :]
    abr, abi = pb_ref[0, 0, seg - 1:seg, :], pb_ref[0, 1, seg - 1:seg, :]

    def step(i, carry):
        hr, hi, gr, gi = carry
        rows = pl.ds(i, 8, stride=seg)
        hr, hi = _cmul_add(afr, afi, hr, hi, *load(sf_ref, rows))
        store(sf_ref, rows, hr, hi)
        rows = pl.ds(seg - 1 - i, 8, stride=seg)
        gr, gi = _cmul_add(abr, abi, gr, gi, *load(sb_ref, rows))
        store(sb_ref, rows, gr, gi)
        return hr, hi, gr, gi

    zero = jnp.zeros((8, half), F32)
    hr, hi, gr, gi = lax.fori_loop(0, seg, step, (zero, zero, zero, zero))

    tfr, tfi = pf_ref[0, 0], pf_ref[0, 1]
    tbr, tbi = pb_ref[0, 0], pb_ref[0, 1]
    cr = ci = None
    for s in range(1, 8):
        er, ei = hr[s - 1:s], hi[s - 1:s]
        if cr is None:
            cr, ci = er, ei
        else:
            cr, ci = _cmul_add(tfr[seg - 1:seg], tfi[seg - 1:seg], cr, ci, er, ei)
        rows = slice(s * seg, (s + 1) * seg)
        store(sf_ref, rows, *_cmul_add(tfr, tfi, cr, ci, *load(sf_ref, rows)))
    cr = ci = None
    for s in range(6, -1, -1):
        er, ei = gr[s + 1:s + 2], gi[s + 1:s + 2]
        if cr is None:
            cr, ci = er, ei
        else:
            cr, ci = _cmul_add(tbr[0:1], tbi[0:1], cr, ci, er, ei)
        rows = slice(s * seg, (s + 1) * seg)
        store(sb_ref, rows, *_cmul_add(tbr, tbi, cr, ci, *load(sb_ref, rows)))

    def entering(ref, shift, first):
        cols = [pltpu.roll(ref[ri, c], shift, 0)[first:first + nx] for ri in range(2) for c in range(tiles)]
        return jnp.concatenate(cols, axis=-1).astype(BF16)

    y = (jnp.dot(zl, m_ref[0], preferred_element_type=F32)
         + jnp.dot(entering(sf_ref, 1, nc), vf_ref[0], preferred_element_type=F32)
         + jnp.dot(entering(sb_ref, npad - 1, pad), vb_ref[0], preferred_element_type=F32))
    for t in range(CHUNK):
        y_ref[0, 0, pl.ds(t, nx, stride=CHUNK), :] = y[:, t * LANES:(t + 1) * LANES]


def _s5_call(u, mats, lc):
    b, sg, la, _ = u.shape
    t = CHUNK
    n, nc = la // t, lc // t
    nx = n - nc
    npad = 8 * _s5_segment(n)
    tiles = mats[1].shape[-1] // (2 * LANES)
    wspec = lambda a: pl.BlockSpec((1,) + a.shape[1:], lambda g, bi: (g,) + (0,) * (a.ndim - 1))
    return pl.pallas_call(
        functools.partial(_s5_kernel, nx=nx, nc=nc),
        out_shape=jax.ShapeDtypeStruct((b, sg, nx * t, LANES), F32),
        grid=(sg, b),
        in_specs=[pl.BlockSpec((1, 1, la, LANES), lambda g, bi: (bi, g, 0, 0))] + [wspec(a) for a in mats],
        out_specs=pl.BlockSpec((1, 1, nx * t, LANES), lambda g, bi: (bi, g, 0, 0)),
        scratch_shapes=[pltpu.VMEM((2, tiles, npad, LANES), F32), pltpu.VMEM((2, tiles, npad, LANES), F32)],
        compiler_params=_cparams(("arbitrary", "arbitrary")),
        name="s5",
    )(u, *mats)


def _gelu_tanh(y):
    return 0.5 * y * (1.0 + jnp.tanh(math.sqrt(2.0 / math.pi) * (y + 0.044715 * (y * y * y))))


def _merge_kernel(x_ref, mod_ref, y_ref, ox_ref, ga_ref, gs_ref, wglu_ref, bglu_ref, wpa_ref, wps_ref,
                  wout_ref, o_ref, *, base):
    ge = _gelu_tanh(jnp.concatenate([y_ref[0, g] for g in range(y_ref.shape[1])], axis=-1))
    z = jnp.dot(ge.astype(BF16), wglu_ref[...], preferred_element_type=F32) + bglu_ref[...]
    sx = (ge * jax.nn.sigmoid(z)).astype(BF16)
    mix = (ga_ref[0].astype(F32) * jnp.dot(ox_ref[0], wpa_ref[...], preferred_element_type=F32)
           + gs_ref[0].astype(F32) * jnp.dot(sx, wps_ref[...], preferred_element_type=F32))
    out = jnp.dot(mix.astype(BF16), wout_ref[...], preferred_element_type=F32)
    o_ref[0] = x_ref[0] + mod_ref[0][base + 2:base + 3, :] * out


def _merge_call(x, mod, y, ox, ga, gs, w_glu, b_glu, w_pa, w_ps, w_out, base, tm):
    b, l, d = x.shape
    sg = y.shape[1]
    tok = lambda width: pl.BlockSpec((1, tm, width), lambda bi, i: (bi, i, 0))
    return pl.pallas_call(
        functools.partial(_merge_kernel, base=base),
        out_shape=jax.ShapeDtypeStruct(x.shape, F32),
        grid=(b, l // tm),
        in_specs=[tok(d), pl.BlockSpec((1, N_MOD, d), lambda bi, i: (bi, 0, 0)),
                  pl.BlockSpec((1, sg, tm, LANES), lambda bi, i: (bi, 0, i, 0)), tok(d), tok(d), tok(d),
                  _resident(w_glu.shape), _resident(b_glu.shape), _resident(w_pa.shape),
                  _resident(w_ps.shape), _resident(w_out.shape)],
        out_specs=tok(d),
        compiler_params=_cparams(("parallel", "parallel")),
        name="merge",
    )(x, mod, y, ox, ga, gs, w_glu, b_glu, w_pa, w_ps, w_out)


def _rope_tables(l):
    n_freq = HEAD_DIM // 4
    inv = ROPE_BASE ** (-jnp.arange(n_freq, dtype=F32) / n_freq)
    pos = jnp.arange(l, dtype=jnp.int32)
    row = (pos // GRID_W).astype(F32)
    col = (pos % GRID_W).astype(F32)
    ang = jnp.concatenate([row[:, None] * inv, col[:, None] * inv], axis=-1)
    cos, sin = jnp.cos(ang), jnp.sin(ang)
    return jnp.concatenate([cos] * 4, axis=-1), jnp.concatenate([-sin, sin, -sin, sin], axis=-1)


def _group_sum_matrices(d):
    grp = jnp.arange(d) // HEAD_DIM
    e1 = (grp[:, None] == jnp.arange(LANES)[None, :]).astype(BF16)
    e2 = jnp.concatenate([e1.T, e1.T], axis=0)
    return e1, e2


def kernel(x, c, ctx, c_ctx, w_mod, b_mod, norm_ffn1, w13_ffn1, w2_ffn1, norm_mix, w_in, q_norm, k_norm, lam_q1, lam_k1, lam_q2, lam_k2, subln, s5_lam_re, s5_lam_im, s5_log_dt, s5_b_re, s5_b_im, s5_c_re, s5_c_im, s5_d, w_glu, b_glu, w_pa, w_ps, w_out, norm_ffn2, w13_ffn2, w2_ffn2):
    assert w_mod.shape[0] == 1, "single-layer block"
    b, l, d = x.shape
    lc = ctx.shape[1]
    lam_init = 0.8 - 0.6 * math.exp(-0.3 * 0)
    bf = lambda a: a.astype(BF16)
    row = lambda a: a.reshape(1, -1)

    rows = b + 1
    pad = (-rows) % 8
    cc = jnp.concatenate([c, c_ctx[None, :], jnp.zeros((pad, d), F32)], axis=0)
    mod = _mod_call(cc, w_mod[0], row(b_mod[0]))
    modx = mod[:b].reshape(b, N_MOD, d)
    modc = mod[b:b + 1].reshape(1, N_MOD, d)

    w13_1, w2_1 = bf(w13_ffn1[0]), bf(w2_ffn1[0])
    x1 = _ffn_call(x, modx, row(norm_ffn1[0]), w13_1, w2_1, 0, 512, "ffn1_x")

    e1, e2 = _group_sum_matrices(d)
    cos, sin = _rope_tables(l)
    qg = row(jnp.tile(q_norm[0], d // HEAD_DIM)) * (LOG2E / math.sqrt(HEAD_DIM))
    kg = row(jnp.tile(k_norm[0], d // HEAD_DIM))
    w_inb = bf(w_in[0])
    s5w = s5_d.shape[-1]
    q, k, vt, u, ga, gs = _inproj_call(x1, lc, modx, row(norm_mix[0]), w_inb, qg, kg, e1, e2,
                                       cos, sin, s5w, 3, 512)
    k, vt, u = _ctx_call(ctx, modc, row(norm_ffn1[0]), w13_1, w2_1, row(norm_mix[0]),
                         w_inb[:, d:3 * d + s5w], kg, e1, e2, k, vt, u, 3)

    lamv = jnp.stack([lam_q1[0], lam_k1[0], lam_q2[0], lam_k2[0]], axis=0)
    bound = (LOG2E * math.sqrt(HEAD_DIM)) * jnp.max(jnp.abs(q_norm[0])) * jnp.max(jnp.abs(k_norm[0]))
    attn = lambda bounded, tq, tk: functools.partial(
        _attn_call, lamv=lamv, subln=subln[0].reshape(-1, 1), q=q, k=k, vt=vt, lc=lc, lam_init=lam_init,
        tq=tq, tk=tk, bounded=bounded)
    ox = lax.cond(2.0 * bound <= MAX_EXP2_SPAN, attn(True, 1024, None), attn(False, 256, 1024),
                  bound.reshape(1, 1))

    mats = _s5_matrices(s5_lam_re[0], s5_lam_im[0], s5_log_dt[0], s5_b_re[0], s5_b_im[0],
                        s5_c_re[0], s5_c_im[0], s5_d[0], _s5_segment((l + lc) // CHUNK))
    y = _s5_call(u, mats, lc)

    x2 = _merge_call(x1, modx, y, ox, ga, gs, bf(w_glu[0]), row(b_glu[0]), bf(w_pa[0]), bf(w_ps[0]),
                     bf(w_out[0]), 3, 256)
    return _ffn_call(x2, modx, row(norm_ffn2[0]), bf(w13_ffn2[0]), bf(w2_ffn2[0]), 6, 512, "ffn2_x")
```

```python
import functools
import math

import jax
import jax.numpy as jnp
import numpy as np
from jax import lax
from jax.experimental import pallas as pl
from jax.experimental.pallas import tpu as pltpu

F32 = jnp.float32
BF16 = jnp.bfloat16

EPS = 1e-6
ROPE_BASE = 10000.0
GRID_W = 64
N_HEADS = 8
HEAD_DIM = 64
V_DIM = 2 * HEAD_DIM
S5_GROUP = 16
S5_STATE = 64
N_MOD = 9
LANES = 128
CHUNK = 8
SUPER = LANES // S5_GROUP
VMEM_LIMIT = 56 * 1024 * 1024
LOG2E = math.log2(math.e)
FFN_SPLIT = 2
MAX_EXP2_SPAN = 100.0


def _cparams(sem):
    return pltpu.CompilerParams(dimension_semantics=sem, vmem_limit_bytes=VMEM_LIMIT)


def _resident(shape):
    nd = len(shape)
    return pl.BlockSpec(shape, lambda *_: (0,) * nd, pipeline_mode=pl.Buffered(1))


def _silu(a):
    return a * jax.nn.sigmoid(a)


def _modulated(x, g, mod, base):
    ms = jnp.mean(x * x, axis=-1, keepdims=True)
    xn = x * lax.rsqrt(ms + EPS) * g
    return xn * (1.0 + mod[base + 1:base + 2, :]) + mod[base:base + 1, :]


def _mod_kernel(c_ref, w_ref, b_ref, o_ref):
    a = _silu(c_ref[...]).astype(BF16)
    o_ref[...] = jnp.dot(a, w_ref[...].astype(BF16), preferred_element_type=F32) + b_ref[...]


def _mod_call(cc, w_mod, b_mod):
    rows, d = cc.shape
    n = w_mod.shape[1]
    tn = d
    return pl.pallas_call(
        _mod_kernel,
        out_shape=jax.ShapeDtypeStruct((rows, n), F32),
        grid=(n // tn,),
        in_specs=[pl.BlockSpec((rows, d), lambda j: (0, 0)),
                  pl.BlockSpec((d, tn), lambda j: (0, j)),
                  pl.BlockSpec((1, tn), lambda j: (0, j))],
        out_specs=pl.BlockSpec((rows, tn), lambda j: (0, j)),
        compiler_params=_cparams(("arbitrary",)),
        name="mod",
    )(cc, w_mod, b_mod)


def _ffn_kernel(x_ref, mod_ref, g_ref, w13_ref, w2_ref, o_ref, *, base):
    mod = mod_ref[0]
    dff = w2_ref.shape[0]
    rows = x_ref.shape[1] // FFN_SPLIT
    for r in range(FFN_SPLIT):
        x = x_ref[0, r * rows:(r + 1) * rows, :]
        xm = _modulated(x, g_ref[...], mod, base).astype(BF16)
        h = jnp.dot(xm, w13_ref[...], preferred_element_type=F32)
        act = (_silu(h[:, :dff]) * h[:, dff:]).astype(BF16)
        y = jnp.dot(act, w2_ref[...], preferred_element_type=F32)
        o_ref[0, r * rows:(r + 1) * rows, :] = x + (0.5 * mod[base + 2:base + 3, :]) * y


def _ffn_call(x, mod, g, w13, w2, base, tm, name):
    b, l, d = x.shape
    mod_map = (lambda bi, i: (bi, 0, 0)) if mod.shape[0] == b else (lambda bi, i: (0, 0, 0))
    return pl.pallas_call(
        functools.partial(_ffn_kernel, base=base),
        out_shape=jax.ShapeDtypeStruct(x.shape, F32),
        grid=(b, l // tm),
        in_specs=[pl.BlockSpec((1, tm, d), lambda bi, i: (bi, i, 0)),
                  pl.BlockSpec((1, N_MOD, d), mod_map),
                  _resident(g.shape), _resident(w13.shape), _resident(w2.shape)],
        out_specs=pl.BlockSpec((1, tm, d), lambda bi, i: (bi, i, 0)),
        compiler_params=_cparams(("parallel", "parallel")),
        name=name,
    )(x, mod, g, w13, w2)


def _qk_norm(t, gain, e1_ref, e2_ref):
    ss = jnp.dot((t * t).astype(BF16), e1_ref[...], preferred_element_type=F32)
    r = lax.rsqrt(ss * (1.0 / HEAD_DIM) + EPS)
    r_hi = r.astype(BF16)
    r_lo = (r - r_hi.astype(F32)).astype(BF16)
    rb = jnp.dot(jnp.concatenate([r_hi, r_lo], axis=-1), e2_ref[...], preferred_element_type=F32)
    return t * rb * gain


def _rope(t, cos, sin_signed):
    n = t.shape[-1]
    half = HEAD_DIM // 2
    lane = lax.broadcasted_iota(jnp.int32, t.shape, 1)
    swapped = jnp.where(lane % HEAD_DIM < half, pltpu.roll(t, n - half, 1), pltpu.roll(t, half, 1))
    reps = n // cos.shape[-1]
    cosf = jnp.concatenate([cos] * reps, axis=-1)
    sinf = jnp.concatenate([sin_signed] * reps, axis=-1)
    return t * cosf + swapped * sinf


def _store_supergroups(u_ref, u):
    for g in range(u_ref.shape[1]):
        u_ref[0, g] = u[:, g * LANES:(g + 1) * LANES]


def _inproj_kernel(x_ref, mod_ref, g_ref, w_ref, qg_ref, kg_ref, e1_ref, e2_ref, cos_ref, sin_ref,
                   q_ref, k_ref, vt_ref, u_ref, ga_ref, gs_ref, *, base):
    d = x_ref.shape[-1]
    qc, kc, vc = d, 2 * d, 3 * d
    uc = vc + u_ref.shape[1] * LANES
    xm = _modulated(x_ref[0], g_ref[...], mod_ref[0], base).astype(BF16)
    proj = lambda lo, hi: jnp.dot(xm, w_ref[:, lo:hi], preferred_element_type=F32)
    cos = cos_ref[...]
    sin = sin_ref[...]
    q_ref[0] = _rope(_qk_norm(proj(0, qc), qg_ref[...], e1_ref, e2_ref), cos, sin).astype(BF16)
    k_ref[0] = _rope(_qk_norm(proj(qc, kc), kg_ref[...], e1_ref, e2_ref), cos, sin).astype(BF16)
    vt_ref[0] = proj(kc, vc).astype(BF16).T
    _store_supergroups(u_ref, proj(vc, uc))
    ga_ref[0] = jax.nn.sigmoid(proj(uc, uc + d)).astype(BF16)
    gs_ref[0] = jax.nn.sigmoid(proj(uc + d, uc + 2 * d)).astype(BF16)


def _inproj_call(x, lc, modx, g, w_in, qg, kg, e1, e2, cos, sin, s5w, base, tm):
    b, l, d = x.shape
    la = l + lc
    tok = lambda bi, i: (bi, i, 0)
    out_shape = (jax.ShapeDtypeStruct((b, l, d), BF16),
                 jax.ShapeDtypeStruct((b, la, d), BF16),
                 jax.ShapeDtypeStruct((b, d, la), BF16),
                 jax.ShapeDtypeStruct((b, s5w // LANES, la, LANES), F32),
                 jax.ShapeDtypeStruct((b, l, d), BF16),
                 jax.ShapeDtypeStruct((b, l, d), BF16))
    return pl.pallas_call(
        functools.partial(_inproj_kernel, base=base),
        out_shape=out_shape,
        grid=(b, l // tm),
        in_specs=[pl.BlockSpec((1, tm, d), tok),
                  pl.BlockSpec((1, N_MOD, d), lambda bi, i: (bi, 0, 0)),
                  _resident(g.shape), _resident(w_in.shape), _resident(qg.shape), _resident(kg.shape),
                  _resident(e1.shape), _resident(e2.shape),
                  pl.BlockSpec((tm, LANES), lambda bi, i: (i, 0)),
                  pl.BlockSpec((tm, LANES), lambda bi, i: (i, 0))],
        out_specs=(pl.BlockSpec((1, tm, d), tok),
                   pl.BlockSpec((1, tm, d), tok),
                   pl.BlockSpec((1, d, tm), lambda bi, i: (bi, 0, i)),
                   pl.BlockSpec((1, s5w // LANES, tm, LANES), lambda bi, i: (bi, 0, i, 0)),
                   pl.BlockSpec((1, tm, d), tok),
                   pl.BlockSpec((1, tm, d), tok)),
        compiler_params=_cparams(("parallel", "parallel")),
        name="inproj",
    )(x, modx, g, w_in, qg, kg, e1, e2, cos, sin)


def _ctx_kernel(ctx_ref, mod_ref, g1_ref, w13_ref, w2_ref, gm_ref, w_ref, kg_ref, e1_ref, e2_ref,
                k_in, vt_in, u_in, k_ref, vt_ref, u_ref, *, base):
    del k_in, vt_in, u_in
    d = ctx_ref.shape[-1]
    mod = mod_ref[0]
    x = ctx_ref[0]
    xm = _modulated(x, g1_ref[...], mod, 0).astype(BF16)
    h = jnp.dot(xm, w13_ref[...], preferred_element_type=F32)
    dff = w2_ref.shape[0]
    act = (_silu(h[:, :dff]) * h[:, dff:]).astype(BF16)
    x1 = x + (0.5 * mod[2:3, :]) * jnp.dot(act, w2_ref[...], preferred_element_type=F32)
    xm = _modulated(x1, gm_ref[...], mod, base).astype(BF16)
    p = jnp.dot(xm, w_ref[...], preferred_element_type=F32)
    k_ref[0] = _qk_norm(p[:, :d], kg_ref[...], e1_ref, e2_ref).astype(BF16)
    vt_ref[0] = p[:, d:2 * d].astype(BF16).T
    _store_supergroups(u_ref, p[:, 2 * d:])


def _ctx_call(ctx, modc, g1, w13, w2, gm, w_kvu, kg, e1, e2, k, vt, u, base):
    b, lc, d = ctx.shape
    sg = u.shape[1]
    last = k.shape[1] // lc - 1
    anyspec = pl.BlockSpec(memory_space=pl.ANY)
    return pl.pallas_call(
        functools.partial(_ctx_kernel, base=base),
        out_shape=(jax.ShapeDtypeStruct(k.shape, k.dtype), jax.ShapeDtypeStruct(vt.shape, vt.dtype),
                   jax.ShapeDtypeStruct(u.shape, u.dtype)),
        grid=(b,),
        in_specs=[pl.BlockSpec((1, lc, d), lambda bi: (bi, 0, 0)),
                  pl.BlockSpec((1, N_MOD, d), lambda bi: (0, 0, 0)),
                  _resident(g1.shape), _resident(w13.shape), _resident(w2.shape), _resident(gm.shape),
                  _resident(w_kvu.shape), _resident(kg.shape), _resident(e1.shape), _resident(e2.shape),
                  anyspec, anyspec, anyspec],
        out_specs=(pl.BlockSpec((1, lc, d), lambda bi: (bi, last, 0)),
                   pl.BlockSpec((1, d, lc), lambda bi: (bi, 0, last)),
                   pl.BlockSpec((1, sg, lc, LANES), lambda bi: (bi, 0, last, 0))),
        input_output_aliases={10: 0, 11: 1, 12: 2},
        compiler_params=_cparams(("parallel",)),
        name="ctx",
    )(ctx, modc, g1, w13, w2, gm, w_kvu, kg, e1, e2, k, vt, u)


def _attn_kernel(bound_ref, lam_ref, subln_ref, q_ref, k_ref, vt_ref, o_ref, *, lc, tk, lam_init, bounded):
    q = q_ref[0]
    la = k_ref.shape[1]
    lane = lax.broadcasted_iota(jnp.int32, q.shape, 1)
    zero = jnp.zeros_like(q)
    q_comp = (jnp.where(lane < HEAD_DIM, q, zero), jnp.where(lane >= HEAD_DIM, q, zero))
    lv = lam_ref[...]
    lam = (jnp.exp(jnp.sum(lv[0:1] * lv[1:2], axis=-1, keepdims=True))
           - jnp.exp(jnp.sum(lv[2:3] * lv[3:4], axis=-1, keepdims=True)) + lam_init)
    blocks = [(0, la)] if bounded else [(st, tk) for st in range(0, la - lc, tk)] + [(la - lc, lc)]
    nt_dims = (((1,), (1,)), ((), ()))
    heads = []
    for qz in q_comp:
        m = l = acc = None
        for st, sz in blocks:
            kb = k_ref[0, st:st + sz, :]
            vb = vt_ref[0, :, st:st + sz]
            s = lax.dot_general(kb, qz, nt_dims, preferred_element_type=F32)
            if bounded:
                p = jnp.exp2(s - bound_ref[...])
                l = jnp.sum(p, axis=0, keepdims=True)
                acc = jnp.dot(vb, p.astype(BF16), preferred_element_type=F32)
                continue
            bm = jnp.max(s, axis=0, keepdims=True)
            if m is None:
                m_new = bm
                p = jnp.exp2(s - m_new)
                l = jnp.sum(p, axis=0, keepdims=True)
                acc = jnp.dot(vb, p.astype(BF16), preferred_element_type=F32)
            else:
                m_new = jnp.maximum(m, bm)
                alpha = jnp.exp2(m - m_new)
                p = jnp.exp2(s - m_new)
                l = alpha * l + jnp.sum(p, axis=0, keepdims=True)
                acc = alpha * acc + jnp.dot(vb, p.astype(BF16), preferred_element_type=F32)
            m = m_new
        heads.append(acc / l)
    o = heads[0] - lam * heads[1]
    ms = jnp.mean(o * o, axis=0, keepdims=True)
    on = o * lax.rsqrt(ms + EPS) * (subln_ref[...] * (1.0 - lam_init))
    o_ref[0] = on.T.astype(BF16)


def _attn_call(bound, lamv, subln, q, k, vt, lc, lam_init, tq, tk, bounded):
    b, l, d = q.shape
    la = k.shape[1]
    return pl.pallas_call(
        functools.partial(_attn_kernel, lc=lc, tk=tk, lam_init=lam_init, bounded=bounded),
        out_shape=jax.ShapeDtypeStruct((b, l, d), BF16),
        grid=(b, N_HEADS, l // tq),
        in_specs=[pl.BlockSpec(bound.shape, lambda bi, h, i: (0, 0)),
                  pl.BlockSpec(lamv.shape, lambda bi, h, i: (0, 0)),
                  pl.BlockSpec(subln.shape, lambda bi, h, i: (0, 0)),
                  pl.BlockSpec((1, tq, V_DIM), lambda bi, h, i: (bi, i, h)),
                  pl.BlockSpec((1, la, V_DIM), lambda bi, h, i: (bi, 0, h)),
                  pl.BlockSpec((1, V_DIM, la), lambda bi, h, i: (bi, h, 0))],
        out_specs=pl.BlockSpec((1, tq, V_DIM), lambda bi, h, i: (bi, i, h)),
        compiler_params=_cparams(("parallel", "parallel", "arbitrary")),
        name="attn_bounded" if bounded else "attn_online",
    )(bound, lamv, subln, q, k, vt)


def _s5_matrices(lam_re, lam_im, log_dt, b_re, b_im, c_re, c_im, d_skip, seg):
    t = CHUNK
    g_all, p_n = lam_re.shape[1:]
    c_n = b_re.shape[-1]
    sg = g_all // SUPER
    ks = jnp.arange(t + 1, dtype=F32)[:, None, None]
    mats = []
    for dr in range(2):
        lre, lim = lam_re[dr], lam_im[dr]
        dt = jnp.exp(log_dt[dr])[:, None]
        mag = jnp.exp(ks * (lre * dt))
        akr = mag * jnp.cos(ks * (lim * dt))
        aki = mag * jnp.sin(ks * (lim * dt))
        ar, ai = akr[1], aki[1]
        den = lre * lre + lim * lim
        nr = ar - 1.0
        fr = (nr * lre + ai * lim) / den
        fi = (ai * lre - nr * lim) / den
        bbr = fr[..., None] * b_re[dr] - fi[..., None] * b_im[dr]
        bbi = fr[..., None] * b_im[dr] + fi[..., None] * b_re[dr]
        abr = akr[..., None] * bbr - aki[..., None] * bbi
        abi = akr[..., None] * bbi + aki[..., None] * bbr
        kern = (jnp.einsum('gdp,kgpc->kgdc', c_re[dr], abr)
                - jnp.einsum('gdp,kgpc->kgdc', c_im[dr], abi))
        car = c_re[dr][None] * akr[:, :, None, :] - c_im[dr][None] * aki[:, :, None, :]
        cai = c_re[dr][None] * aki[:, :, None, :] + c_im[dr][None] * akr[:, :, None, :]
        mats.append((akr, aki, abr, abi, kern, car, cai))

    s_idx = jnp.arange(t)
    lag = s_idx[None, :] - s_idx[:, None]
    kf, kb = mats[0][4], mats[1][4]
    d_diag = jnp.eye(c_n, dtype=F32)[None] * d_skip.reshape(g_all, 1, c_n)
    kf_l = kf[jnp.clip(lag, 0, t)]
    kb_l = kb[jnp.clip(-lag, 0, t)]
    fwd_on = (lag >= 0)[..., None, None, None]
    bwd_on = (lag <= 0)[..., None, None, None]
    same = (lag == 0)[..., None, None, None]
    mg = jnp.where(fwd_on, kf_l, 0.0) + jnp.where(bwd_on, kb_l, 0.0) + jnp.where(same, d_diag[None, None], 0.0)
    mg = mg.reshape(t, t, sg, SUPER, c_n, c_n)
    m = mg.transpose(2, 0, 3, 5, 1, 4).reshape(sg, t * SUPER * c_n, t * c_n)

    def state_in(abr, abi, order):
        wr = abr[order].reshape(t, sg, SUPER, p_n, c_n)
        wi = abi[order].reshape(t, sg, SUPER, p_n, c_n)
        w = jnp.stack([wr, wi], axis=0)
        return w.transpose(2, 1, 3, 5, 0, 4).reshape(sg, t * SUPER * c_n, 2 * p_n)

    def state_out(car, cai, order):
        vr = car[order].reshape(t, sg, SUPER, c_n, p_n)
        vi = -cai[order].reshape(t, sg, SUPER, c_n, p_n)
        v = jnp.stack([vr, vi], axis=0)
        return v.transpose(2, 0, 3, 5, 1, 4).reshape(sg, 2 * SUPER * p_n, t * c_n)

    wf = state_in(mats[0][2], mats[0][3], (t - 1) - s_idx)
    wb = state_in(mats[1][2], mats[1][3], s_idx)
    vf = state_out(mats[0][5], mats[0][6], s_idx + 1)
    vb = state_out(mats[1][5], mats[1][6], t - s_idx)

    def decay_table(dr, powers):
        lre, lim = lam_re[dr], lam_im[dr]
        dt = jnp.exp(log_dt[dr])[:, None]
        kk = (t * powers).astype(F32)[:, None, None]
        mag = jnp.exp(kk * (lre * dt))
        tab = jnp.stack([mag * jnp.cos(kk * (lim * dt)), mag * jnp.sin(kk * (lim * dt))], axis=0)
        return tab.reshape(2, -1, sg, SUPER * p_n).transpose(2, 0, 1, 3)

    j = jnp.arange(seg)
    bf = lambda a: a.astype(BF16)
    return (bf(m), bf(wf), bf(wb), bf(vf), bf(vb), decay_table(0, j + 1), decay_table(1, seg - j))


def _s5_segment(n):
    return 8 * (-(-n // 64))


def _cmul_add(ar, ai, xr, xi, br, bi):
    return ar * xr - ai * xi + br, ar * xi + ai * xr + bi


def _group_replicator(outer, inner):
    col = np.arange(outer * SUPER * inner)
    src = (col // (SUPER * inner)) * inner + col % inner
    return jnp.asarray(np.arange(outer * inner)[:, None] == src[None, :], dtype=BF16)


def _expand_groups(xc, rep, row_block, col_block):
    dense = jnp.dot(xc, rep, preferred_element_type=F32)
    r = lax.broadcasted_iota(jnp.int32, dense.shape, 0)
    c = lax.broadcasted_iota(jnp.int32, dense.shape, 1)
    own = (r // row_block) % SUPER == (c // col_block) % SUPER
    return jnp.where(own, dense, 0.0).astype(BF16)


def _s5_kernel(u_ref, mc_ref, wfc_ref, wbc_ref, vfc_ref, vbc_ref, rep_out_ref, rep_state_ref, pf_ref, pb_ref,
               y_ref, m_ref, wf_ref, wb_ref, vf_ref, vb_ref, sf_ref, sb_ref, *, nx, nc):
    @pl.when(pl.program_id(1) == 0)
    def _build_matrices():
        rep_out, rep_state = rep_out_ref[...], rep_state_ref[...]
        m_ref[...] = _expand_groups(mc_ref[0], rep_out, S5_GROUP, S5_GROUP)
        wf_ref[...] = _expand_groups(wfc_ref[0], rep_state, S5_GROUP, S5_STATE)
        wb_ref[...] = _expand_groups(wbc_ref[0], rep_state, S5_GROUP, S5_STATE)
        vf_ref[...] = _expand_groups(vfc_ref[0], rep_out, S5_STATE, S5_GROUP)
        vb_ref[...] = _expand_groups(vbc_ref[0], rep_out, S5_STATE, S5_GROUP)

    n = nx + nc
    tiles, npad = sf_ref.shape[1:3]
    seg = npad // 8
    pad = npad - n
    half = tiles * LANES

    def load(ref, rows):
        return tuple(jnp.concatenate([ref[ri, c, rows, :] for c in range(tiles)], axis=-1) for ri in range(2))

    def store(ref, rows, re, im):
        for c in range(tiles):
            ref[0, c, rows, :] = re[:, c * LANES:(c + 1) * LANES]
            ref[1, c, rows, :] = im[:, c * LANES:(c + 1) * LANES]

    z = jnp.concatenate([u_ref[0, 0, pl.ds(t, n, stride=CHUNK), :] for t in range(CHUNK)], axis=-1).astype(BF16)
    zl, zc = z[:nx], z[nx:]
    zpad = [jnp.zeros((pad, z.shape[1]), BF16)] if pad else []
    s = jnp.dot(jnp.concatenate([zc, zl] + zpad, axis=0), wf_ref[...], preferred_element_type=F32)
    store(sf_ref, slice(None), s[:, :half], s[:, half:])
    s = jnp.dot(jnp.concatenate(zpad + [zl, zc], axis=0), wb_ref[...], preferred_element_type=F32)
    store(sb_ref, slice(None), s[:, :half], s[:, half:])

    afr, afi = pf_ref[0, 0, 0:1, :], pf_ref[0, 1, 0:1, :]
    abr, abi = pb_ref[0, 0, seg - 1:seg, :], pb_ref[0, 1, seg - 1:seg, :]

    def step(i, carry):
        hr, hi, gr, gi = carry
        rows = pl.ds(i, 8, stride=seg)
        hr, hi = _cmul_add(afr, afi, hr, hi, *load(sf_ref, rows))
        store(sf_ref, rows, hr, hi)
        rows = pl.ds(seg - 1 - i, 8, stride=seg)
        gr, gi = _cmul_add(abr, abi, gr, gi, *load(sb_ref, rows))
        store(sb_ref, rows, gr, gi)
        return hr, hi, gr, gi

    zero = jnp.zeros((8, half), F32)
    hr, hi, gr, gi = lax.fori_loop(0, seg, step, (zero, zero, zero, zero))

    tfr, tfi = pf_ref[0, 0], pf_ref[0, 1]
    tbr, tbi = pb_ref[0, 0], pb_ref[0, 1]
    cr = ci = None
    for s in range(1, 8):
        er, ei = hr[s - 1:s], hi[s - 1:s]
        if cr is None:
            cr, ci = er, ei
        else:
            cr, ci = _cmul_add(tfr[seg - 1:seg], tfi[seg - 1:seg], cr, ci, er, ei)
        rows = slice(s * seg, (s + 1) * seg)
        store(sf_ref, rows, *_cmul_add(tfr, tfi, cr, ci, *load(sf_ref, rows)))
    cr = ci = None
    for s in range(6, -1, -1):
        er, ei = gr[s + 1:s + 2], gi[s + 1:s + 2]
        if cr is None:
            cr, ci = er, ei
        else:
            cr, ci = _cmul_add(tbr[0:1], tbi[0:1], cr, ci, er, ei)
        rows = slice(s * seg, (s + 1) * seg)
        store(sb_ref, rows, *_cmul_add(tbr, tbi, cr, ci, *load(sb_ref, rows)))

    def entering(ref, shift, first):
        cols = [pltpu.roll(ref[ri, c], shift, 0)[first:first + nx] for ri in range(2) for c in range(tiles)]
        return jnp.concatenate(cols, axis=-1).astype(BF16)

    y = (jnp.dot(zl, m_ref[...], preferred_element_type=F32)
         + jnp.dot(entering(sf_ref, 1, nc), vf_ref[...], preferred_element_type=F32)
         + jnp.dot(entering(sb_ref, npad - 1, pad), vb_ref[...], preferred_element_type=F32))
    for t in range(CHUNK):
        y_ref[0, 0, pl.ds(t, nx, stride=CHUNK), :] = y[:, t * LANES:(t + 1) * LANES]


def _s5_call(u, mats, lc):
    b, sg, la, _ = u.shape
    t = CHUNK
    n, nc = la // t, lc // t
    nx = n - nc
    npad = 8 * _s5_segment(n)
    width = t * LANES
    tiles = width // (2 * LANES)
    mc, wfc, wbc, vfc, vbc, pf, pb = mats
    rep_out = _group_replicator(t, S5_GROUP)
    rep_state = _group_replicator(2, S5_STATE)
    wspec = lambda a: pl.BlockSpec((1,) + a.shape[1:], lambda g, bi: (g,) + (0,) * (a.ndim - 1))
    dense = pltpu.VMEM((width, width), BF16)
    states = pltpu.VMEM((2, tiles, npad, LANES), F32)
    return pl.pallas_call(
        functools.partial(_s5_kernel, nx=nx, nc=nc),
        out_shape=jax.ShapeDtypeStruct((b, sg, nx * t, LANES), F32),
        grid=(sg, b),
        in_specs=[pl.BlockSpec((1, 1, la, LANES), lambda g, bi: (bi, g, 0, 0))]
                 + [wspec(a) for a in (mc, wfc, wbc, vfc, vbc)]
                 + [_resident(rep_out.shape), _resident(rep_state.shape), wspec(pf), wspec(pb)],
        out_specs=pl.BlockSpec((1, 1, nx * t, LANES), lambda g, bi: (bi, g, 0, 0)),
        scratch_shapes=[dense] * 5 + [states] * 2,
        compiler_params=_cparams(("arbitrary", "arbitrary")),
        name="s5",
    )(u, mc, wfc, wbc, vfc, vbc, rep_out, rep_state, pf, pb)


def _gelu_tanh(y):
    return 0.5 * y * (1.0 + jnp.tanh(math.sqrt(2.0 / math.pi) * (y + 0.044715 * (y * y * y))))


def _merge_kernel(x_ref, mod_ref, y_ref, ox_ref, ga_ref, gs_ref, wglu_ref, bglu_ref, wpa_ref, wps_ref,
                  wout_ref, o_ref, *, base):
    ge = _gelu_tanh(jnp.concatenate([y_ref[0, g] for g in range(y_ref.shape[1])], axis=-1))
    z = jnp.dot(ge.astype(BF16), wglu_ref[...], preferred_element_type=F32) + bglu_ref[...]
    sx = (ge * jax.nn.sigmoid(z)).astype(BF16)
    mix = (ga_ref[0].astype(F32) * jnp.dot(ox_ref[0], wpa_ref[...], preferred_element_type=F32)
           + gs_ref[0].astype(F32) * jnp.dot(sx, wps_ref[...], preferred_element_type=F32))
    out = jnp.dot(mix.astype(BF16), wout_ref[...], preferred_element_type=F32)
    o_ref[0] = x_ref[0] + mod_ref[0][base + 2:base + 3, :] * out


def _merge_call(x, mod, y, ox, ga, gs, w_glu, b_glu, w_pa, w_ps, w_out, base, tm):
    b, l, d = x.shape
    sg = y.shape[1]
    tok = lambda width: pl.BlockSpec((1, tm, width), lambda bi, i: (bi, i, 0))
    return pl.pallas_call(
        functools.partial(_merge_kernel, base=base),
        out_shape=jax.ShapeDtypeStruct(x.shape, F32),
        grid=(b, l // tm),
        in_specs=[tok(d), pl.BlockSpec((1, N_MOD, d), lambda bi, i: (bi, 0, 0)),
                  pl.BlockSpec((1, sg, tm, LANES), lambda bi, i: (bi, 0, i, 0)), tok(d), tok(d), tok(d),
                  _resident(w_glu.shape), _resident(b_glu.shape), _resident(w_pa.shape),
                  _resident(w_ps.shape), _resident(w_out.shape)],
        out_specs=tok(d),
        compiler_params=_cparams(("parallel", "parallel")),
        name="merge",
    )(x, mod, y, ox, ga, gs, w_glu, b_glu, w_pa, w_ps, w_out)


def _rope_tables(l):
    n_freq = HEAD_DIM // 4
    inv = ROPE_BASE ** (-jnp.arange(n_freq, dtype=F32) / n_freq)
    pos = jnp.arange(l, dtype=jnp.int32)
    row = (pos // GRID_W).astype(F32)
    col = (pos % GRID_W).astype(F32)
    ang = jnp.concatenate([row[:, None] * inv, col[:, None] * inv], axis=-1)
    cos, sin = jnp.cos(ang), jnp.sin(ang)
    return jnp.concatenate([cos] * 4, axis=-1), jnp.concatenate([-sin, sin, -sin, sin], axis=-1)


def _group_sum_matrices(d):
    grp = jnp.arange(d) // HEAD_DIM
    e1 = (grp[:, None] == jnp.arange(LANES)[None, :]).astype(BF16)
    e2 = jnp.concatenate([e1.T, e1.T], axis=0)
    return e1, e2


def kernel(x, c, ctx, c_ctx, w_mod, b_mod, norm_ffn1, w13_ffn1, w2_ffn1, norm_mix, w_in, q_norm, k_norm, lam_q1, lam_k1, lam_q2, lam_k2, subln, s5_lam_re, s5_lam_im, s5_log_dt, s5_b_re, s5_b_im, s5_c_re, s5_c_im, s5_d, w_glu, b_glu, w_pa, w_ps, w_out, norm_ffn2, w13_ffn2, w2_ffn2):
    assert w_mod.shape[0] == 1, "single-layer block"
    b, l, d = x.shape
    lc = ctx.shape[1]
    lam_init = 0.8 - 0.6 * math.exp(-0.3 * 0)
    bf = lambda a: a.astype(BF16)
    row = lambda a: a.reshape(1, -1)

    rows = b + 1
    pad = (-rows) % 8
    cc = jnp.concatenate([c, c_ctx[None, :], jnp.zeros((pad, d), F32)], axis=0)
    mod = _mod_call(cc, w_mod[0], row(b_mod[0]))
    modx = mod[:b].reshape(b, N_MOD, d)
    modc = mod[b:b + 1].reshape(1, N_MOD, d)

    w13_1, w2_1 = bf(w13_ffn1[0]), bf(w2_ffn1[0])
    x1 = _ffn_call(x, modx, row(norm_ffn1[0]), w13_1, w2_1, 0, 512, "ffn1_x")

    e1, e2 = _group_sum_matrices(d)
    cos, sin = _rope_tables(l)
    qg = row(jnp.tile(q_norm[0], d // HEAD_DIM)) * (LOG2E / math.sqrt(HEAD_DIM))
    kg = row(jnp.tile(k_norm[0], d // HEAD_DIM))
    w_inb = bf(w_in[0])
    s5w = s5_d.shape[-1]
    q, k, vt, u, ga, gs = _inproj_call(x1, lc, modx, row(norm_mix[0]), w_inb, qg, kg, e1, e2,
                                       cos, sin, s5w, 3, 512)
    k, vt, u = _ctx_call(ctx, modc, row(norm_ffn1[0]), w13_1, w2_1, row(norm_mix[0]),
                         w_inb[:, d:3 * d + s5w], kg, e1, e2, k, vt, u, 3)

    lamv = jnp.stack([lam_q1[0], lam_k1[0], lam_q2[0], lam_k2[0]], axis=0)
    bound = (LOG2E * math.sqrt(HEAD_DIM)) * jnp.max(jnp.abs(q_norm[0])) * jnp.max(jnp.abs(k_norm[0]))
    attn = lambda bounded, tq, tk: functools.partial(
        _attn_call, lamv=lamv, subln=subln[0].reshape(-1, 1), q=q, k=k, vt=vt, lc=lc, lam_init=lam_init,
        tq=tq, tk=tk, bounded=bounded)
    ox = lax.cond(2.0 * bound <= MAX_EXP2_SPAN, attn(True, 1024, None), attn(False, 256, 1024),
                  bound.reshape(1, 1))

    mats = _s5_matrices(s5_lam_re[0], s5_lam_im[0], s5_log_dt[0], s5_b_re[0], s5_b_im[0],
                        s5_c_re[0], s5_c_im[0], s5_d[0], _s5_segment((l + lc) // CHUNK))
    y = _s5_call(u, mats, lc)

    x2 = _merge_call(x1, modx, y, ox, ga, gs, bf(w_glu[0]), row(b_glu[0]), bf(w_pa[0]), bf(w_ps[0]),
                     bf(w_out[0]), 3, 512)
    return _ffn_call(x2, modx, row(norm_ffn2[0]), bf(w13_ffn2[0]), bf(w2_ffn2[0]), 6, 512, "ffn2_x")
```

```python
import functools
import math

import jax
import jax.numpy as jnp
import numpy as np
from jax import lax
from jax.experimental import pallas as pl
from jax.experimental.pallas import tpu as pltpu

F32 = jnp.float32
BF16 = jnp.bfloat16

EPS = 1e-6
ROPE_BASE = 10000.0
GRID_W = 64
N_HEADS = 8
HEAD_DIM = 64
V_DIM = 2 * HEAD_DIM
S5_GROUP = 16
S5_STATE = 64
N_MOD = 9
LANES = 128
MXU_WIDTH = 256
CHUNK = 8
SUPER = LANES // S5_GROUP
VMEM_LIMIT = 56 * 1024 * 1024
LOG2E = math.log2(math.e)
FFN_SPLIT = 2
MAX_EXP2_SPAN = 100.0


def _cparams(sem):
    return pltpu.CompilerParams(dimension_semantics=sem, vmem_limit_bytes=VMEM_LIMIT)


def _resident(shape):
    nd = len(shape)
    return pl.BlockSpec(shape, lambda *_: (0,) * nd, pipeline_mode=pl.Buffered(1))


def _silu(a):
    return a * jax.nn.sigmoid(a)


def _modulated(x, g, mod, base):
    ms = jnp.mean(x * x, axis=-1, keepdims=True)
    xn = x * lax.rsqrt(ms + EPS) * g
    return xn * (1.0 + mod[base + 1:base + 2, :]) + mod[base:base + 1, :]


def _mod_kernel(c_ref, w_ref, b_ref, o_ref):
    a = _silu(c_ref[...]).astype(BF16)
    o_ref[...] = jnp.dot(a, w_ref[...].astype(BF16), preferred_element_type=F32) + b_ref[...]


def _mod_call(cc, w_mod, b_mod):
    rows, d = cc.shape
    n = w_mod.shape[1]
    tn = d
    return pl.pallas_call(
        _mod_kernel,
        out_shape=jax.ShapeDtypeStruct((rows, n), F32),
        grid=(n // tn,),
        in_specs=[pl.BlockSpec((rows, d), lambda j: (0, 0)),
                  pl.BlockSpec((d, tn), lambda j: (0, j)),
                  pl.BlockSpec((1, tn), lambda j: (0, j))],
        out_specs=pl.BlockSpec((rows, tn), lambda j: (0, j)),
        compiler_params=_cparams(("arbitrary",)),
        name="mod",
    )(cc, w_mod, b_mod)


def _ffn_rows(x, g, mod, base, w13_ref, w2_ref):
    dff = w2_ref.shape[0]
    xm = _modulated(x, g, mod, base).astype(BF16)
    h = jnp.dot(xm, w13_ref[...], preferred_element_type=F32)
    act = (_silu(h[:, :dff]) * h[:, dff:]).astype(BF16)
    return x + (0.5 * mod[base + 2:base + 3, :]) * jnp.dot(act, w2_ref[...], preferred_element_type=F32)


def _ffn_kernel(x_ref, mod_ref, g_ref, w13_ref, w2_ref, o_ref, *, base):
    rows = x_ref.shape[1] // FFN_SPLIT
    for r in range(FFN_SPLIT):
        rs = slice(r * rows, (r + 1) * rows)
        o_ref[0, rs, :] = _ffn_rows(x_ref[0, rs, :], g_ref[...], mod_ref[0], base, w13_ref, w2_ref)


def _ffn_call(x, mod, g, w13, w2, base, tm, name):
    b, l, d = x.shape
    return pl.pallas_call(
        functools.partial(_ffn_kernel, base=base),
        out_shape=jax.ShapeDtypeStruct(x.shape, F32),
        grid=(b, l // tm),
        in_specs=[pl.BlockSpec((1, tm, d), lambda bi, i: (bi, i, 0)),
                  pl.BlockSpec((1, N_MOD, d), lambda bi, i: (bi, 0, 0)),
                  _resident(g.shape), _resident(w13.shape), _resident(w2.shape)],
        out_specs=pl.BlockSpec((1, tm, d), lambda bi, i: (bi, i, 0)),
        compiler_params=_cparams(("parallel", "parallel")),
        name=name,
    )(x, mod, g, w13, w2)


def _qk_norm(t, gain, ones_ref):
    sq = (t * t).astype(BF16)
    w = ones_ref.shape[0]
    ss = jnp.concatenate([jnp.dot(sq[:, j:j + w], ones_ref[...], preferred_element_type=F32)
                          for j in range(0, t.shape[-1], w)], axis=-1)
    return t * lax.rsqrt(ss * (1.0 / HEAD_DIM) + EPS) * gain


def _rope(t, cos, sin_signed):
    n = t.shape[-1]
    half = HEAD_DIM // 2
    lane = lax.broadcasted_iota(jnp.int32, t.shape, 1)
    swapped = jnp.where(lane % HEAD_DIM < half, pltpu.roll(t, n - half, 1), pltpu.roll(t, half, 1))
    reps = n // cos.shape[-1]
    cosf = jnp.concatenate([cos] * reps, axis=-1)
    sinf = jnp.concatenate([sin_signed] * reps, axis=-1)
    return t * cosf + swapped * sinf


def _store_supergroups(u_ref, u):
    for g in range(u_ref.shape[1]):
        u_ref[0, g] = u[:, g * LANES:(g + 1) * LANES]


def _inproj_kernel(x_ref, mod_ref, g_ref, w_ref, qg_ref, kg_ref, ones_ref, cos_ref, sin_ref,
                   q_ref, k_ref, vt_ref, u_ref, ga_ref, gs_ref, *, base):
    d = x_ref.shape[-1]
    qc, kc, vc = d, 2 * d, 3 * d
    uc = vc + u_ref.shape[1] * LANES
    xm = _modulated(x_ref[0], g_ref[...], mod_ref[0], base).astype(BF16)
    proj = lambda lo, hi: jnp.dot(xm, w_ref[:, lo:hi], preferred_element_type=F32)
    cos = cos_ref[...]
    sin = sin_ref[...]
    tq = proj(0, qc)
    tk = proj(qc, kc)
    q_ref[0] = _rope(_qk_norm(tq, qg_ref[...], ones_ref), cos, sin).astype(BF16)
    vt_ref[0] = proj(kc, vc).astype(BF16).T
    k_ref[0] = _rope(_qk_norm(tk, kg_ref[...], ones_ref), cos, sin).astype(BF16)
    _store_supergroups(u_ref, proj(vc, uc))
    ga_ref[0] = jax.nn.sigmoid(proj(uc, uc + d)).astype(BF16)
    gs_ref[0] = jax.nn.sigmoid(proj(uc + d, uc + 2 * d)).astype(BF16)


def _inproj_call(x, lc, modx, g, w_in, qg, kg, ones, cos, sin, s5w, base, tm):
    b, l, d = x.shape
    la = l + lc
    tok = lambda bi, i: (bi, i, 0)
    out_shape = (jax.ShapeDtypeStruct((b, l, d), BF16),
                 jax.ShapeDtypeStruct((b, la, d), BF16),
                 jax.ShapeDtypeStruct((b, d, la), BF16),
                 jax.ShapeDtypeStruct((b, s5w // LANES, la, LANES), F32),
                 jax.ShapeDtypeStruct((b, l, d), BF16),
                 jax.ShapeDtypeStruct((b, l, d), BF16))
    return pl.pallas_call(
        functools.partial(_inproj_kernel, base=base),
        out_shape=out_shape,
        grid=(b, l // tm),
        in_specs=[pl.BlockSpec((1, tm, d), tok),
                  pl.BlockSpec((1, N_MOD, d), lambda bi, i: (bi, 0, 0)),
                  _resident(g.shape), _resident(w_in.shape), _resident(qg.shape), _resident(kg.shape),
                  _resident(ones.shape),
                  pl.BlockSpec((tm, LANES), lambda bi, i: (i, 0)),
                  pl.BlockSpec((tm, LANES), lambda bi, i: (i, 0))],
        out_specs=(pl.BlockSpec((1, tm, d), tok),
                   pl.BlockSpec((1, tm, d), tok),
                   pl.BlockSpec((1, d, tm), lambda bi, i: (bi, 0, i)),
                   pl.BlockSpec((1, s5w // LANES, tm, LANES), lambda bi, i: (bi, 0, i, 0)),
                   pl.BlockSpec((1, tm, d), tok),
                   pl.BlockSpec((1, tm, d), tok)),
        compiler_params=_cparams(("parallel", "parallel")),
        name="inproj",
    )(x, modx, g, w_in, qg, kg, ones, cos, sin)


def _ctx_kernel(ctx_ref, mod_ref, g1_ref, w13_ref, w2_ref, gm_ref, w_ref, kg_ref, ones_ref,
                k_in, vt_in, u_in, k_ref, vt_ref, u_ref, *, base):
    del k_in, vt_in, u_in
    d = ctx_ref.shape[-1]
    mod = mod_ref[0]
    x1 = _ffn_rows(ctx_ref[0], g1_ref[...], mod, 0, w13_ref, w2_ref)
    xm = _modulated(x1, gm_ref[...], mod, base).astype(BF16)
    p = jnp.dot(xm, w_ref[...], preferred_element_type=F32)
    k_ref[0] = _qk_norm(p[:, :d], kg_ref[...], ones_ref).astype(BF16)
    vt_ref[0] = p[:, d:2 * d].astype(BF16).T
    _store_supergroups(u_ref, p[:, 2 * d:])


def _ctx_call(ctx, modc, g1, w13, w2, gm, w_kvu, kg, ones, k, vt, u, base):
    b, lc, d = ctx.shape
    sg = u.shape[1]
    last = k.shape[1] // lc - 1
    anyspec = pl.BlockSpec(memory_space=pl.ANY)
    return pl.pallas_call(
        functools.partial(_ctx_kernel, base=base),
        out_shape=(jax.ShapeDtypeStruct(k.shape, k.dtype), jax.ShapeDtypeStruct(vt.shape, vt.dtype),
                   jax.ShapeDtypeStruct(u.shape, u.dtype)),
        grid=(b,),
        in_specs=[pl.BlockSpec((1, lc, d), lambda bi: (bi, 0, 0)),
                  pl.BlockSpec((1, N_MOD, d), lambda bi: (0, 0, 0)),
                  _resident(g1.shape), _resident(w13.shape), _resident(w2.shape), _resident(gm.shape),
                  _resident(w_kvu.shape), _resident(kg.shape), _resident(ones.shape),
                  anyspec, anyspec, anyspec],
        out_specs=(pl.BlockSpec((1, lc, d), lambda bi: (bi, last, 0)),
                   pl.BlockSpec((1, d, lc), lambda bi: (bi, 0, last)),
                   pl.BlockSpec((1, sg, lc, LANES), lambda bi: (bi, 0, last, 0))),
        input_output_aliases={9: 0, 10: 1, 11: 2},
        compiler_params=_cparams(("parallel",)),
        name="ctx",
    )(ctx, modc, g1, w13, w2, gm, w_kvu, kg, ones, k, vt, u)


def _attn_kernel(bound_ref, lam_ref, subln_ref, q_ref, k_ref, vt_ref, o_ref, *, lc, tk, lam_init, bounded):
    q = q_ref[0]
    la = k_ref.shape[1]
    lane = lax.broadcasted_iota(jnp.int32, q.shape, 1)
    zero = jnp.zeros_like(q)
    q_comp = (jnp.where(lane < HEAD_DIM, q, zero), jnp.where(lane >= HEAD_DIM, q, zero))
    lv = lam_ref[...]
    lam = (jnp.exp(jnp.sum(lv[0:1] * lv[1:2], axis=-1, keepdims=True))
           - jnp.exp(jnp.sum(lv[2:3] * lv[3:4], axis=-1, keepdims=True)) + lam_init)
    blocks = [(0, la)] if bounded else [(st, tk) for st in range(0, la - lc, tk)] + [(la - lc, lc)]
    nt_dims = (((1,), (1,)), ((), ()))
    heads = []
    for qz in q_comp:
        m = l = acc = None
        for st, sz in blocks:
            kb = k_ref[0, st:st + sz, :]
            vb = vt_ref[0, :, st:st + sz]
            s = lax.dot_general(kb, qz, nt_dims, preferred_element_type=F32)
            if bounded:
                p = jnp.exp2(s - bound_ref[...])
                l = jnp.sum(p, axis=0, keepdims=True)
                acc = jnp.dot(vb, p.astype(BF16), preferred_element_type=F32)
                continue
            bm = jnp.max(s, axis=0, keepdims=True)
            if m is None:
                m_new = bm
                p = jnp.exp2(s - m_new)
                l = jnp.sum(p, axis=0, keepdims=True)
                acc = jnp.dot(vb, p.astype(BF16), preferred_element_type=F32)
            else:
                m_new = jnp.maximum(m, bm)
                alpha = jnp.exp2(m - m_new)
                p = jnp.exp2(s - m_new)
                l = alpha * l + jnp.sum(p, axis=0, keepdims=True)
                acc = alpha * acc + jnp.dot(vb, p.astype(BF16), preferred_element_type=F32)
            m = m_new
        heads.append(acc / l)
    o = heads[0] - lam * heads[1]
    ms = jnp.mean(o * o, axis=0, keepdims=True)
    on = o * lax.rsqrt(ms + EPS) * (subln_ref[...] * (1.0 - lam_init))
    o_ref[0] = on.T.astype(BF16)


def _attn_call(bound, lamv, subln, q, k, vt, lc, lam_init, tq, tk, bounded):
    b, l, d = q.shape
    la = k.shape[1]
    return pl.pallas_call(
        functools.partial(_attn_kernel, lc=lc, tk=tk, lam_init=lam_init, bounded=bounded),
        out_shape=jax.ShapeDtypeStruct((b, l, d), BF16),
        grid=(b, N_HEADS, l // tq),
        in_specs=[pl.BlockSpec(bound.shape, lambda bi, h, i: (0, 0)),
                  pl.BlockSpec(lamv.shape, lambda bi, h, i: (0, 0)),
                  pl.BlockSpec(subln.shape, lambda bi, h, i: (0, 0)),
                  pl.BlockSpec((1, tq, V_DIM), lambda bi, h, i: (bi, i, h)),
                  pl.BlockSpec((1, la, V_DIM), lambda bi, h, i: (bi, 0, h)),
                  pl.BlockSpec((1, V_DIM, la), lambda bi, h, i: (bi, h, 0))],
        out_specs=pl.BlockSpec((1, tq, V_DIM), lambda bi, h, i: (bi, i, h)),
        compiler_params=_cparams(("parallel", "parallel", "arbitrary")),
        name="attn_bounded" if bounded else "attn_online",
    )(bound, lamv, subln, q, k, vt)


def _s5_matrices(lam_re, lam_im, log_dt, b_re, b_im, c_re, c_im, d_skip, seg):
    t = CHUNK
    g_all, p_n = lam_re.shape[1:]
    c_n = b_re.shape[-1]
    sg = g_all // SUPER
    ks = jnp.arange(t + 1, dtype=F32)[:, None, None]
    mats = []
    for dr in range(2):
        lre, lim = lam_re[dr], lam_im[dr]
        dt = jnp.exp(log_dt[dr])[:, None]
        mag = jnp.exp(ks * (lre * dt))
        akr = mag * jnp.cos(ks * (lim * dt))
        aki = mag * jnp.sin(ks * (lim * dt))
        ar, ai = akr[1], aki[1]
        den = lre * lre + lim * lim
        nr = ar - 1.0
        fr = (nr * lre + ai * lim) / den
        fi = (ai * lre - nr * lim) / den
        bbr = fr[..., None] * b_re[dr] - fi[..., None] * b_im[dr]
        bbi = fr[..., None] * b_im[dr] + fi[..., None] * b_re[dr]
        abr = akr[..., None] * bbr - aki[..., None] * bbi
        abi = akr[..., None] * bbi + aki[..., None] * bbr
        kern = (jnp.einsum('gdp,kgpc->kgdc', c_re[dr], abr)
                - jnp.einsum('gdp,kgpc->kgdc', c_im[dr], abi))
        car = c_re[dr][None] * akr[:, :, None, :] - c_im[dr][None] * aki[:, :, None, :]
        cai = c_re[dr][None] * aki[:, :, None, :] + c_im[dr][None] * akr[:, :, None, :]
        mats.append((akr, aki, abr, abi, kern, car, cai))

    s_idx = jnp.arange(t)
    lag = s_idx[None, :] - s_idx[:, None]
    kf, kb = mats[0][4], mats[1][4]
    d_diag = jnp.eye(c_n, dtype=F32)[None] * d_skip.reshape(g_all, 1, c_n)
    kf_l = kf[jnp.clip(lag, 0, t)]
    kb_l = kb[jnp.clip(-lag, 0, t)]
    fwd_on = (lag >= 0)[..., None, None, None]
    bwd_on = (lag <= 0)[..., None, None, None]
    same = (lag == 0)[..., None, None, None]
    mg = jnp.where(fwd_on, kf_l, 0.0) + jnp.where(bwd_on, kb_l, 0.0) + jnp.where(same, d_diag[None, None], 0.0)
    mg = mg.reshape(t, t, sg, SUPER, c_n, c_n)
    m = mg.transpose(2, 0, 3, 5, 1, 4).reshape(sg, t * SUPER * c_n, t * c_n)

    def state_in(abr, abi, order):
        wr = abr[order].reshape(t, sg, SUPER, p_n, c_n)
        wi = abi[order].reshape(t, sg, SUPER, p_n, c_n)
        w = jnp.stack([wr, wi], axis=0)
        return w.transpose(2, 1, 3, 5, 0, 4).reshape(sg, t * SUPER * c_n, 2 * p_n)

    def state_out(car, cai, order):
        vr = car[order].reshape(t, sg, SUPER, c_n, p_n)
        vi = -cai[order].reshape(t, sg, SUPER, c_n, p_n)
        v = jnp.stack([vr, vi], axis=0)
        return v.transpose(2, 0, 3, 5, 1, 4).reshape(sg, 2 * SUPER * p_n, t * c_n)

    wf = state_in(mats[0][2], mats[0][3], (t - 1) - s_idx)
    wb = state_in(mats[1][2], mats[1][3], s_idx)
    vf = state_out(mats[0][5], mats[0][6], s_idx + 1)
    vb = state_out(mats[1][5], mats[1][6], t - s_idx)

    def decay_table(dr, powers):
        lre, lim = lam_re[dr], lam_im[dr]
        dt = jnp.exp(log_dt[dr])[:, None]
        kk = (t * powers).astype(F32)[:, None, None]
        mag = jnp.exp(kk * (lre * dt))
        tab = jnp.stack([mag * jnp.cos(kk * (lim * dt)), mag * jnp.sin(kk * (lim * dt))], axis=0)
        return tab.reshape(2, -1, sg, SUPER * p_n).transpose(2, 0, 1, 3)

    j = jnp.arange(seg)
    bf = lambda a: a.astype(BF16)
    return (bf(m), bf(wf), bf(wb), bf(vf), bf(vb), decay_table(0, j + 1), decay_table(1, seg - j))


def _s5_segment(n):
    return 8 * (-(-n // 64))


def _cmul_add(ar, ai, xr, xi, br, bi):
    return ar * xr - ai * xi + br, ar * xi + ai * xr + bi


def _group_replicator(outer, inner):
    col = np.arange(outer * SUPER * inner)
    src = (col // (SUPER * inner)) * inner + col % inner
    return jnp.asarray(np.arange(outer * inner)[:, None] == src[None, :], dtype=BF16)


def _expand_groups(xc, rep, row_block, col_block):
    dense = jnp.dot(xc, rep, preferred_element_type=F32)
    r = lax.broadcasted_iota(jnp.int32, dense.shape, 0)
    c = lax.broadcasted_iota(jnp.int32, dense.shape, 1)
    own = (r // row_block) % SUPER == (c // col_block) % SUPER
    return jnp.where(own, dense, 0.0).astype(BF16)


def _s5_kernel(u_ref, mc_ref, wfc_ref, wbc_ref, vfc_ref, vbc_ref, rep_out_ref, rep_state_ref, pf_ref, pb_ref,
               y_ref, m_ref, wf_ref, wb_ref, vf_ref, vb_ref, sf_ref, sb_ref, *, nx, nc):
    @pl.when(pl.program_id(1) == 0)
    def _build_matrices():
        rep_out, rep_state = rep_out_ref[...], rep_state_ref[...]
        m_ref[...] = _expand_groups(mc_ref[0], rep_out, S5_GROUP, S5_GROUP)
        wf_ref[...] = _expand_groups(wfc_ref[0], rep_state, S5_GROUP, S5_STATE)
        wb_ref[...] = _expand_groups(wbc_ref[0], rep_state, S5_GROUP, S5_STATE)
        vf_ref[...] = _expand_groups(vfc_ref[0], rep_out, S5_STATE, S5_GROUP)
        vb_ref[...] = _expand_groups(vbc_ref[0], rep_out, S5_STATE, S5_GROUP)

    n = nx + nc
    tiles, npad = sf_ref.shape[1:3]
    seg = npad // 8
    pad = npad - n
    half = tiles * LANES

    def load(ref, rows):
        return tuple(jnp.concatenate([ref[ri, c, rows, :] for c in range(tiles)], axis=-1) for ri in range(2))

    def store(ref, rows, re, im):
        for c in range(tiles):
            ref[0, c, rows, :] = re[:, c * LANES:(c + 1) * LANES]
            ref[1, c, rows, :] = im[:, c * LANES:(c + 1) * LANES]

    z = jnp.concatenate([u_ref[0, 0, pl.ds(t, n, stride=CHUNK), :] for t in range(CHUNK)], axis=-1).astype(BF16)
    zl, zc = z[:nx], z[nx:]
    zpad = [jnp.zeros((pad, z.shape[1]), BF16)] if pad else []
    s = jnp.dot(jnp.concatenate([zc, zl] + zpad, axis=0), wf_ref[...], preferred_element_type=F32)
    store(sf_ref, slice(None), s[:, :half], s[:, half:])
    s = jnp.dot(jnp.concatenate(zpad + [zl, zc], axis=0), wb_ref[...], preferred_element_type=F32)
    store(sb_ref, slice(None), s[:, :half], s[:, half:])

    afr, afi = pf_ref[0, 0, 0:1, :], pf_ref[0, 1, 0:1, :]
    abr, abi = pb_ref[0, 0, seg - 1:seg, :], pb_ref[0, 1, seg - 1:seg, :]

    def step(i, carry):
        hr, hi, gr, gi = carry
        rows = pl.ds(i, 8, stride=seg)
        hr, hi = _cmul_add(afr, afi, hr, hi, *load(sf_ref, rows))
        store(sf_ref, rows, hr, hi)
        rows = pl.ds(seg - 1 - i, 8, stride=seg)
        gr, gi = _cmul_add(abr, abi, gr, gi, *load(sb_ref, rows))
        store(sb_ref, rows, gr, gi)
        return hr, hi, gr, gi

    zero = jnp.zeros((8, half), F32)
    hr, hi, gr, gi = lax.fori_loop(0, seg, step, (zero, zero, zero, zero), unroll=True)

    tfr, tfi = pf_ref[0, 0], pf_ref[0, 1]
    tbr, tbi = pb_ref[0, 0], pb_ref[0, 1]
    cr = ci = None
    for s in range(1, 8):
        er, ei = hr[s - 1:s], hi[s - 1:s]
        if cr is None:
            cr, ci = er, ei
        else:
            cr, ci = _cmul_add(tfr[seg - 1:seg], tfi[seg - 1:seg], cr, ci, er, ei)
        rows = slice(s * seg, (s + 1) * seg)
        store(sf_ref, rows, *_cmul_add(tfr, tfi, cr, ci, *load(sf_ref, rows)))
    cr = ci = None
    for s in range(6, -1, -1):
        er, ei = gr[s + 1:s + 2], gi[s + 1:s + 2]
        if cr is None:
            cr, ci = er, ei
        else:
            cr, ci = _cmul_add(tbr[0:1], tbi[0:1], cr, ci, er, ei)
        rows = slice(s * seg, (s + 1) * seg)
        store(sb_ref, rows, *_cmul_add(tbr, tbi, cr, ci, *load(sb_ref, rows)))

    def entering(ref, shift, first):
        cols = [pltpu.roll(ref[ri, c], shift, 0)[first:first + nx] for ri in range(2) for c in range(tiles)]
        return jnp.concatenate(cols, axis=-1).astype(BF16)

    y = (jnp.dot(zl, m_ref[...], preferred_element_type=F32)
         + jnp.dot(entering(sf_ref, 1, nc), vf_ref[...], preferred_element_type=F32)
         + jnp.dot(entering(sb_ref, npad - 1, pad), vb_ref[...], preferred_element_type=F32))
    for t in range(CHUNK):
        y_ref[0, 0, pl.ds(t, nx, stride=CHUNK), :] = y[:, t * LANES:(t + 1) * LANES]


def _s5_call(u, mats, lc):
    b, sg, la, _ = u.shape
    t = CHUNK
    n, nc = la // t, lc // t
    nx = n - nc
    npad = 8 * _s5_segment(n)
    width = t * LANES
    tiles = width // (2 * LANES)
    mc, wfc, wbc, vfc, vbc, pf, pb = mats
    rep_out = _group_replicator(t, S5_GROUP)
    rep_state = _group_replicator(2, S5_STATE)
    wspec = lambda a: pl.BlockSpec((1,) + a.shape[1:], lambda g, bi: (g,) + (0,) * (a.ndim - 1))
    dense = pltpu.VMEM((width, width), BF16)
    states = pltpu.VMEM((2, tiles, npad, LANES), F32)
    return pl.pallas_call(
        functools.partial(_s5_kernel, nx=nx, nc=nc),
        out_shape=jax.ShapeDtypeStruct((b, sg, nx * t, LANES), F32),
        grid=(sg, b),
        in_specs=[pl.BlockSpec((1, 1, la, LANES), lambda g, bi: (bi, g, 0, 0))]
                 + [wspec(a) for a in (mc, wfc, wbc, vfc, vbc)]
                 + [_resident(rep_out.shape), _resident(rep_state.shape), wspec(pf), wspec(pb)],
        out_specs=pl.BlockSpec((1, 1, nx * t, LANES), lambda g, bi: (bi, g, 0, 0)),
        scratch_shapes=[dense] * 5 + [states] * 2,
        compiler_params=_cparams(("arbitrary", "arbitrary")),
        name="s5",
    )(u, mc, wfc, wbc, vfc, vbc, rep_out, rep_state, pf, pb)


def _gelu_tanh(y):
    return 0.5 * y * (1.0 + jnp.tanh(math.sqrt(2.0 / math.pi) * (y + 0.044715 * (y * y * y))))


def _merge_ffn_kernel(x_ref, mod_ref, y_ref, ox_ref, ga_ref, gs_ref, wglu_ref, bglu_ref, wpa_ref, wps_ref,
                      wout_ref, g2_ref, w13_ref, w2_ref, o_ref, *, base):
    mod = mod_ref[0]
    rows = x_ref.shape[1] // FFN_SPLIT
    for r in range(FFN_SPLIT):
        rs = slice(r * rows, (r + 1) * rows)
        ge = _gelu_tanh(jnp.concatenate([y_ref[0, g, rs, :] for g in range(y_ref.shape[1])], axis=-1))
        z = jnp.dot(ge.astype(BF16), wglu_ref[...], preferred_element_type=F32) + bglu_ref[...]
        sx = (ge * jax.nn.sigmoid(z)).astype(BF16)
        mix = (ga_ref[0, rs, :].astype(F32) * jnp.dot(ox_ref[0, rs, :], wpa_ref[...], preferred_element_type=F32)
               + gs_ref[0, rs, :].astype(F32) * jnp.dot(sx, wps_ref[...], preferred_element_type=F32))
        out = jnp.dot(mix.astype(BF16), wout_ref[...], preferred_element_type=F32)
        x2 = x_ref[0, rs, :] + mod[base + 2:base + 3, :] * out
        o_ref[0, rs, :] = _ffn_rows(x2, g2_ref[...], mod, base + 3, w13_ref, w2_ref)


def _merge_ffn_call(x, mod, y, ox, ga, gs, w_glu, b_glu, w_pa, w_ps, w_out, g2, w13, w2, base, tm):
    b, l, d = x.shape
    sg = y.shape[1]
    tok = lambda width: pl.BlockSpec((1, tm, width), lambda bi, i: (bi, i, 0))
    weights = (w_glu, b_glu, w_pa, w_ps, w_out, g2, w13, w2)
    return pl.pallas_call(
        functools.partial(_merge_ffn_kernel, base=base),
        out_shape=jax.ShapeDtypeStruct(x.shape, F32),
        grid=(b, l // tm),
        in_specs=[tok(d), pl.BlockSpec((1, N_MOD, d), lambda bi, i: (bi, 0, 0)),
                  pl.BlockSpec((1, sg, tm, LANES), lambda bi, i: (bi, 0, i, 0)), tok(d), tok(d), tok(d)]
                 + [_resident(w.shape) for w in weights],
        out_specs=tok(d),
        compiler_params=_cparams(("parallel", "parallel")),
        name="merge_ffn2",
    )(x, mod, y, ox, ga, gs, *weights)


def _rope_tables(l):
    n_freq = HEAD_DIM // 4
    inv = ROPE_BASE ** (-jnp.arange(n_freq, dtype=F32) / n_freq)
    pos = jnp.arange(l, dtype=jnp.int32)
    row = (pos // GRID_W).astype(F32)
    col = (pos % GRID_W).astype(F32)
    ang = jnp.concatenate([row[:, None] * inv, col[:, None] * inv], axis=-1)
    cos, sin = jnp.cos(ang), jnp.sin(ang)
    return jnp.concatenate([cos] * 4, axis=-1), jnp.concatenate([-sin, sin, -sin, sin], axis=-1)


def _group_ones():
    grp = np.arange(MXU_WIDTH) // HEAD_DIM
    return jnp.asarray(grp[:, None] == grp[None, :], dtype=BF16)


def kernel(x, c, ctx, c_ctx, w_mod, b_mod, norm_ffn1, w13_ffn1, w2_ffn1, norm_mix, w_in, q_norm, k_norm, lam_q1, lam_k1, lam_q2, lam_k2, subln, s5_lam_re, s5_lam_im, s5_log_dt, s5_b_re, s5_b_im, s5_c_re, s5_c_im, s5_d, w_glu, b_glu, w_pa, w_ps, w_out, norm_ffn2, w13_ffn2, w2_ffn2):
    assert w_mod.shape[0] == 1, "single-layer block"
    b, l, d = x.shape
    lc = ctx.shape[1]
    lam_init = 0.8 - 0.6 * math.exp(-0.3 * 0)
    bf = lambda a: a.astype(BF16)
    row = lambda a: a.reshape(1, -1)

    rows = b + 1
    pad = (-rows) % 8
    cc = jnp.concatenate([c, c_ctx[None, :], jnp.zeros((pad, d), F32)], axis=0)
    mod = _mod_call(cc, w_mod[0], row(b_mod[0]))
    modx = mod[:b].reshape(b, N_MOD, d)
    modc = mod[b:b + 1].reshape(1, N_MOD, d)

    w13_1, w2_1 = bf(w13_ffn1[0]), bf(w2_ffn1[0])
    x1 = _ffn_call(x, modx, row(norm_ffn1[0]), w13_1, w2_1, 0, 512, "ffn1_x")

    ones = _group_ones()
    cos, sin = _rope_tables(l)
    qg = row(jnp.tile(q_norm[0], d // HEAD_DIM)) * (LOG2E / math.sqrt(HEAD_DIM))
    kg = row(jnp.tile(k_norm[0], d // HEAD_DIM))
    w_inb = bf(w_in[0])
    s5w = s5_d.shape[-1]
    q, k, vt, u, ga, gs = _inproj_call(x1, lc, modx, row(norm_mix[0]), w_inb, qg, kg, ones,
                                       cos, sin, s5w, 3, 512)
    k, vt, u = _ctx_call(ctx, modc, row(norm_ffn1[0]), w13_1, w2_1, row(norm_mix[0]),
                         w_inb[:, d:3 * d + s5w], kg, ones, k, vt, u, 3)

    lamv = jnp.stack([lam_q1[0], lam_k1[0], lam_q2[0], lam_k2[0]], axis=0)
    bound = (LOG2E * math.sqrt(HEAD_DIM)) * jnp.max(jnp.abs(q_norm[0])) * jnp.max(jnp.abs(k_norm[0]))
    attn = lambda bounded, tq, tk: functools.partial(
        _attn_call, lamv=lamv, subln=subln[0].reshape(-1, 1), q=q, k=k, vt=vt, lc=lc, lam_init=lam_init,
        tq=tq, tk=tk, bounded=bounded)
    ox = lax.cond(2.0 * bound <= MAX_EXP2_SPAN, attn(True, 1024, None), attn(False, 256, 1024),
                  bound.reshape(1, 1))

    mats = _s5_matrices(s5_lam_re[0], s5_lam_im[0], s5_log_dt[0], s5_b_re[0], s5_b_im[0],
                        s5_c_re[0], s5_c_im[0], s5_d[0], _s5_segment((l + lc) // CHUNK))
    y = _s5_call(u, mats, lc)

    return _merge_ffn_call(x1, modx, y, ox, ga, gs, bf(w_glu[0]), row(b_glu[0]), bf(w_pa[0]), bf(w_ps[0]),
                           bf(w_out[0]), row(norm_ffn2[0]), bf(w13_ffn2[0]), bf(w2_ffn2[0]), 3, 512)
```

```python
import functools
import math

import jax
import jax.numpy as jnp
import numpy as np
from jax import lax
from jax.experimental import pallas as pl
from jax.experimental.pallas import tpu as pltpu

F32 = jnp.float32
BF16 = jnp.bfloat16

EPS = 1e-6
ROPE_BASE = 10000.0
GRID_W = 64
N_HEADS = 8
HEAD_DIM = 64
V_DIM = 2 * HEAD_DIM
S5_GROUP = 16
S5_STATE = 64
N_MOD = 9
LANES = 128
MXU_WIDTH = 256
CHUNK = 8
SUPER = LANES // S5_GROUP
VMEM_LIMIT = 56 * 1024 * 1024
LOG2E = math.log2(math.e)
FFN_ROWS = 256
PROJ_ROWS = 256
MAX_EXP2_SPAN = 100.0


def _cparams(sem):
    return pltpu.CompilerParams(dimension_semantics=sem, vmem_limit_bytes=VMEM_LIMIT)


def _resident(shape):
    nd = len(shape)
    return pl.BlockSpec(shape, lambda *_: (0,) * nd, pipeline_mode=pl.Buffered(1))


def _silu(a):
    return a * jax.nn.sigmoid(a)


def _modulated(x, g, mod, base):
    ms = jnp.mean(x * x, axis=-1, keepdims=True)
    xn = x * lax.rsqrt(ms + EPS) * g
    return xn * (1.0 + mod[base + 1:base + 2, :]) + mod[base:base + 1, :]


def _mod_kernel(c_ref, w_ref, b_ref, o_ref):
    a = _silu(c_ref[...]).astype(BF16)
    o_ref[...] = jnp.dot(a, w_ref[...].astype(BF16), preferred_element_type=F32) + b_ref[...]


def _mod_call(cc, w_mod, b_mod):
    rows, d = cc.shape
    n = w_mod.shape[1]
    tn = d
    return pl.pallas_call(
        _mod_kernel,
        out_shape=jax.ShapeDtypeStruct((rows, n), F32),
        grid=(n // tn,),
        in_specs=[pl.BlockSpec((rows, d), lambda j: (0, 0)),
                  pl.BlockSpec((d, tn), lambda j: (0, j)),
                  pl.BlockSpec((1, tn), lambda j: (0, j))],
        out_specs=pl.BlockSpec((rows, tn), lambda j: (0, j)),
        compiler_params=_cparams(("arbitrary",)),
        name="mod",
    )(cc, w_mod, b_mod)


def _ffn_rows(x, g, mod, base, w13_ref, w2_ref):
    dff = w2_ref.shape[0]
    xm = _modulated(x, g, mod, base).astype(BF16)
    h = jnp.dot(xm, w13_ref[...], preferred_element_type=F32)
    act = (_silu(h[:, :dff]) * h[:, dff:]).astype(BF16)
    return x + (0.5 * mod[base + 2:base + 3, :]) * jnp.dot(act, w2_ref[...], preferred_element_type=F32)


def _ffn_kernel(x_ref, mod_ref, g_ref, w13_ref, w2_ref, o_ref, *, base):
    for r in range(x_ref.shape[1] // FFN_ROWS):
        rs = slice(r * FFN_ROWS, (r + 1) * FFN_ROWS)
        o_ref[0, rs, :] = _ffn_rows(x_ref[0, rs, :], g_ref[...], mod_ref[0], base, w13_ref, w2_ref)


def _ffn_call(x, mod, g, w13, w2, base, tm, name):
    b, l, d = x.shape
    return pl.pallas_call(
        functools.partial(_ffn_kernel, base=base),
        out_shape=jax.ShapeDtypeStruct(x.shape, F32),
        grid=(b, l // tm),
        in_specs=[pl.BlockSpec((1, tm, d), lambda bi, i: (bi, i, 0)),
                  pl.BlockSpec((1, N_MOD, d), lambda bi, i: (bi, 0, 0)),
                  _resident(g.shape), _resident(w13.shape), _resident(w2.shape)],
        out_specs=pl.BlockSpec((1, tm, d), lambda bi, i: (bi, i, 0)),
        compiler_params=_cparams(("parallel", "parallel")),
        name=name,
    )(x, mod, g, w13, w2)


def _qk_norm(t, gain, ones_ref):
    sq = (t * t).astype(BF16)
    w = ones_ref.shape[0]
    ss = jnp.concatenate([jnp.dot(sq[:, j:j + w], ones_ref[...], preferred_element_type=F32)
                          for j in range(0, t.shape[-1], w)], axis=-1)
    return t * lax.rsqrt(ss * (1.0 / HEAD_DIM) + EPS) * gain


def _rope(t, cos, sin_signed):
    n = t.shape[-1]
    half = HEAD_DIM // 2
    lane = lax.broadcasted_iota(jnp.int32, t.shape, 1)
    swapped = jnp.where(lane % HEAD_DIM < half, pltpu.roll(t, n - half, 1), pltpu.roll(t, half, 1))
    reps = n // cos.shape[-1]
    cosf = jnp.concatenate([cos] * reps, axis=-1)
    sinf = jnp.concatenate([sin_signed] * reps, axis=-1)
    return t * cosf + swapped * sinf


def _store_supergroups(u_ref, rs, u):
    for g in range(u_ref.shape[1]):
        u_ref[0, g, rs, :] = u[:, g * LANES:(g + 1) * LANES]


def _inproj_kernel(x_ref, mod_ref, g_ref, w_ref, qg_ref, kg_ref, ones_ref, cos_ref, sin_ref,
                   q_ref, k_ref, vt_ref, u_ref, ga_ref, gs_ref, *, base):
    d = x_ref.shape[-1]
    qc, kc, vc = d, 2 * d, 3 * d
    uc = vc + u_ref.shape[1] * LANES
    for r in range(x_ref.shape[1] // PROJ_ROWS):
        rs = slice(r * PROJ_ROWS, (r + 1) * PROJ_ROWS)
        xm = _modulated(x_ref[0, rs, :], g_ref[...], mod_ref[0], base).astype(BF16)
        proj = lambda lo, hi: jnp.dot(xm, w_ref[:, lo:hi], preferred_element_type=F32)
        cos = cos_ref[rs, :]
        sin = sin_ref[rs, :]
        tq = proj(0, qc)
        tk = proj(qc, kc)
        q_ref[0, rs, :] = _rope(_qk_norm(tq, qg_ref[...], ones_ref), cos, sin).astype(BF16)
        vt_ref[0, :, rs] = proj(kc, vc).astype(BF16).T
        k_ref[0, rs, :] = _rope(_qk_norm(tk, kg_ref[...], ones_ref), cos, sin).astype(BF16)
        _store_supergroups(u_ref, rs, proj(vc, uc))
        ga_ref[0, rs, :] = jax.nn.sigmoid(proj(uc, uc + d)).astype(BF16)
        gs_ref[0, rs, :] = jax.nn.sigmoid(proj(uc + d, uc + 2 * d)).astype(BF16)


def _inproj_call(x, lc, modx, g, w_in, qg, kg, ones, cos, sin, s5w, base, tm):
    b, l, d = x.shape
    la = l + lc
    tok = lambda bi, i: (bi, i, 0)
    out_shape = (jax.ShapeDtypeStruct((b, l, d), BF16),
                 jax.ShapeDtypeStruct((b, la, d), BF16),
                 jax.ShapeDtypeStruct((b, d, la), BF16),
                 jax.ShapeDtypeStruct((b, s5w // LANES, la, LANES), F32),
                 jax.ShapeDtypeStruct((b, l, d), BF16),
                 jax.ShapeDtypeStruct((b, l, d), BF16))
    return pl.pallas_call(
        functools.partial(_inproj_kernel, base=base),
        out_shape=out_shape,
        grid=(b, l // tm),
        in_specs=[pl.BlockSpec((1, tm, d), tok),
                  pl.BlockSpec((1, N_MOD, d), lambda bi, i: (bi, 0, 0)),
                  _resident(g.shape), _resident(w_in.shape), _resident(qg.shape), _resident(kg.shape),
                  _resident(ones.shape),
                  pl.BlockSpec((tm, LANES), lambda bi, i: (i, 0)),
                  pl.BlockSpec((tm, LANES), lambda bi, i: (i, 0))],
        out_specs=(pl.BlockSpec((1, tm, d), tok),
                   pl.BlockSpec((1, tm, d), tok),
                   pl.BlockSpec((1, d, tm), lambda bi, i: (bi, 0, i)),
                   pl.BlockSpec((1, s5w // LANES, tm, LANES), lambda bi, i: (bi, 0, i, 0)),
                   pl.BlockSpec((1, tm, d), tok),
                   pl.BlockSpec((1, tm, d), tok)),
        compiler_params=_cparams(("parallel", "parallel")),
        name="inproj",
    )(x, modx, g, w_in, qg, kg, ones, cos, sin)


def _ctx_kernel(ctx_ref, mod_ref, g1_ref, w13_ref, w2_ref, gm_ref, w_ref, kg_ref, ones_ref,
                k_in, vt_in, u_in, k_ref, vt_ref, u_ref, *, base):
    del k_in, vt_in, u_in
    d = ctx_ref.shape[-1]
    mod = mod_ref[0]
    x1 = _ffn_rows(ctx_ref[0], g1_ref[...], mod, 0, w13_ref, w2_ref)
    xm = _modulated(x1, gm_ref[...], mod, base).astype(BF16)
    p = jnp.dot(xm, w_ref[...], preferred_element_type=F32)
    k_ref[0] = _qk_norm(p[:, :d], kg_ref[...], ones_ref).astype(BF16)
    vt_ref[0] = p[:, d:2 * d].astype(BF16).T
    _store_supergroups(u_ref, slice(None), p[:, 2 * d:])


def _ctx_call(ctx, modc, g1, w13, w2, gm, w_kvu, kg, ones, k, vt, u, base):
    b, lc, d = ctx.shape
    sg = u.shape[1]
    last = k.shape[1] // lc - 1
    anyspec = pl.BlockSpec(memory_space=pl.ANY)
    return pl.pallas_call(
        functools.partial(_ctx_kernel, base=base),
        out_shape=(jax.ShapeDtypeStruct(k.shape, k.dtype), jax.ShapeDtypeStruct(vt.shape, vt.dtype),
                   jax.ShapeDtypeStruct(u.shape, u.dtype)),
        grid=(b,),
        in_specs=[pl.BlockSpec((1, lc, d), lambda bi: (bi, 0, 0)),
                  pl.BlockSpec((1, N_MOD, d), lambda bi: (0, 0, 0)),
                  _resident(g1.shape), _resident(w13.shape), _resident(w2.shape), _resident(gm.shape),
                  _resident(w_kvu.shape), _resident(kg.shape), _resident(ones.shape),
                  anyspec, anyspec, anyspec],
        out_specs=(pl.BlockSpec((1, lc, d), lambda bi: (bi, last, 0)),
                   pl.BlockSpec((1, d, lc), lambda bi: (bi, 0, last)),
                   pl.BlockSpec((1, sg, lc, LANES), lambda bi: (bi, 0, last, 0))),
        input_output_aliases={9: 0, 10: 1, 11: 2},
        compiler_params=_cparams(("parallel",)),
        name="ctx",
    )(ctx, modc, g1, w13, w2, gm, w_kvu, kg, ones, k, vt, u)


def _attn_kernel(bound_ref, lam_ref, subln_ref, q_ref, k_ref, vt_ref, o_ref, *, lc, tk, lam_init, bounded):
    q = q_ref[0]
    la = k_ref.shape[1]
    lane = lax.broadcasted_iota(jnp.int32, q.shape, 1)
    zero = jnp.zeros_like(q)
    q_comp = (jnp.where(lane < HEAD_DIM, q, zero), jnp.where(lane >= HEAD_DIM, q, zero))
    lv = lam_ref[...]
    lam = (jnp.exp(jnp.sum(lv[0:1] * lv[1:2], axis=-1, keepdims=True))
           - jnp.exp(jnp.sum(lv[2:3] * lv[3:4], axis=-1, keepdims=True)) + lam_init)
    blocks = [(0, la)] if bounded else [(st, tk) for st in range(0, la - lc, tk)] + [(la - lc, lc)]
    nt_dims = (((1,), (1,)), ((), ()))
    heads = []
    for qz in q_comp:
        m = l = acc = None
        for st, sz in blocks:
            kb = k_ref[0, st:st + sz, :]
            vb = vt_ref[0, :, st:st + sz]
            s = lax.dot_general(kb, qz, nt_dims, preferred_element_type=F32)
            if bounded:
                p = jnp.exp2(s - bound_ref[...])
                l = jnp.sum(p, axis=0, keepdims=True)
                acc = jnp.dot(vb, p.astype(BF16), preferred_element_type=F32)
                continue
            bm = jnp.max(s, axis=0, keepdims=True)
            if m is None:
                m_new = bm
                p = jnp.exp2(s - m_new)
                l = jnp.sum(p, axis=0, keepdims=True)
                acc = jnp.dot(vb, p.astype(BF16), preferred_element_type=F32)
            else:
                m_new = jnp.maximum(m, bm)
                alpha = jnp.exp2(m - m_new)
                p = jnp.exp2(s - m_new)
                l = alpha * l + jnp.sum(p, axis=0, keepdims=True)
                acc = alpha * acc + jnp.dot(vb, p.astype(BF16), preferred_element_type=F32)
            m = m_new
        heads.append(acc / l)
    o = heads[0] - lam * heads[1]
    ms = jnp.mean(o * o, axis=0, keepdims=True)
    on = o * lax.rsqrt(ms + EPS) * (subln_ref[...] * (1.0 - lam_init))
    o_ref[0] = on.T.astype(BF16)


def _attn_call(bound, lamv, subln, q, k, vt, lc, lam_init, tq, tk, bounded):
    b, l, d = q.shape
    la = k.shape[1]
    return pl.pallas_call(
        functools.partial(_attn_kernel, lc=lc, tk=tk, lam_init=lam_init, bounded=bounded),
        out_shape=jax.ShapeDtypeStruct((b, l, d), BF16),
        grid=(b, N_HEADS, l // tq),
        in_specs=[pl.BlockSpec(bound.shape, lambda bi, h, i: (0, 0)),
                  pl.BlockSpec(lamv.shape, lambda bi, h, i: (0, 0)),
                  pl.BlockSpec(subln.shape, lambda bi, h, i: (0, 0)),
                  pl.BlockSpec((1, tq, V_DIM), lambda bi, h, i: (bi, i, h)),
                  pl.BlockSpec((1, la, V_DIM), lambda bi, h, i: (bi, 0, h)),
                  pl.BlockSpec((1, V_DIM, la), lambda bi, h, i: (bi, h, 0))],
        out_specs=pl.BlockSpec((1, tq, V_DIM), lambda bi, h, i: (bi, i, h)),
        compiler_params=_cparams(("parallel", "parallel", "arbitrary")),
        name="attn_bounded" if bounded else "attn_online",
    )(bound, lamv, subln, q, k, vt)


def _s5_matrices(lam_re, lam_im, log_dt, b_re, b_im, c_re, c_im, d_skip, seg):
    t = CHUNK
    g_all, p_n = lam_re.shape[1:]
    c_n = b_re.shape[-1]
    sg = g_all // SUPER
    ks = jnp.arange(t + 1, dtype=F32)[:, None, None]
    mats = []
    for dr in range(2):
        lre, lim = lam_re[dr], lam_im[dr]
        dt = jnp.exp(log_dt[dr])[:, None]
        mag = jnp.exp(ks * (lre * dt))
        akr = mag * jnp.cos(ks * (lim * dt))
        aki = mag * jnp.sin(ks * (lim * dt))
        ar, ai = akr[1], aki[1]
        den = lre * lre + lim * lim
        nr = ar - 1.0
        fr = (nr * lre + ai * lim) / den
        fi = (ai * lre - nr * lim) / den
        bbr = fr[..., None] * b_re[dr] - fi[..., None] * b_im[dr]
        bbi = fr[..., None] * b_im[dr] + fi[..., None] * b_re[dr]
        abr = akr[..., None] * bbr - aki[..., None] * bbi
        abi = akr[..., None] * bbi + aki[..., None] * bbr
        kern = (jnp.einsum('gdp,kgpc->kgdc', c_re[dr], abr)
                - jnp.einsum('gdp,kgpc->kgdc', c_im[dr], abi))
        car = c_re[dr][None] * akr[:, :, None, :] - c_im[dr][None] * aki[:, :, None, :]
        cai = c_re[dr][None] * aki[:, :, None, :] + c_im[dr][None] * akr[:, :, None, :]
        mats.append((akr, aki, abr, abi, kern, car, cai))

    s_idx = jnp.arange(t)
    lag = s_idx[None, :] - s_idx[:, None]
    kf, kb = mats[0][4], mats[1][4]
    d_diag = jnp.eye(c_n, dtype=F32)[None] * d_skip.reshape(g_all, 1, c_n)
    kf_l = kf[jnp.clip(lag, 0, t)]
    kb_l = kb[jnp.clip(-lag, 0, t)]
    fwd_on = (lag >= 0)[..., None, None, None]
    bwd_on = (lag <= 0)[..., None, None, None]
    same = (lag == 0)[..., None, None, None]
    mg = jnp.where(fwd_on, kf_l, 0.0) + jnp.where(bwd_on, kb_l, 0.0) + jnp.where(same, d_diag[None, None], 0.0)
    mg = mg.reshape(t, t, sg, SUPER, c_n, c_n)
    m = mg.transpose(2, 0, 3, 5, 1, 4).reshape(sg, t * SUPER * c_n, t * c_n)

    def state_in(abr, abi, order):
        wr = abr[order].reshape(t, sg, SUPER, p_n, c_n)
        wi = abi[order].reshape(t, sg, SUPER, p_n, c_n)
        w = jnp.stack([wr, wi], axis=0)
        return w.transpose(2, 1, 3, 5, 0, 4).reshape(sg, t * SUPER * c_n, 2 * p_n)

    def state_out(car, cai, order):
        vr = car[order].reshape(t, sg, SUPER, c_n, p_n)
        vi = -cai[order].reshape(t, sg, SUPER, c_n, p_n)
        v = jnp.stack([vr, vi], axis=0)
        return v.transpose(2, 0, 3, 5, 1, 4).reshape(sg, 2 * SUPER * p_n, t * c_n)

    wf = state_in(mats[0][2], mats[0][3], (t - 1) - s_idx)
    wb = state_in(mats[1][2], mats[1][3], s_idx)
    vf = state_out(mats[0][5], mats[0][6], s_idx + 1)
    vb = state_out(mats[1][5], mats[1][6], t - s_idx)

    def decay_table(dr, powers):
        lre, lim = lam_re[dr], lam_im[dr]
        dt = jnp.exp(log_dt[dr])[:, None]
        kk = (t * powers).astype(F32)[:, None, None]
        mag = jnp.exp(kk * (lre * dt))
        tab = jnp.stack([mag * jnp.cos(kk * (lim * dt)), mag * jnp.sin(kk * (lim * dt))], axis=0)
        return tab.reshape(2, -1, sg, SUPER * p_n).transpose(2, 0, 1, 3)

    j = jnp.arange(seg)
    bf = lambda a: a.astype(BF16)
    return (bf(m), bf(wf), bf(wb), bf(vf), bf(vb), decay_table(0, j + 1), decay_table(1, seg - j))


def _s5_segment(n):
    return 8 * (-(-n // 64))


def _cmul_add(ar, ai, xr, xi, br, bi):
    return ar * xr - ai * xi + br, ar * xi + ai * xr + bi


def _group_replicator(outer, inner):
    col = np.arange(outer * SUPER * inner)
    src = (col // (SUPER * inner)) * inner + col % inner
    return jnp.asarray(np.arange(outer * inner)[:, None] == src[None, :], dtype=BF16)


def _expand_groups(xc, rep, row_block, col_block):
    dense = jnp.dot(xc, rep, preferred_element_type=F32)
    r = lax.broadcasted_iota(jnp.int32, dense.shape, 0)
    c = lax.broadcasted_iota(jnp.int32, dense.shape, 1)
    own = (r // row_block) % SUPER == (c // col_block) % SUPER
    return jnp.where(own, dense, 0.0).astype(BF16)


def _s5_kernel(u_ref, mc_ref, wfc_ref, wbc_ref, vfc_ref, vbc_ref, rep_out_ref, rep_state_ref, pf_ref, pb_ref,
               y_ref, m_ref, wf_ref, wb_ref, vf_ref, vb_ref, sf_ref, sb_ref, *, nx, nc):
    @pl.when(pl.program_id(1) == 0)
    def _build_matrices():
        rep_out, rep_state = rep_out_ref[...], rep_state_ref[...]
        m_ref[...] = _expand_groups(mc_ref[0], rep_out, S5_GROUP, S5_GROUP)
        wf_ref[...] = _expand_groups(wfc_ref[0], rep_state, S5_GROUP, S5_STATE)
        wb_ref[...] = _expand_groups(wbc_ref[0], rep_state, S5_GROUP, S5_STATE)
        vf_ref[...] = _expand_groups(vfc_ref[0], rep_out, S5_STATE, S5_GROUP)
        vb_ref[...] = _expand_groups(vbc_ref[0], rep_out, S5_STATE, S5_GROUP)

    n = nx + nc
    tiles, npad = sf_ref.shape[1:3]
    seg = npad // 8
    pad = npad - n
    half = tiles * LANES

    def load(ref, rows):
        return tuple(jnp.concatenate([ref[ri, c, rows, :] for c in range(tiles)], axis=-1) for ri in range(2))

    def store(ref, rows, re, im):
        for c in range(tiles):
            ref[0, c, rows, :] = re[:, c * LANES:(c + 1) * LANES]
            ref[1, c, rows, :] = im[:, c * LANES:(c + 1) * LANES]

    z = jnp.concatenate([u_ref[0, 0, pl.ds(t, n, stride=CHUNK), :] for t in range(CHUNK)], axis=-1).astype(BF16)
    zl, zc = z[:nx], z[nx:]
    zpad = [jnp.zeros((pad, z.shape[1]), BF16)] if pad else []
    s = jnp.dot(jnp.concatenate([zc, zl] + zpad, axis=0), wf_ref[...], preferred_element_type=F32)
    store(sf_ref, slice(None), s[:, :half], s[:, half:])
    s = jnp.dot(jnp.concatenate(zpad + [zl, zc], axis=0), wb_ref[...], preferred_element_type=F32)
    store(sb_ref, slice(None), s[:, :half], s[:, half:])

    afr, afi = pf_ref[0, 0, 0:1, :], pf_ref[0, 1, 0:1, :]
    abr, abi = pb_ref[0, 0, seg - 1:seg, :], pb_ref[0, 1, seg - 1:seg, :]

    def step(i, carry):
        hr, hi, gr, gi = carry
        rows = pl.ds(i, 8, stride=seg)
        hr, hi = _cmul_add(afr, afi, hr, hi, *load(sf_ref, rows))
        store(sf_ref, rows, hr, hi)
        rows = pl.ds(seg - 1 - i, 8, stride=seg)
        gr, gi = _cmul_add(abr, abi, gr, gi, *load(sb_ref, rows))
        store(sb_ref, rows, gr, gi)
        return hr, hi, gr, gi

    zero = jnp.zeros((8, half), F32)
    hr, hi, gr, gi = lax.fori_loop(0, seg, step, (zero, zero, zero, zero), unroll=True)

    tfr, tfi = pf_ref[0, 0], pf_ref[0, 1]
    tbr, tbi = pb_ref[0, 0], pb_ref[0, 1]
    cr = ci = None
    for s in range(1, 8):
        er, ei = hr[s - 1:s], hi[s - 1:s]
        if cr is None:
            cr, ci = er, ei
        else:
            cr, ci = _cmul_add(tfr[seg - 1:seg], tfi[seg - 1:seg], cr, ci, er, ei)
        rows = slice(s * seg, (s + 1) * seg)
        store(sf_ref, rows, *_cmul_add(tfr, tfi, cr, ci, *load(sf_ref, rows)))
    cr = ci = None
    for s in range(6, -1, -1):
        er, ei = gr[s + 1:s + 2], gi[s + 1:s + 2]
        if cr is None:
            cr, ci = er, ei
        else:
            cr, ci = _cmul_add(tbr[0:1], tbi[0:1], cr, ci, er, ei)
        rows = slice(s * seg, (s + 1) * seg)
        store(sb_ref, rows, *_cmul_add(tbr, tbi, cr, ci, *load(sb_ref, rows)))

    def entering(ref, shift, first):
        cols = [pltpu.roll(ref[ri, c], shift, 0)[first:first + nx] for ri in range(2) for c in range(tiles)]
        return jnp.concatenate(cols, axis=-1).astype(BF16)

    y = (jnp.dot(zl, m_ref[...], preferred_element_type=F32)
         + jnp.dot(entering(sf_ref, 1, nc), vf_ref[...], preferred_element_type=F32)
         + jnp.dot(entering(sb_ref, npad - 1, pad), vb_ref[...], preferred_element_type=F32))
    for t in range(CHUNK):
        y_ref[0, 0, pl.ds(t, nx, stride=CHUNK), :] = y[:, t * LANES:(t + 1) * LANES]


def _s5_call(u, mats, lc):
    b, sg, la, _ = u.shape
    t = CHUNK
    n, nc = la // t, lc // t
    nx = n - nc
    npad = 8 * _s5_segment(n)
    width = t * LANES
    tiles = width // (2 * LANES)
    mc, wfc, wbc, vfc, vbc, pf, pb = mats
    rep_out = _group_replicator(t, S5_GROUP)
    rep_state = _group_replicator(2, S5_STATE)
    wspec = lambda a: pl.BlockSpec((1,) + a.shape[1:], lambda g, bi: (g,) + (0,) * (a.ndim - 1))
    dense = pltpu.VMEM((width, width), BF16)
    states = pltpu.VMEM((2, tiles, npad, LANES), F32)
    return pl.pallas_call(
        functools.partial(_s5_kernel, nx=nx, nc=nc),
        out_shape=jax.ShapeDtypeStruct((b, sg, nx * t, LANES), F32),
        grid=(sg, b),
        in_specs=[pl.BlockSpec((1, 1, la, LANES), lambda g, bi: (bi, g, 0, 0))]
                 + [wspec(a) for a in (mc, wfc, wbc, vfc, vbc)]
                 + [_resident(rep_out.shape), _resident(rep_state.shape), wspec(pf), wspec(pb)],
        out_specs=pl.BlockSpec((1, 1, nx * t, LANES), lambda g, bi: (bi, g, 0, 0)),
        scratch_shapes=[dense] * 5 + [states] * 2,
        compiler_params=_cparams(("arbitrary", "arbitrary")),
        name="s5",
    )(u, mc, wfc, wbc, vfc, vbc, rep_out, rep_state, pf, pb)


def _gelu_tanh(y):
    return 0.5 * y * (1.0 + jnp.tanh(math.sqrt(2.0 / math.pi) * (y + 0.044715 * (y * y * y))))


def _merge_ffn_kernel(x_ref, mod_ref, y_ref, ox_ref, ga_ref, gs_ref, wglu_ref, bglu_ref, wpa_ref, wps_ref,
                      wout_ref, g2_ref, w13_ref, w2_ref, o_ref, *, base):
    mod = mod_ref[0]
    for r in range(x_ref.shape[1] // FFN_ROWS):
        rs = slice(r * FFN_ROWS, (r + 1) * FFN_ROWS)
        ge =_gelu_tanh(jnp.concatenate([y_ref[0, g, rs, :] for g in range(y_ref.shape[1])], axis=-1))
        z = jnp.dot(ge.astype(BF16), wglu_ref[...], preferred_element_type=F32) + bglu_ref[...]
        sx = (ge * jax.nn.sigmoid(z)).astype(BF16)
        mix = (ga_ref[0, rs, :].astype(F32) * jnp.dot(ox_ref[0, rs, :], wpa_ref[...], preferred_element_type=F32)
               + gs_ref[0, rs, :].astype(F32) * jnp.dot(sx, wps_ref[...], preferred_element_type=F32))
        out = jnp.dot(mix.astype(BF16), wout_ref[...], preferred_element_type=F32)
        x2 = x_ref[0, rs, :] + mod[base + 2:base + 3, :] * out
        o_ref[0, rs, :] = _ffn_rows(x2, g2_ref[...], mod, base + 3, w13_ref, w2_ref)


def _merge_ffn_call(x, mod, y, ox, ga, gs, w_glu, b_glu, w_pa, w_ps, w_out, g2, w13, w2, base, tm):
    b, l, d = x.shape
    sg = y.shape[1]
    tok = lambda width: pl.BlockSpec((1, tm, width), lambda bi, i: (bi, i, 0))
    weights = (w_glu, b_glu, w_pa, w_ps, w_out, g2, w13, w2)
    return pl.pallas_call(
        functools.partial(_merge_ffn_kernel, base=base),
        out_shape=jax.ShapeDtypeStruct(x.shape, F32),
        grid=(b, l // tm),
        in_specs=[tok(d), pl.BlockSpec((1, N_MOD, d), lambda bi, i: (bi, 0, 0)),
                  pl.BlockSpec((1, sg, tm, LANES), lambda bi, i: (bi, 0, i, 0)), tok(d), tok(d), tok(d)]
                 + [_resident(w.shape) for w in weights],
        out_specs=tok(d),
        compiler_params=_cparams(("parallel", "parallel")),
        name="merge_ffn2",
    )(x, mod, y, ox, ga, gs, *weights)


def _rope_tables(l):
    n_freq = HEAD_DIM // 4
    inv = ROPE_BASE ** (-jnp.arange(n_freq, dtype=F32) / n_freq)
    pos = jnp.arange(l, dtype=jnp.int32)
    row = (pos // GRID_W).astype(F32)
    col = (pos % GRID_W).astype(F32)
    ang = jnp.concatenate([row[:, None] * inv, col[:, None] * inv], axis=-1)
    cos, sin = jnp.cos(ang), jnp.sin(ang)
    return jnp.concatenate([cos] * 4, axis=-1), jnp.concatenate([-sin, sin, -sin, sin], axis=-1)


def _group_ones():
    grp = np.arange(MXU_WIDTH) // HEAD_DIM
    return jnp.asarray(grp[:, None] == grp[None, :], dtype=BF16)


def kernel(x, c, ctx, c_ctx, w_mod, b_mod, norm_ffn1, w13_ffn1, w2_ffn1, norm_mix, w_in, q_norm, k_norm, lam_q1, lam_k1, lam_q2, lam_k2, subln, s5_lam_re, s5_lam_im, s5_log_dt, s5_b_re, s5_b_im, s5_c_re, s5_c_im, s5_d, w_glu, b_glu, w_pa, w_ps, w_out, norm_ffn2, w13_ffn2, w2_ffn2):
    assert w_mod.shape[0] == 1, "single-layer block"
    b, l, d = x.shape
    lc = ctx.shape[1]
    lam_init = 0.8 - 0.6 * math.exp(-0.3 * 0)
    bf = lambda a: a.astype(BF16)
    row = lambda a: a.reshape(1, -1)

    rows = b + 1
    pad = (-rows) % 8
    cc = jnp.concatenate([c, c_ctx[None, :], jnp.zeros((pad, d), F32)], axis=0)
    mod = _mod_call(cc, w_mod[0], row(b_mod[0]))
    modx = mod[:b].reshape(b, N_MOD, d)
    modc = mod[b:b + 1].reshape(1, N_MOD, d)

    w13_1, w2_1 = bf(w13_ffn1[0]), bf(w2_ffn1[0])
    x1 = _ffn_call(x, modx, row(norm_ffn1[0]), w13_1, w2_1, 0, 1024, "ffn1_x")

    ones = _group_ones()
    cos, sin = _rope_tables(l)
    qg = row(jnp.tile(q_norm[0], d // HEAD_DIM)) * (LOG2E / math.sqrt(HEAD_DIM))
    kg = row(jnp.tile(k_norm[0], d // HEAD_DIM))
    w_inb = bf(w_in[0])
    s5w = s5_d.shape[-1]
    q, k, vt, u, ga, gs = _inproj_call(x1, lc, modx, row(norm_mix[0]), w_inb, qg, kg, ones,
                                       cos, sin, s5w, 3, 1024)
    k, vt, u = _ctx_call(ctx, modc, row(norm_ffn1[0]), w13_1, w2_1, row(norm_mix[0]),
                         w_inb[:, d:3 * d + s5w], kg, ones, k, vt, u, 3)

    lamv = jnp.stack([lam_q1[0], lam_k1[0], lam_q2[0], lam_k2[0]], axis=0)
    bound = (LOG2E * math.sqrt(HEAD_DIM)) * jnp.max(jnp.abs(q_norm[0])) * jnp.max(jnp.abs(k_norm[0]))
    attn = lambda bounded, tq, tk: functools.partial(
        _attn_call, lamv=lamv, subln=subln[0].reshape(-1, 1), q=q, k=k, vt=vt, lc=lc, lam_init=lam_init,
        tq=tq, tk=tk, bounded=bounded)
    ox = lax.cond(2.0 * bound <= MAX_EXP2_SPAN, attn(True, 1024, None), attn(False, 256, 1024),
                  bound.reshape(1, 1))

    mats = _s5_matrices(s5_lam_re[0], s5_lam_im[0], s5_log_dt[0], s5_b_re[0], s5_b_im[0],
                        s5_c_re[0], s5_c_im[0], s5_d[0], _s5_segment((l + lc) // CHUNK))
    y = _s5_call(u, mats, lc)

    return _merge_ffn_call(x1, modx, y, ox, ga, gs, bf(w_glu[0]), row(b_glu[0]), bf(w_pa[0]), bf(w_ps[0]),
                           bf(w_out[0]), row(norm_ffn2[0]), bf(w13_ffn2[0]), bf(w2_ffn2[0]), 3, 512)
```

```python
import functools
import math

import jax
import jax.numpy as jnp
import numpy as np
from jax import lax
from jax.experimental import pallas as pl
from jax.experimental.pallas import tpu as pltpu

F32 = jnp.float32
BF16 = jnp.bfloat16

EPS = 1e-6
ROPE_BASE = 10000.0
GRID_W = 64
N_HEADS = 8
HEAD_DIM = 64
V_DIM = 2 * HEAD_DIM
S5_GROUP = 16
S5_STATE = 64
N_MOD = 9
LANES = 128
MXU_WIDTH = 256
CHUNK = 8
SUPER = LANES // S5_GROUP
VMEM_LIMIT = 56 * 1024 * 1024
LOG2E = math.log2(math.e)
FFN_ROWS = 256
PROJ_ROWS = 256
MAX_EXP2_SPAN = 100.0


def _cparams(sem):
    return pltpu.CompilerParams(dimension_semantics=sem, vmem_limit_bytes=VMEM_LIMIT)


def _resident(shape):
    nd = len(shape)
    return pl.BlockSpec(shape, lambda *_: (0,) * nd, pipeline_mode=pl.Buffered(1))


def _silu(a):
    return a * jax.nn.sigmoid(a)


def _modulated(x, g, mod, base):
    ms = jnp.mean(x * x, axis=-1, keepdims=True)
    xn = x * lax.rsqrt(ms + EPS) * g
    return xn * (1.0 + mod[base + 1:base + 2, :]) + mod[base:base + 1, :]


def _mod_kernel(c_ref, w_ref, b_ref, o_ref):
    a = _silu(c_ref[...]).astype(BF16)
    o_ref[...] = jnp.dot(a, w_ref[...].astype(BF16), preferred_element_type=F32) + b_ref[...]


def _mod_call(cc, w_mod, b_mod):
    rows, d = cc.shape
    n = w_mod.shape[1]
    tn = d
    return pl.pallas_call(
        _mod_kernel,
        out_shape=jax.ShapeDtypeStruct((rows, n), F32),
        grid=(n // tn,),
        in_specs=[pl.BlockSpec((rows, d), lambda j: (0, 0)),
                  pl.BlockSpec((d, tn), lambda j: (0, j)),
                  pl.BlockSpec((1, tn), lambda j: (0, j))],
        out_specs=pl.BlockSpec((rows, tn), lambda j: (0, j)),
        compiler_params=_cparams(("arbitrary",)),
        name="mod",
    )(cc, w_mod, b_mod)


def _ffn_rows(x, g, mod, base, w13_ref, w2_ref):
    dff = w2_ref.shape[0]
    xm = _modulated(x, g, mod, base).astype(BF16)
    h = jnp.dot(xm, w13_ref[...], preferred_element_type=F32)
    act = (_silu(h[:, :dff]) * h[:, dff:]).astype(BF16)
    return x + (0.5 * mod[base + 2:base + 3, :]) * jnp.dot(act, w2_ref[...], preferred_element_type=F32)


def _ffn_kernel(x_ref, mod_ref, g_ref, w13_ref, w2_ref, o_ref, *, base):
    for r in range(x_ref.shape[1] // FFN_ROWS):
        rs = slice(r * FFN_ROWS, (r + 1) * FFN_ROWS)
        o_ref[0, rs, :] = _ffn_rows(x_ref[0, rs, :], g_ref[...], mod_ref[0], base, w13_ref, w2_ref)


def _ffn_call(x, mod, g, w13, w2, base, tm, name):
    b, l, d = x.shape
    return pl.pallas_call(
        functools.partial(_ffn_kernel, base=base),
        out_shape=jax.ShapeDtypeStruct(x.shape, F32),
        grid=(b, l // tm),
        in_specs=[pl.BlockSpec((1, tm, d), lambda bi, i: (bi, i, 0)),
                  pl.BlockSpec((1, N_MOD, d), lambda bi, i: (bi, 0, 0)),
                  _resident(g.shape), _resident(w13.shape), _resident(w2.shape)],
        out_specs=pl.BlockSpec((1, tm, d), lambda bi, i: (bi, i, 0)),
        compiler_params=_cparams(("parallel", "parallel")),
        name=name,
    )(x, mod, g, w13, w2)


def _qk_norm(t, gain, ones_ref):
    sq = (t * t).astype(BF16)
    w = ones_ref.shape[0]
    ss = jnp.concatenate([jnp.dot(sq[:, j:j + w], ones_ref[...], preferred_element_type=F32)
                          for j in range(0, t.shape[-1], w)], axis=-1)
    return t * lax.rsqrt(ss * (1.0 / HEAD_DIM) + EPS) * gain


def _rope(t, cos, sin_signed):
    n = t.shape[-1]
    half = HEAD_DIM // 2
    lane = lax.broadcasted_iota(jnp.int32, t.shape, 1)
    swapped = jnp.where(lane % HEAD_DIM < half, pltpu.roll(t, n - half, 1), pltpu.roll(t, half, 1))
    reps = n // cos.shape[-1]
    cosf = jnp.concatenate([cos] * reps, axis=-1)
    sinf = jnp.concatenate([sin_signed] * reps, axis=-1)
    return t * cosf + swapped * sinf


def _store_supergroups(u_ref, rs, u):
    for g in range(u_ref.shape[1]):
        u_ref[0, g, rs, :] = u[:, g * LANES:(g + 1) * LANES]


def _inproj_kernel(x_ref, mod_ref, g_ref, w_ref, qg_ref, kg_ref, ones_ref, cos_ref, sin_ref,
                   q_ref, k_ref, vt_ref, u_ref, ga_ref, gs_ref, *, base):
    d = x_ref.shape[-1]
    qc, kc, vc = d, 2 * d, 3 * d
    uc = vc + u_ref.shape[1] * LANES
    for r in range(x_ref.shape[1] // PROJ_ROWS):
        rs = slice(r * PROJ_ROWS, (r + 1) * PROJ_ROWS)
        xm = _modulated(x_ref[0, rs, :], g_ref[...], mod_ref[0], base).astype(BF16)
        proj = lambda lo, hi: jnp.dot(xm, w_ref[:, lo:hi], preferred_element_type=F32)
        cos = cos_ref[rs, :]
        sin = sin_ref[rs, :]
        tq = proj(0, qc)
        tk = proj(qc, kc)
        q_ref[0, rs, :] = _rope(_qk_norm(tq, qg_ref[...], ones_ref), cos, sin).astype(BF16)
        vt_ref[0, :, rs] = proj(kc, vc).astype(BF16).T
        k_ref[0, rs, :] = _rope(_qk_norm(tk, kg_ref[...], ones_ref), cos, sin).astype(BF16)
        _store_supergroups(u_ref, rs, proj(vc, uc))
        ga_ref[0, rs, :] = jax.nn.sigmoid(proj(uc, uc + d)).astype(BF16)
        gs_ref[0, rs, :] = jax.nn.sigmoid(proj(uc + d, uc + 2 * d)).astype(BF16)


def _inproj_call(x, lc, modx, g, w_in, qg, kg, ones, cos, sin, s5w, base, tm):
    b, l, d = x.shape
    la = l + lc
    tok = lambda bi, i: (bi, i, 0)
    out_shape = (jax.ShapeDtypeStruct((b, l, d), BF16),
                 jax.ShapeDtypeStruct((b, la, d), BF16),
                 jax.ShapeDtypeStruct((b, d, la), BF16),
                 jax.ShapeDtypeStruct((b, s5w // LANES, la, LANES), F32),
                 jax.ShapeDtypeStruct((b, l, d), BF16),
                 jax.ShapeDtypeStruct((b, l, d), BF16))
    return pl.pallas_call(
        functools.partial(_inproj_kernel, base=base),
        out_shape=out_shape,
        grid=(b, l // tm),
        in_specs=[pl.BlockSpec((1, tm, d), tok),
                  pl.BlockSpec((1, N_MOD, d), lambda bi, i: (bi, 0, 0)),
                  _resident(g.shape), _resident(w_in.shape), _resident(qg.shape), _resident(kg.shape),
                  _resident(ones.shape),
                  pl.BlockSpec((tm, LANES), lambda bi, i: (i, 0)),
                  pl.BlockSpec((tm, LANES), lambda bi, i: (i, 0))],
        out_specs=(pl.BlockSpec((1, tm, d), tok),
                   pl.BlockSpec((1, tm, d), tok),
                   pl.BlockSpec((1, d, tm), lambda bi, i: (bi, 0, i)),
                   pl.BlockSpec((1, s5w // LANES, tm, LANES), lambda bi, i: (bi, 0, i, 0)),
                   pl.BlockSpec((1, tm, d), tok),
                   pl.BlockSpec((1, tm, d), tok)),
        compiler_params=_cparams(("parallel", "parallel")),
        name="inproj",
    )(x, modx, g, w_in, qg, kg, ones, cos, sin)


def _ctx_kernel(ctx_ref, mod_ref, g1_ref, w13_ref, w2_ref, gm_ref, w_ref, kg_ref, ones_ref,
                k_in, vt_in, u_in, k_ref, vt_ref, u_ref, *, base):
    del k_in, vt_in, u_in
    d = ctx_ref.shape[-1]
    mod = mod_ref[0]
    x1 = _ffn_rows(ctx_ref[0], g1_ref[...], mod, 0, w13_ref, w2_ref)
    xm = _modulated(x1, gm_ref[...], mod, base).astype(BF16)
    p = jnp.dot(xm, w_ref[...], preferred_element_type=F32)
    k_ref[0] = _qk_norm(p[:, :d], kg_ref[...], ones_ref).astype(BF16)
    vt_ref[0] = p[:, d:2 * d].astype(BF16).T
    _store_supergroups(u_ref, slice(None), p[:, 2 * d:])


def _ctx_call(ctx, modc, g1, w13, w2, gm, w_kvu, kg, ones, k, vt, u, base):
    b, lc, d = ctx.shape
    sg = u.shape[1]
    last = k.shape[1] // lc - 1
    anyspec = pl.BlockSpec(memory_space=pl.ANY)
    return pl.pallas_call(
        functools.partial(_ctx_kernel, base=base),
        out_shape=(jax.ShapeDtypeStruct(k.shape, k.dtype), jax.ShapeDtypeStruct(vt.shape, vt.dtype),
                   jax.ShapeDtypeStruct(u.shape, u.dtype)),
        grid=(b,),
        in_specs=[pl.BlockSpec((1, lc, d), lambda bi: (bi, 0, 0)),
                  pl.BlockSpec((1, N_MOD, d), lambda bi: (0, 0, 0)),
                  _resident(g1.shape), _resident(w13.shape), _resident(w2.shape), _resident(gm.shape),
                  _resident(w_kvu.shape), _resident(kg.shape), _resident(ones.shape),
                  anyspec, anyspec, anyspec],
        out_specs=(pl.BlockSpec((1, lc, d), lambda bi: (bi, last, 0)),
                   pl.BlockSpec((1, d, lc), lambda bi: (bi, 0, last)),
                   pl.BlockSpec((1, sg, lc, LANES), lambda bi: (bi, 0, last, 0))),
        input_output_aliases={9: 0, 10: 1, 11: 2},
        compiler_params=_cparams(("parallel",)),
        name="ctx",
    )(ctx, modc, g1, w13, w2, gm, w_kvu, kg, ones, k, vt, u)


def _attn_kernel(bound_ref, lam_ref, subln_ref, q_ref, k_ref, vt_ref, o_ref, *, lc, tk, lam_init, bounded):
    q = q_ref[0]
    la = k_ref.shape[1]
    lane = lax.broadcasted_iota(jnp.int32, q.shape, 1)
    zero = jnp.zeros_like(q)
    q_comp = (jnp.where(lane < HEAD_DIM, q, zero), jnp.where(lane >= HEAD_DIM, q, zero))
    lv = lam_ref[...]
    lam = (jnp.exp(jnp.sum(lv[0:1] * lv[1:2], axis=-1, keepdims=True))
           - jnp.exp(jnp.sum(lv[2:3] * lv[3:4], axis=-1, keepdims=True)) + lam_init)
    blocks = [(0, la)] if bounded else [(st, tk) for st in range(0, la - lc, tk)] + [(la - lc, lc)]
    nt_dims = (((1,), (1,)), ((), ()))
    heads = []
    for qz in q_comp:
        m = l = acc = None
        for st, sz in blocks:
            kb = k_ref[0, st:st + sz, :]
            vb = vt_ref[0, :, st:st + sz]
            s = lax.dot_general(kb, qz, nt_dims, preferred_element_type=F32)
            if bounded:
                p = jnp.exp2(s - bound_ref[...])
                l = jnp.sum(p, axis=0, keepdims=True)
                acc = jnp.dot(vb, p.astype(BF16), preferred_element_type=F32)
                continue
            bm = jnp.max(s, axis=0, keepdims=True)
            if m is None:
                m_new = bm
                p = jnp.exp2(s - m_new)
                l = jnp.sum(p, axis=0, keepdims=True)
                acc = jnp.dot(vb, p.astype(BF16), preferred_element_type=F32)
            else:
                m_new = jnp.maximum(m, bm)
                alpha = jnp.exp2(m - m_new)
                p = jnp.exp2(s - m_new)
                l = alpha * l + jnp.sum(p, axis=0, keepdims=True)
                acc = alpha * acc + jnp.dot(vb, p.astype(BF16), preferred_element_type=F32)
            m = m_new
        heads.append(acc / l)
    o = heads[0] - lam * heads[1]
    ms = jnp.mean(o * o, axis=0, keepdims=True)
    on = o * lax.rsqrt(ms + EPS) * (subln_ref[...] * (1.0 - lam_init))
    o_ref[0] = on.astype(BF16)


def _attn_call(bound, lamv, subln, q, k, vt, lc, lam_init, tq, tk, bounded):
    b, l, d = q.shape
    la = k.shape[1]
    return pl.pallas_call(
        functools.partial(_attn_kernel, lc=lc, tk=tk, lam_init=lam_init, bounded=bounded),
        out_shape=jax.ShapeDtypeStruct((b, d, l), BF16),
        grid=(b, N_HEADS, l // tq),
        in_specs=[pl.BlockSpec(bound.shape, lambda bi, h, i: (0, 0)),
                  pl.BlockSpec(lamv.shape, lambda bi, h, i: (0, 0)),
                  pl.BlockSpec(subln.shape, lambda bi, h, i: (0, 0)),
                  pl.BlockSpec((1, tq, V_DIM), lambda bi, h, i: (bi, i, h)),
                  pl.BlockSpec((1, la, V_DIM), lambda bi, h, i: (bi, 0, h)),
                  pl.BlockSpec((1, V_DIM, la), lambda bi, h, i: (bi, h, 0))],
        out_specs=pl.BlockSpec((1, V_DIM, tq), lambda bi, h, i: (bi, h, i)),
        compiler_params=_cparams(("parallel", "parallel", "arbitrary")),
        name="attn_bounded" if bounded else "attn_online",
    )(bound, lamv, subln, q, k, vt)


def _s5_matrices(lam_re, lam_im, log_dt, b_re, b_im, c_re, c_im, d_skip, seg):
    t = CHUNK
    g_all, p_n = lam_re.shape[1:]
    c_n = b_re.shape[-1]
    sg = g_all // SUPER
    ks = jnp.arange(t + 1, dtype=F32)[:, None, None]
    mats = []
    for dr in range(2):
        lre, lim = lam_re[dr], lam_im[dr]
        dt = jnp.exp(log_dt[dr])[:, None]
        mag = jnp.exp(ks * (lre * dt))
        akr = mag * jnp.cos(ks * (lim * dt))
        aki = mag * jnp.sin(ks * (lim * dt))
        ar, ai = akr[1], aki[1]
        den = lre * lre + lim * lim
        nr = ar - 1.0
        fr = (nr * lre + ai * lim) / den
        fi = (ai * lre - nr * lim) / den
        bbr = fr[..., None] * b_re[dr] - fi[..., None] * b_im[dr]
        bbi = fr[..., None] * b_im[dr] + fi[..., None] * b_re[dr]
        abr = akr[..., None] * bbr - aki[..., None] * bbi
        abi = akr[..., None] * bbi + aki[..., None] * bbr
        kern = (jnp.einsum('gdp,kgpc->kgdc', c_re[dr], abr)
                - jnp.einsum('gdp,kgpc->kgdc', c_im[dr], abi))
        car = c_re[dr][None] * akr[:, :, None, :] - c_im[dr][None] * aki[:, :, None, :]
        cai = c_re[dr][None] * aki[:, :, None, :] + c_im[dr][None] * akr[:, :, None, :]
        mats.append((akr, aki, abr, abi, kern, car, cai))

    s_idx = jnp.arange(t)
    lag = s_idx[None, :] - s_idx[:, None]
    kf, kb = mats[0][4], mats[1][4]
    d_diag = jnp.eye(c_n, dtype=F32)[None] * d_skip.reshape(g_all, 1, c_n)
    kf_l = kf[jnp.clip(lag, 0, t)]
    kb_l = kb[jnp.clip(-lag, 0, t)]
    fwd_on = (lag >= 0)[..., None, None, None]
    bwd_on = (lag <= 0)[..., None, None, None]
    same = (lag == 0)[..., None, None, None]
    mg = jnp.where(fwd_on, kf_l, 0.0) + jnp.where(bwd_on, kb_l, 0.0) + jnp.where(same, d_diag[None, None], 0.0)
    mg = mg.reshape(t, t, sg, SUPER, c_n, c_n)
    m = mg.transpose(2, 0, 3, 5, 1, 4).reshape(sg, t * SUPER * c_n, t * c_n)

    def state_in(abr, abi, order):
        wr = abr[order].reshape(t, sg, SUPER, p_n, c_n)
        wi = abi[order].reshape(t, sg, SUPER, p_n, c_n)
        w = jnp.stack([wr, wi], axis=0)
        return w.transpose(2, 1, 3, 5, 0, 4).reshape(sg, t * SUPER * c_n, 2 * p_n)

    def state_out(car, cai, order):
        vr = car[order].reshape(t, sg, SUPER, c_n, p_n)
        vi = -cai[order].reshape(t, sg, SUPER, c_n, p_n)
        v = jnp.stack([vr, vi], axis=0)
        return v.transpose(2, 0, 3, 5, 1, 4).reshape(sg, 2 * SUPER * p_n, t * c_n)

    wf = state_in(mats[0][2], mats[0][3], (t - 1) - s_idx)
    wb = state_in(mats[1][2], mats[1][3], s_idx)
    vf = state_out(mats[0][5], mats[0][6], s_idx + 1)
    vb = state_out(mats[1][5], mats[1][6], t - s_idx)

    def decay_table(dr, powers):
        lre, lim = lam_re[dr], lam_im[dr]
        dt = jnp.exp(log_dt[dr])[:, None]
        kk = (t * powers).astype(F32)[:, None, None]
        mag = jnp.exp(kk * (lre * dt))
        tab = jnp.stack([mag * jnp.cos(kk * (lim * dt)), mag * jnp.sin(kk * (lim * dt))], axis=0)
        return tab.reshape(2, -1, sg, SUPER * p_n).transpose(2, 0, 1, 3)

    j = jnp.arange(seg)
    bf = lambda a: a.astype(BF16)
    return (bf(m), bf(wf), bf(wb), bf(vf), bf(vb), decay_table(0, j + 1), decay_table(1, seg - j))


def _s5_segment(n):
    return 8 * (-(-n // 64))


def _cmul_add(ar, ai, xr, xi, br, bi):
    return ar * xr - ai * xi + br, ar * xi + ai * xr + bi


def _group_replicator(outer, inner):
    col = np.arange(outer * SUPER * inner)
    src = (col // (SUPER * inner)) * inner + col % inner
    return jnp.asarray(np.arange(outer * inner)[:, None] == src[None, :], dtype=BF16)


def _expand_groups(xc, rep, row_block, col_block):
    dense = jnp.dot(xc, rep, preferred_element_type=F32)
    r = lax.broadcasted_iota(jnp.int32, dense.shape, 0)
    c = lax.broadcasted_iota(jnp.int32, dense.shape, 1)
    own = (r // row_block) % SUPER == (c // col_block) % SUPER
    return jnp.where(own, dense, 0.0).astype(BF16)


def _s5_kernel(u_ref, mc_ref, wfc_ref, wbc_ref, vfc_ref, vbc_ref, rep_out_ref, rep_state_ref, pf_ref, pb_ref,
               y_ref, m_ref, wf_ref, wb_ref, vf_ref, vb_ref, sf_ref, sb_ref, *, nx, nc):
    @pl.when(pl.program_id(1) == 0)
    def _build_matrices():
        rep_out, rep_state = rep_out_ref[...], rep_state_ref[...]
        m_ref[...] = _expand_groups(mc_ref[0], rep_out, S5_GROUP, S5_GROUP)
        wf_ref[...] = _expand_groups(wfc_ref[0], rep_state, S5_GROUP, S5_STATE)
        wb_ref[...] = _expand_groups(wbc_ref[0], rep_state, S5_GROUP, S5_STATE)
        vf_ref[...] = _expand_groups(vfc_ref[0], rep_out, S5_STATE, S5_GROUP)
        vb_ref[...] = _expand_groups(vbc_ref[0], rep_out, S5_STATE, S5_GROUP)

    n = nx + nc
    tiles, npad = sf_ref.shape[1:3]
    seg = npad // 8
    pad = npad - n
    half = tiles * LANES

    def load(ref, rows):
        return tuple(jnp.concatenate([ref[ri, c, rows, :] for c in range(tiles)], axis=-1) for ri in range(2))

    def store(ref, rows, re, im):
        for c in range(tiles):
            ref[0, c, rows, :] = re[:, c * LANES:(c + 1) * LANES]
            ref[1, c, rows, :] = im[:, c * LANES:(c + 1) * LANES]

    z = jnp.concatenate([u_ref[0, 0, pl.ds(t, n, stride=CHUNK), :] for t in range(CHUNK)], axis=-1).astype(BF16)
    zl, zc = z[:nx], z[nx:]
    zpad = [jnp.zeros((pad, z.shape[1]), BF16)] if pad else []
    s = jnp.dot(jnp.concatenate([zc, zl] + zpad, axis=0), wf_ref[...], preferred_element_type=F32)
    store(sf_ref, slice(None), s[:, :half], s[:, half:])
    s = jnp.dot(jnp.concatenate(zpad + [zl, zc], axis=0), wb_ref[...], preferred_element_type=F32)
    store(sb_ref, slice(None), s[:, :half], s[:, half:])

    afr, afi = pf_ref[0, 0, 0:1, :], pf_ref[0, 1, 0:1, :]
    abr, abi = pb_ref[0, 0, seg - 1:seg, :], pb_ref[0, 1, seg - 1:seg, :]

    def step(i, carry):
        hr, hi, gr, gi = carry
        rows = pl.ds(i, 8, stride=seg)
        hr, hi = _cmul_add(afr, afi, hr, hi, *load(sf_ref, rows))
        store(sf_ref, rows, hr, hi)
        rows = pl.ds(seg - 1 - i, 8, stride=seg)
        gr, gi = _cmul_add(abr, abi, gr, gi, *load(sb_ref, rows))
        store(sb_ref, rows, gr, gi)
        return hr, hi, gr, gi

    zero = jnp.zeros((8, half), F32)
    hr, hi, gr, gi = lax.fori_loop(0, seg, step, (zero, zero, zero, zero), unroll=True)

    tfr, tfi = pf_ref[0, 0], pf_ref[0, 1]
    tbr, tbi = pb_ref[0, 0], pb_ref[0, 1]
    cr = ci = None
    for s in range(1, 8):
        er, ei = hr[s - 1:s], hi[s - 1:s]
        if cr is None:
            cr, ci = er, ei
        else:
            cr, ci = _cmul_add(tfr[seg - 1:seg], tfi[seg - 1:seg], cr, ci, er, ei)
        rows = slice(s * seg, (s + 1) * seg)
        store(sf_ref, rows, *_cmul_add(tfr, tfi, cr, ci, *load(sf_ref, rows)))
    cr = ci = None
    for s in range(6, -1, -1):
        er, ei = gr[s + 1:s + 2], gi[s + 1:s + 2]
        if cr is None:
            cr, ci = er, ei
        else:
            cr, ci = _cmul_add(tbr[0:1], tbi[0:1], cr, ci, er, ei)
        rows = slice(s * seg, (s + 1) * seg)
        store(sb_ref, rows, *_cmul_add(tbr, tbi, cr, ci, *load(sb_ref, rows)))

    def entering(ref, shift, first):
        cols = [pltpu.roll(ref[ri, c], shift, 0)[first:first + nx] for ri in range(2) for c in range(tiles)]
        return jnp.concatenate(cols, axis=-1).astype(BF16)

    y = (jnp.dot(zl, m_ref[...], preferred_element_type=F32)
         + jnp.dot(entering(sf_ref, 1, nc), vf_ref[...], preferred_element_type=F32)
         + jnp.dot(entering(sb_ref, npad - 1, pad), vb_ref[...], preferred_element_type=F32))
    for t in range(CHUNK):
        y_ref[0, 0, pl.ds(t, nx, stride=CHUNK), :] = y[:, t * LANES:(t + 1) * LANES]


def _s5_call(u, mats, lc):
    b, sg, la, _ = u.shape
    t = CHUNK
    n, nc = la // t, lc // t
    nx = n - nc
    npad = 8 * _s5_segment(n)
    width = t * LANES
    tiles = width // (2 * LANES)
    mc, wfc, wbc, vfc, vbc, pf, pb = mats
    rep_out = _group_replicator(t, S5_GROUP)
    rep_state = _group_replicator(2, S5_STATE)
    wspec = lambda a: pl.BlockSpec((1,) + a.shape[1:], lambda g, bi: (g,) + (0,) * (a.ndim - 1))
    dense = pltpu.VMEM((width, width), BF16)
    states = pltpu.VMEM((2, tiles, npad, LANES), F32)
    return pl.pallas_call(
        functools.partial(_s5_kernel, nx=nx, nc=nc),
        out_shape=jax.ShapeDtypeStruct((b, sg, nx * t, LANES), F32),
        grid=(sg, b),
        in_specs=[pl.BlockSpec((1, 1, la, LANES), lambda g, bi: (bi, g, 0, 0))]
                 + [wspec(a) for a in (mc, wfc, wbc, vfc, vbc)]
                 + [_resident(rep_out.shape), _resident(rep_state.shape), wspec(pf), wspec(pb)],
        out_specs=pl.BlockSpec((1, 1, nx * t, LANES), lambda g, bi: (bi, g, 0, 0)),
        scratch_shapes=[dense] * 5 + [states] * 2,
        compiler_params=_cparams(("arbitrary", "arbitrary")),
        name="s5",
    )(u, mc, wfc, wbc, vfc, vbc, rep_out, rep_state, pf, pb)


def _gelu_tanh(y):
    return 0.5 * y * (1.0 + jnp.tanh(math.sqrt(2.0 / math.pi) * (y + 0.044715 * (y * y * y))))


def _merge_ffn_kernel(x_ref, mod_ref, y_ref, oxt_ref, ga_ref, gs_ref, wglu_ref, bglu_ref, wpa_ref, wps_ref,
                      wout_ref, g2_ref, w13_ref, w2_ref, o_ref, *, base):
    mod = mod_ref[0]
    for r in range(x_ref.shape[1] // FFN_ROWS):
        rs = slice(r * FFN_ROWS, (r + 1) * FFN_ROWS)
        ge =_gelu_tanh(jnp.concatenate([y_ref[0, g, rs, :] for g in range(y_ref.shape[1])], axis=-1))
        z = jnp.dot(ge.astype(BF16), wglu_ref[...], preferred_element_type=F32) + bglu_ref[...]
        sx = (ge * jax.nn.sigmoid(z)).astype(BF16)
        attn = lax.dot_general(oxt_ref[0, :, rs], wpa_ref[...], (((0,), (0,)), ((), ())),
                               preferred_element_type=F32)
        mix = (ga_ref[0, rs, :].astype(F32) * attn
               + gs_ref[0, rs, :].astype(F32) * jnp.dot(sx, wps_ref[...], preferred_element_type=F32))
        out = jnp.dot(mix.astype(BF16), wout_ref[...], preferred_element_type=F32)
        x2 = x_ref[0, rs, :] + mod[base + 2:base + 3, :] * out
        o_ref[0, rs, :] = _ffn_rows(x2, g2_ref[...], mod, base + 3, w13_ref, w2_ref)


def _merge_ffn_call(x, mod, y, ox, ga, gs, w_glu, b_glu, w_pa, w_ps, w_out, g2, w13, w2, base, tm):
    b, l, d = x.shape
    sg = y.shape[1]
    tok = lambda width: pl.BlockSpec((1, tm, width), lambda bi, i: (bi, i, 0))
    weights = (w_glu, b_glu, w_pa, w_ps, w_out, g2, w13, w2)
    return pl.pallas_call(
        functools.partial(_merge_ffn_kernel, base=base),
        out_shape=jax.ShapeDtypeStruct(x.shape, F32),
        grid=(b, l // tm),
        in_specs=[tok(d), pl.BlockSpec((1, N_MOD, d), lambda bi, i: (bi, 0, 0)),
                  pl.BlockSpec((1, sg, tm, LANES), lambda bi, i: (bi, 0, i, 0)),
                  pl.BlockSpec((1, d, tm), lambda bi, i: (bi, 0, i)), tok(d), tok(d)]
                 + [_resident(w.shape) for w in weights],
        out_specs=tok(d),
        compiler_params=_cparams(("parallel", "parallel")),
        name="merge_ffn2",
    )(x, mod, y, ox, ga, gs, *weights)


def _rope_tables(l):
    n_freq = HEAD_DIM // 4
    inv = ROPE_BASE ** (-jnp.arange(n_freq, dtype=F32) / n_freq)
    pos = jnp.arange(l, dtype=jnp.int32)
    row = (pos // GRID_W).astype(F32)
    col = (pos % GRID_W).astype(F32)
    ang = jnp.concatenate([row[:, None] * inv, col[:, None] * inv], axis=-1)
    cos, sin = jnp.cos(ang), jnp.sin(ang)
    return jnp.concatenate([cos] * 4, axis=-1), jnp.concatenate([-sin, sin, -sin, sin], axis=-1)


def _group_ones():
    grp = np.arange(MXU_WIDTH) // HEAD_DIM
    return jnp.asarray(grp[:, None] == grp[None, :], dtype=BF16)


def kernel(x, c, ctx, c_ctx, w_mod, b_mod, norm_ffn1, w13_ffn1, w2_ffn1, norm_mix, w_in, q_norm, k_norm, lam_q1, lam_k1, lam_q2, lam_k2, subln, s5_lam_re, s5_lam_im, s5_log_dt, s5_b_re, s5_b_im, s5_c_re, s5_c_im, s5_d, w_glu, b_glu, w_pa, w_ps, w_out, norm_ffn2, w13_ffn2, w2_ffn2):
    assert w_mod.shape[0] == 1, "single-layer block"
    b, l, d = x.shape
    lc = ctx.shape[1]
    lam_init = 0.8 - 0.6 * math.exp(-0.3 * 0)
    bf = lambda a: a.astype(BF16)
    row = lambda a: a.reshape(1, -1)

    rows = b + 1
    pad = (-rows) % 8
    cc = jnp.concatenate([c, c_ctx[None, :], jnp.zeros((pad, d), F32)], axis=0)
    mod = _mod_call(cc, w_mod[0], row(b_mod[0]))
    modx = mod[:b].reshape(b, N_MOD, d)
    modc = mod[b:b + 1].reshape(1, N_MOD, d)

    w13_1, w2_1 = bf(w13_ffn1[0]), bf(w2_ffn1[0])
    x1 = _ffn_call(x, modx, row(norm_ffn1[0]), w13_1, w2_1, 0, 1024, "ffn1_x")

    ones = _group_ones()
    cos, sin = _rope_tables(l)
    qg = row(jnp.tile(q_norm[0], d // HEAD_DIM)) * (LOG2E / math.sqrt(HEAD_DIM))
    kg = row(jnp.tile(k_norm[0], d // HEAD_DIM))
    w_inb = bf(w_in[0])
    s5w = s5_d.shape[-1]
    q, k, vt, u, ga, gs = _inproj_call(x1, lc, modx, row(norm_mix[0]), w_inb, qg, kg, ones,
                                       cos, sin, s5w, 3, 1024)
    k, vt, u = _ctx_call(ctx, modc, row(norm_ffn1[0]), w13_1, w2_1, row(norm_mix[0]),
                         w_inb[:, d:3 * d + s5w], kg, ones, k, vt, u, 3)

    lamv = jnp.stack([lam_q1[0], lam_k1[0], lam_q2[0], lam_k2[0]], axis=0)
    bound = (LOG2E * math.sqrt(HEAD_DIM)) * jnp.max(jnp.abs(q_norm[0])) * jnp.max(jnp.abs(k_norm[0]))
    attn = lambda bounded, tq, tk: functools.partial(
        _attn_call, lamv=lamv, subln=subln[0].reshape(-1, 1), q=q, k=k, vt=vt, lc=lc, lam_init=lam_init,
        tq=tq, tk=tk, bounded=bounded)
    ox = lax.cond(2.0 * bound <= MAX_EXP2_SPAN, attn(True, 1024, None), attn(False, 256, 1024),
                  bound.reshape(1, 1))

    mats = _s5_matrices(s5_lam_re[0], s5_lam_im[0], s5_log_dt[0], s5_b_re[0], s5_b_im[0],
                        s5_c_re[0], s5_c_im[0], s5_d[0], _s5_segment((l + lc) // CHUNK))
    y = _s5_call(u, mats, lc)

    return _merge_ffn_call(x1, modx, y, ox, ga, gs, bf(w_glu[0]), row(b_glu[0]), bf(w_pa[0]), bf(w_ps[0]),
                           bf(w_out[0]), row(norm_ffn2[0]), bf(w13_ffn2[0]), bf(w2_ffn2[0]), 3, 512)
```

```python
import functools
import math

import jax
import jax.numpy as jnp
import numpy as np
from jax import lax
from jax.experimental import pallas as pl
from jax.experimental.pallas import tpu as pltpu

F32 = jnp.float32
BF16 = jnp.bfloat16

EPS = 1e-6
ROPE_BASE = 10000.0
GRID_W = 64
N_HEADS = 8
HEAD_DIM = 64
V_DIM = 2 * HEAD_DIM
S5_GROUP = 16
S5_STATE = 64
N_MOD = 9
LANES = 128
MXU_WIDTH = 256
CHUNK = 8
SUPER = LANES // S5_GROUP
VMEM_CAPACITY = 64 * 1024 * 1024
VMEM_LIMIT = VMEM_CAPACITY - 8 * 1024 * 1024
LOG2E = math.log2(math.e)
MAX_EXP2_SPAN = 100.0

FFN_ROWS = 256
PROJ_ROWS = 256
TILE_FFN1 = 1024
TILE_PROJ = 1024
TILE_MERGE = 512
TILE_Q_BOUNDED = 1024
TILE_Q_ONLINE, TILE_K_ONLINE = 256, 1024


def _cparams(sem):
    return pltpu.CompilerParams(dimension_semantics=sem, vmem_limit_bytes=VMEM_LIMIT)


def _resident(shape):
    nd = len(shape)
    return pl.BlockSpec(shape, lambda *_: (0,) * nd, pipeline_mode=pl.Buffered(1))


def _silu(a):
    return a * jax.nn.sigmoid(a)


def _modulated(x, g, mod, base):
    ms = jnp.mean(x * x, axis=-1, keepdims=True)
    xn = x * lax.rsqrt(ms + EPS) * g
    return xn * (1.0 + mod[base + 1:base + 2, :]) + mod[base:base + 1, :]


def _mod_kernel(c_ref, w_ref, b_ref, o_ref):
    a = _silu(c_ref[...]).astype(BF16)
    o_ref[...] = jnp.dot(a, w_ref[...].astype(BF16), preferred_element_type=F32) + b_ref[...]


def _mod_call(cc, w_mod, b_mod):
    rows, d = cc.shape
    n = w_mod.shape[1]
    tn = d
    return pl.pallas_call(
        _mod_kernel,
        out_shape=jax.ShapeDtypeStruct((rows, n), F32),
        grid=(n // tn,),
        in_specs=[pl.BlockSpec((rows, d), lambda j: (0, 0)),
                  pl.BlockSpec((d, tn), lambda j: (0, j)),
                  pl.BlockSpec((1, tn), lambda j: (0, j))],
        out_specs=pl.BlockSpec((rows, tn), lambda j: (0, j)),
        compiler_params=_cparams(("arbitrary",)),
        name="mod",
    )(cc, w_mod, b_mod)


def _ffn_rows(x, g, mod, base, w13_ref, w2_ref):
    dff = w2_ref.shape[0]
    xm = _modulated(x, g, mod, base).astype(BF16)
    h = jnp.dot(xm, w13_ref[...], preferred_element_type=F32)
    act = (_silu(h[:, :dff]) * h[:, dff:]).astype(BF16)
    return x + (0.5 * mod[base + 2:base + 3, :]) * jnp.dot(act, w2_ref[...], preferred_element_type=F32)


def _ffn_kernel(x_ref, mod_ref, g_ref, w13_ref, w2_ref, o_ref, *, base):
    for r in range(x_ref.shape[1] // FFN_ROWS):
        rs = slice(r * FFN_ROWS, (r + 1) * FFN_ROWS)
        o_ref[0, rs, :] = _ffn_rows(x_ref[0, rs, :], g_ref[...], mod_ref[0], base, w13_ref, w2_ref)


def _ffn_call(x, mod, g, w13, w2, base, tm, name):
    b, l, d = x.shape
    return pl.pallas_call(
        functools.partial(_ffn_kernel, base=base),
        out_shape=jax.ShapeDtypeStruct(x.shape, F32),
        grid=(b, l // tm),
        in_specs=[pl.BlockSpec((1, tm, d), lambda bi, i: (bi, i, 0)),
                  pl.BlockSpec((1, N_MOD, d), lambda bi, i: (bi, 0, 0)),
                  _resident(g.shape), _resident(w13.shape), _resident(w2.shape)],
        out_specs=pl.BlockSpec((1, tm, d), lambda bi, i: (bi, i, 0)),
        compiler_params=_cparams(("parallel", "parallel")),
        name=name,
    )(x, mod, g, w13, w2)


def _qk_norm(t, gain, ones_ref):
    sq = (t * t).astype(BF16)
    w = ones_ref.shape[0]
    ss = jnp.concatenate([jnp.dot(sq[:, j:j + w], ones_ref[...], preferred_element_type=F32)
                          for j in range(0, t.shape[-1], w)], axis=-1)
    return t * lax.rsqrt(ss * (1.0 / HEAD_DIM) + EPS) * gain


def _rope(t, cos, sin_signed):
    n = t.shape[-1]
    half = HEAD_DIM // 2
    lane = lax.broadcasted_iota(jnp.int32, t.shape, 1)
    swapped = jnp.where(lane % HEAD_DIM < half, pltpu.roll(t, n - half, 1), pltpu.roll(t, half, 1))
    reps = n // cos.shape[-1]
    cosf = jnp.concatenate([cos] * reps, axis=-1)
    sinf = jnp.concatenate([sin_signed] * reps, axis=-1)
    return t * cosf + swapped * sinf


def _store_supergroups(u_ref, rs, u):
    for g in range(u_ref.shape[1]):
        u_ref[0, g, rs, :] = u[:, g * LANES:(g + 1) * LANES]


def _inproj_kernel(x_ref, mod_ref, g_ref, w_ref, qg_ref, kg_ref, ones_ref, cos_ref, sin_ref,
                   q_ref, k_ref, vt_ref, u_ref, ga_ref, gs_ref, *, base):
    d = x_ref.shape[-1]
    qc, kc, vc = d, 2 * d, 3 * d
    uc = vc + u_ref.shape[1] * LANES
    for r in range(x_ref.shape[1] // PROJ_ROWS):
        rs = slice(r * PROJ_ROWS, (r + 1) * PROJ_ROWS)
        xm = _modulated(x_ref[0, rs, :], g_ref[...], mod_ref[0], base).astype(BF16)
        proj = lambda lo, hi: jnp.dot(xm, w_ref[:, lo:hi], preferred_element_type=F32)
        cos = cos_ref[rs, :]
        sin = sin_ref[rs, :]
        tq = proj(0, qc)
        tk = proj(qc, kc)
        q_ref[0, rs, :] = _rope(_qk_norm(tq, qg_ref[...], ones_ref), cos, sin).astype(BF16)
        vt_ref[0, :, rs] = proj(kc, vc).astype(BF16).T
        k_ref[0, rs, :] = _rope(_qk_norm(tk, kg_ref[...], ones_ref), cos, sin).astype(BF16)
        _store_supergroups(u_ref, rs, proj(vc, uc))
        ga_ref[0, rs, :] = jax.nn.sigmoid(proj(uc, uc + d)).astype(BF16)
        gs_ref[0, rs, :] = jax.nn.sigmoid(proj(uc + d, uc + 2 * d)).astype(BF16)


def _inproj_call(x, lc, modx, g, w_in, qg, kg, ones, cos, sin, s5w, base, tm):
    b, l, d = x.shape
    la = l + lc
    tok = lambda bi, i: (bi, i, 0)
    out_shape = (jax.ShapeDtypeStruct((b, l, d), BF16),
                 jax.ShapeDtypeStruct((b, la, d), BF16),
                 jax.ShapeDtypeStruct((b, d, la), BF16),
                 jax.ShapeDtypeStruct((b, s5w // LANES, la, LANES), F32),
                 jax.ShapeDtypeStruct((b, l, d), BF16),
                 jax.ShapeDtypeStruct((b, l, d), BF16))
    return pl.pallas_call(
        functools.partial(_inproj_kernel, base=base),
        out_shape=out_shape,
        grid=(b, l // tm),
        in_specs=[pl.BlockSpec((1, tm, d), tok),
                  pl.BlockSpec((1, N_MOD, d), lambda bi, i: (bi, 0, 0)),
                  _resident(g.shape), _resident(w_in.shape), _resident(qg.shape), _resident(kg.shape),
                  _resident(ones.shape),
                  pl.BlockSpec((tm, LANES), lambda bi, i: (i, 0)),
                  pl.BlockSpec((tm, LANES), lambda bi, i: (i, 0))],
        out_specs=(pl.BlockSpec((1, tm, d), tok),
                   pl.BlockSpec((1, tm, d), tok),
                   pl.BlockSpec((1, d, tm), lambda bi, i: (bi, 0, i)),
                   pl.BlockSpec((1, s5w // LANES, tm, LANES), lambda bi, i: (bi, 0, i, 0)),
                   pl.BlockSpec((1, tm, d), tok),
                   pl.BlockSpec((1, tm, d), tok)),
        compiler_params=_cparams(("parallel", "parallel")),
        name="inproj",
    )(x, modx, g, w_in, qg, kg, ones, cos, sin)


def _ctx_kernel(ctx_ref, mod_ref, g1_ref, w13_ref, w2_ref, gm_ref, w_ref, kg_ref, ones_ref,
                k_in, vt_in, u_in, k_ref, vt_ref, u_ref, *, base):
    del k_in, vt_in, u_in
    d = ctx_ref.shape[-1]
    mod = mod_ref[0]
    x1 = _ffn_rows(ctx_ref[0], g1_ref[...], mod, 0, w13_ref, w2_ref)
    xm = _modulated(x1, gm_ref[...], mod, base).astype(BF16)
    p = jnp.dot(xm, w_ref[...], preferred_element_type=F32)
    k_ref[0] = _qk_norm(p[:, :d], kg_ref[...], ones_ref).astype(BF16)
    vt_ref[0] = p[:, d:2 * d].astype(BF16).T
    _store_supergroups(u_ref, slice(None), p[:, 2 * d:])


def _ctx_call(ctx, modc, g1, w13, w2, gm, w_kvu, kg, ones, k, vt, u, base):
    b, lc, d = ctx.shape
    sg = u.shape[1]
    last = k.shape[1] // lc - 1
    anyspec = pl.BlockSpec(memory_space=pl.ANY)
    return pl.pallas_call(
        functools.partial(_ctx_kernel, base=base),
        out_shape=(jax.ShapeDtypeStruct(k.shape, k.dtype), jax.ShapeDtypeStruct(vt.shape, vt.dtype),
                   jax.ShapeDtypeStruct(u.shape, u.dtype)),
        grid=(b,),
        in_specs=[pl.BlockSpec((1, lc, d), lambda bi: (bi, 0, 0)),
                  pl.BlockSpec((1, N_MOD, d), lambda bi: (0, 0, 0)),
                  _resident(g1.shape), _resident(w13.shape), _resident(w2.shape), _resident(gm.shape),
                  _resident(w_kvu.shape), _resident(kg.shape), _resident(ones.shape),
                  anyspec, anyspec, anyspec],
        out_specs=(pl.BlockSpec((1, lc, d), lambda bi: (bi, last, 0)),
                   pl.BlockSpec((1, d, lc), lambda bi: (bi, 0, last)),
                   pl.BlockSpec((1, sg, lc, LANES), lambda bi: (bi, 0, last, 0))),
        input_output_aliases={9: 0, 10: 1, 11: 2},
        compiler_params=_cparams(("parallel",)),
        name="ctx",
    )(ctx, modc, g1, w13, w2, gm, w_kvu, kg, ones, k, vt, u)


def _attn_kernel(bound_ref, lam_ref, subln_ref, q_ref, k_ref, vt_ref, o_ref, *, lc, tk, lam_init, bounded):
    q = q_ref[0]
    la = k_ref.shape[1]
    lane = lax.broadcasted_iota(jnp.int32, q.shape, 1)
    zero = jnp.zeros_like(q)
    q_comp = (jnp.where(lane < HEAD_DIM, q, zero), jnp.where(lane >= HEAD_DIM, q, zero))
    lv = lam_ref[...]
    lam = (jnp.exp(jnp.sum(lv[0:1] * lv[1:2], axis=-1, keepdims=True))
           - jnp.exp(jnp.sum(lv[2:3] * lv[3:4], axis=-1, keepdims=True)) + lam_init)
    blocks = [(0, la)] if bounded else [(st, tk) for st in range(0, la - lc, tk)] + [(la - lc, lc)]
    nt_dims = (((1,), (1,)), ((), ()))
    heads = []
    for qz in q_comp:
        m = l = acc = None
        for st, sz in blocks:
            kb = k_ref[0, st:st + sz, :]
            vb = vt_ref[0, :, st:st + sz]
            s = lax.dot_general(kb, qz, nt_dims, preferred_element_type=F32)
            if bounded:
                p = jnp.exp2(s - bound_ref[...])
                l = jnp.sum(p, axis=0, keepdims=True)
                acc = jnp.dot(vb, p.astype(BF16), preferred_element_type=F32)
                continue
            bm = jnp.max(s, axis=0, keepdims=True)
            if m is None:
                m_new = bm
                p = jnp.exp2(s - m_new)
                l = jnp.sum(p, axis=0, keepdims=True)
                acc = jnp.dot(vb, p.astype(BF16), preferred_element_type=F32)
            else:
                m_new = jnp.maximum(m, bm)
                alpha = jnp.exp2(m - m_new)
                p = jnp.exp2(s - m_new)
                l = alpha * l + jnp.sum(p, axis=0, keepdims=True)
                acc = alpha * acc + jnp.dot(vb, p.astype(BF16), preferred_element_type=F32)
            m = m_new
        heads.append(acc * (1.0 / l))
    o = heads[0] - lam * heads[1]
    ms = jnp.mean(o * o, axis=0, keepdims=True)
    on = o * lax.rsqrt(ms + EPS) * (subln_ref[...] * (1.0 - lam_init))
    o_ref[0] = on.astype(BF16)


def _attn_call(bound, lamv, subln, q, k, vt, lc, lam_init, tq, tk, bounded):
    b, l, d = q.shape
    la = k.shape[1]
    return pl.pallas_call(
        functools.partial(_attn_kernel, lc=lc, tk=tk, lam_init=lam_init, bounded=bounded),
        out_shape=jax.ShapeDtypeStruct((b, d, l), BF16),
        grid=(b, N_HEADS, l // tq),
        in_specs=[pl.BlockSpec(bound.shape, lambda bi, h, i: (0, 0)),
                  pl.BlockSpec(lamv.shape, lambda bi, h, i: (0, 0)),
                  pl.BlockSpec(subln.shape, lambda bi, h, i: (0, 0)),
                  pl.BlockSpec((1, tq, V_DIM), lambda bi, h, i: (bi, i, h)),
                  pl.BlockSpec((1, la, V_DIM), lambda bi, h, i: (bi, 0, h)),
                  pl.BlockSpec((1, V_DIM, la), lambda bi, h, i: (bi, h, 0))],
        out_specs=pl.BlockSpec((1, V_DIM, tq), lambda bi, h, i: (bi, h, i)),
        compiler_params=_cparams(("parallel", "parallel", "arbitrary")),
        name="attn_bounded" if bounded else "attn_online",
    )(bound, lamv, subln, q, k, vt)


def _s5_matrices(lam_re, lam_im, log_dt, b_re, b_im, c_re, c_im, d_skip, seg):
    t = CHUNK
    g_all, p_n = lam_re.shape[1:]
    c_n = b_re.shape[-1]
    sg = g_all // SUPER
    ks = jnp.arange(t + 1, dtype=F32)[:, None, None]
    mats = []
    for dr in range(2):
        lre, lim = lam_re[dr], lam_im[dr]
        dt = jnp.exp(log_dt[dr])[:, None]
        mag = jnp.exp(ks * (lre * dt))
        akr = mag * jnp.cos(ks * (lim * dt))
        aki = mag * jnp.sin(ks * (lim * dt))
        ar, ai = akr[1], aki[1]
        den = lre * lre + lim * lim
        nr = ar - 1.0
        fr = (nr * lre + ai * lim) / den
        fi = (ai * lre - nr * lim) / den
        bbr = fr[..., None] * b_re[dr] - fi[..., None] * b_im[dr]
        bbi = fr[..., None] * b_im[dr] + fi[..., None] * b_re[dr]
        abr = akr[..., None] * bbr - aki[..., None] * bbi
        abi = akr[..., None] * bbi + aki[..., None] * bbr
        kern = (jnp.einsum('gdp,kgpc->kgdc', c_re[dr], abr)
                - jnp.einsum('gdp,kgpc->kgdc', c_im[dr], abi))
        car = c_re[dr][None] * akr[:, :, None, :] - c_im[dr][None] * aki[:, :, None, :]
        cai = c_re[dr][None] * aki[:, :, None, :] + c_im[dr][None] * akr[:, :, None, :]
        mats.append((akr, aki, abr, abi, kern, car, cai))

    s_idx = jnp.arange(t)
    lag = s_idx[None, :] - s_idx[:, None]
    kf, kb = mats[0][4], mats[1][4]
    d_diag = jnp.eye(c_n, dtype=F32)[None] * d_skip.reshape(g_all, 1, c_n)
    kf_l = kf[jnp.clip(lag, 0, t)]
    kb_l = kb[jnp.clip(-lag, 0, t)]
    fwd_on = (lag >= 0)[..., None, None, None]
    bwd_on = (lag <= 0)[..., None, None, None]
    same = (lag == 0)[..., None, None, None]
    mg = jnp.where(fwd_on, kf_l, 0.0) + jnp.where(bwd_on, kb_l, 0.0) + jnp.where(same, d_diag[None, None], 0.0)
    mg = mg.reshape(t, t, sg, SUPER, c_n, c_n)
    m = mg.transpose(2, 0, 3, 5, 1, 4).reshape(sg, t * SUPER * c_n, t * c_n)

    def state_in(abr, abi, order):
        wr = abr[order].reshape(t, sg, SUPER, p_n, c_n)
        wi = abi[order].reshape(t, sg, SUPER, p_n, c_n)
        w = jnp.stack([wr, wi], axis=0)
        return w.transpose(2, 1, 3, 5, 0, 4).reshape(sg, t * SUPER * c_n, 2 * p_n)

    def state_out(car, cai, order):
        vr = car[order].reshape(t, sg, SUPER, c_n, p_n)
        vi = -cai[order].reshape(t, sg, SUPER, c_n, p_n)
        v = jnp.stack([vr, vi], axis=0)
        return v.transpose(2, 0, 3, 5, 1, 4).reshape(sg, 2 * SUPER * p_n, t * c_n)

    wf = state_in(mats[0][2], mats[0][3], (t - 1) - s_idx)
    wb = state_in(mats[1][2], mats[1][3], s_idx)
    vf = state_out(mats[0][5], mats[0][6], s_idx + 1)
    vb = state_out(mats[1][5], mats[1][6], t - s_idx)

    def decay_table(dr, powers):
        lre, lim = lam_re[dr], lam_im[dr]
        dt = jnp.exp(log_dt[dr])[:, None]
        kk = (t * powers).astype(F32)[:, None, None]
        mag = jnp.exp(kk * (lre * dt))
        tab = jnp.stack([mag * jnp.cos(kk * (lim * dt)), mag * jnp.sin(kk * (lim * dt))], axis=0)
        return tab.reshape(2, -1, sg, SUPER * p_n).transpose(2, 0, 1, 3)

    j = jnp.arange(seg)
    bf = lambda a: a.astype(BF16)
    return (bf(m), bf(wf), bf(wb), bf(vf), bf(vb), decay_table(0, j + 1), decay_table(1, seg - j))


def _s5_segment(n):
    return 8 * (-(-n // 64))


def _cmul_add(ar, ai, xr, xi, br, bi):
    return ar * xr - ai * xi + br, ar * xi + ai * xr + bi


def _group_replicator(outer, inner):
    col = np.arange(outer * SUPER * inner)
    src = (col // (SUPER * inner)) * inner + col % inner
    return jnp.asarray(np.arange(outer * inner)[:, None] == src[None, :], dtype=BF16)


def _expand_groups(xc, rep, row_block, col_block):
    dense = jnp.dot(xc, rep, preferred_element_type=F32)
    r = lax.broadcasted_iota(jnp.int32, dense.shape, 0)
    c = lax.broadcasted_iota(jnp.int32, dense.shape, 1)
    own = (r // row_block) % SUPER == (c // col_block) % SUPER
    return jnp.where(own, dense, 0.0).astype(BF16)


def _s5_kernel(u_ref, mc_ref, wfc_ref, wbc_ref, vfc_ref, vbc_ref, rep_out_ref, rep_state_ref, pf_ref, pb_ref,
               y_ref, m_ref, wf_ref, wb_ref, vf_ref, vb_ref, sf_ref, sb_ref, *, nx, nc):
    @pl.when(pl.program_id(1) == 0)
    def _build_matrices():
        rep_out, rep_state = rep_out_ref[...], rep_state_ref[...]
        m_ref[...] = _expand_groups(mc_ref[0], rep_out, S5_GROUP, S5_GROUP)
        wf_ref[...] = _expand_groups(wfc_ref[0], rep_state, S5_GROUP, S5_STATE)
        wb_ref[...] = _expand_groups(wbc_ref[0], rep_state, S5_GROUP, S5_STATE)
        vf_ref[...] = _expand_groups(vfc_ref[0], rep_out, S5_STATE, S5_GROUP)
        vb_ref[...] = _expand_groups(vbc_ref[0], rep_out, S5_STATE, S5_GROUP)

    n = nx + nc
    tiles, npad = sf_ref.shape[1:3]
    seg = npad // 8
    pad = npad - n
    half = tiles * LANES

    def load(ref, rows):
        return tuple(jnp.concatenate([ref[ri, c, rows, :] for c in range(tiles)], axis=-1) for ri in range(2))

    def store(ref, rows, re, im):
        for c in range(tiles):
            ref[0, c, rows, :] = re[:, c * LANES:(c + 1) * LANES]
            ref[1, c, rows, :] = im[:, c * LANES:(c + 1) * LANES]

    z = jnp.concatenate([u_ref[0, 0, pl.ds(t, n, stride=CHUNK), :] for t in range(CHUNK)], axis=-1).astype(BF16)
    zl, zc = z[:nx], z[nx:]
    zpad = [jnp.zeros((pad, z.shape[1]), BF16)] if pad else []
    s = jnp.dot(jnp.concatenate([zc, zl] + zpad, axis=0), wf_ref[...], preferred_element_type=F32)
    store(sf_ref, slice(None), s[:, :half], s[:, half:])
    s = jnp.dot(jnp.concatenate(zpad + [zl, zc], axis=0), wb_ref[...], preferred_element_type=F32)
    store(sb_ref, slice(None), s[:, :half], s[:, half:])

    afr, afi = pf_ref[0, 0, 0:1, :], pf_ref[0, 1, 0:1, :]
    abr, abi = pb_ref[0, 0, seg - 1:seg, :], pb_ref[0, 1, seg - 1:seg, :]

    def step(i, carry):
        hr, hi, gr, gi = carry
        rows = pl.ds(i, 8, stride=seg)
        hr, hi = _cmul_add(afr, afi, hr, hi, *load(sf_ref, rows))
        store(sf_ref, rows, hr, hi)
        rows = pl.ds(seg - 1 - i, 8, stride=seg)
        gr, gi = _cmul_add(abr, abi, gr, gi, *load(sb_ref, rows))
        store(sb_ref, rows, gr, gi)
        return hr, hi, gr, gi

    zero = jnp.zeros((8, half), F32)
    hr, hi, gr, gi = lax.fori_loop(0, seg, step, (zero, zero, zero, zero), unroll=True)

    tfr, tfi = pf_ref[0, 0], pf_ref[0, 1]
    tbr, tbi = pb_ref[0, 0], pb_ref[0, 1]
    cr = ci = None
    for s in range(1, 8):
        er, ei = hr[s - 1:s], hi[s - 1:s]
        if cr is None:
            cr, ci = er, ei
        else:
            cr, ci = _cmul_add(tfr[seg - 1:seg], tfi[seg - 1:seg], cr, ci, er, ei)
        rows = slice(s * seg, (s + 1) * seg)
        store(sf_ref, rows, *_cmul_add(tfr, tfi, cr, ci, *load(sf_ref, rows)))
    cr = ci = None
    for s in range(6, -1, -1):
        er, ei = gr[s + 1:s + 2], gi[s + 1:s + 2]
        if cr is None:
            cr, ci = er, ei
        else:
            cr, ci = _cmul_add(tbr[0:1], tbi[0:1], cr, ci, er, ei)
        rows = slice(s * seg, (s + 1) * seg)
        store(sb_ref, rows, *_cmul_add(tbr, tbi, cr, ci, *load(sb_ref, rows)))

    def entering(ref, shift, first):
        cols = [pltpu.roll(ref[ri, c], shift, 0)[first:first + nx] for ri in range(2) for c in range(tiles)]
        return jnp.concatenate(cols, axis=-1).astype(BF16)

    y = (jnp.dot(zl, m_ref[...], preferred_element_type=F32)
         + jnp.dot(entering(sf_ref, 1, nc), vf_ref[...], preferred_element_type=F32)
         + jnp.dot(entering(sb_ref, npad - 1, pad), vb_ref[...], preferred_element_type=F32))
    for t in range(CHUNK):
        y_ref[0, 0, pl.ds(t, nx, stride=CHUNK), :] = y[:, t * LANES:(t + 1) * LANES]


def _s5_call(u, mats, lc):
    b, sg, la, _ = u.shape
    t = CHUNK
    n, nc = la // t, lc // t
    nx = n - nc
    npad = 8 * _s5_segment(n)
    width = t * LANES
    tiles = width // (2 * LANES)
    mc, wfc, wbc, vfc, vbc, pf, pb = mats
    rep_out = _group_replicator(t, S5_GROUP)
    rep_state = _group_replicator(2, S5_STATE)
    wspec = lambda a: pl.BlockSpec((1,) + a.shape[1:], lambda g, bi: (g,) + (0,) * (a.ndim - 1))
    dense = pltpu.VMEM((width, width), BF16)
    states = pltpu.VMEM((2, tiles, npad, LANES), F32)
    return pl.pallas_call(
        functools.partial(_s5_kernel, nx=nx, nc=nc),
        out_shape=jax.ShapeDtypeStruct((b, sg, nx * t, LANES), F32),
        grid=(sg, b),
        in_specs=[pl.BlockSpec((1, 1, la, LANES), lambda g, bi: (bi, g, 0, 0))]
                 + [wspec(a) for a in (mc, wfc, wbc, vfc, vbc)]
                 + [_resident(rep_out.shape), _resident(rep_state.shape), wspec(pf), wspec(pb)],
        out_specs=pl.BlockSpec((1, 1, nx * t, LANES), lambda g, bi: (bi, g, 0, 0)),
        scratch_shapes=[dense] * 5 + [states] * 2,
        compiler_params=_cparams(("arbitrary", "arbitrary")),
        name="s5",
    )(u, mc, wfc, wbc, vfc, vbc, rep_out, rep_state, pf, pb)


def _gelu_tanh(y):
    return 0.5 * y * (1.0 + jnp.tanh(math.sqrt(2.0 / math.pi) * (y + 0.044715 * (y * y * y))))


def _merge_ffn_kernel(x_ref, mod_ref, y_ref, oxt_ref, ga_ref, gs_ref, wglu_ref, bglu_ref, wpa_ref, wps_ref,
                      wout_ref, g2_ref, w13_ref, w2_ref, o_ref, *, base):
    mod = mod_ref[0]
    for r in range(x_ref.shape[1] // FFN_ROWS):
        rs = slice(r * FFN_ROWS, (r + 1) * FFN_ROWS)
        ge =_gelu_tanh(jnp.concatenate([y_ref[0, g, rs, :] for g in range(y_ref.shape[1])], axis=-1))
        z = jnp.dot(ge.astype(BF16), wglu_ref[...], preferred_element_type=F32) + bglu_ref[...]
        sx = (ge * jax.nn.sigmoid(z)).astype(BF16)
        attn = lax.dot_general(oxt_ref[0, :, rs], wpa_ref[...], (((0,), (0,)), ((), ())),
                               preferred_element_type=F32)
        mix = (ga_ref[0, rs, :].astype(F32) * attn
               + gs_ref[0, rs, :].astype(F32) * jnp.dot(sx, wps_ref[...], preferred_element_type=F32))
        out = jnp.dot(mix.astype(BF16), wout_ref[...], preferred_element_type=F32)
        x2 = x_ref[0, rs, :] + mod[base + 2:base + 3, :] * out
        o_ref[0, rs, :] = _ffn_rows(x2, g2_ref[...], mod, base + 3, w13_ref, w2_ref)


def _merge_ffn_call(x, mod, y, oxt, ga, gs, w_glu, b_glu, w_pa, w_ps, w_out, g2, w13, w2, base, tm):
    b, l, d = x.shape
    sg = y.shape[1]
    tok = lambda width: pl.BlockSpec((1, tm, width), lambda bi, i: (bi, i, 0))
    weights = (w_glu, b_glu, w_pa, w_ps, w_out, g2, w13, w2)
    return pl.pallas_call(
        functools.partial(_merge_ffn_kernel, base=base),
        out_shape=jax.ShapeDtypeStruct(x.shape, F32),
        grid=(b, l // tm),
        in_specs=[tok(d), pl.BlockSpec((1, N_MOD, d), lambda bi, i: (bi, 0, 0)),
                  pl.BlockSpec((1, sg, tm, LANES), lambda bi, i: (bi, 0, i, 0)),
                  pl.BlockSpec((1, d, tm), lambda bi, i: (bi, 0, i)), tok(d), tok(d)]
                 + [_resident(w.shape) for w in weights],
        out_specs=tok(d),
        compiler_params=_cparams(("parallel", "parallel")),
        name="merge_ffn2",
    )(x, mod, y, oxt, ga, gs, *weights)


def _rope_tables(l):
    n_freq = HEAD_DIM // 4
    inv = ROPE_BASE ** (-jnp.arange(n_freq, dtype=F32) / n_freq)
    pos = jnp.arange(l, dtype=jnp.int32)
    row = (pos // GRID_W).astype(F32)
    col = (pos % GRID_W).astype(F32)
    ang = jnp.concatenate([row[:, None] * inv, col[:, None] * inv], axis=-1)
    cos, sin = jnp.cos(ang), jnp.sin(ang)
    return jnp.concatenate([cos] * 4, axis=-1), jnp.concatenate([-sin, sin, -sin, sin], axis=-1)


def _group_ones():
    grp = np.arange(MXU_WIDTH) // HEAD_DIM
    return jnp.asarray(grp[:, None] == grp[None, :], dtype=BF16)


def kernel(x, c, ctx, c_ctx, w_mod, b_mod, norm_ffn1, w13_ffn1, w2_ffn1, norm_mix, w_in, q_norm, k_norm, lam_q1, lam_k1, lam_q2, lam_k2, subln, s5_lam_re, s5_lam_im, s5_log_dt, s5_b_re, s5_b_im, s5_c_re, s5_c_im, s5_d, w_glu, b_glu, w_pa, w_ps, w_out, norm_ffn2, w13_ffn2, w2_ffn2):
    assert w_mod.shape[0] == 1, "single-layer block"
    b, l, d = x.shape
    lc = ctx.shape[1]
    assert l % max(TILE_FFN1, TILE_PROJ, TILE_MERGE, TILE_Q_BOUNDED, TILE_K_ONLINE) == 0 and l % lc == 0
    assert lc % (2 * CHUNK) == 0 and d == N_HEADS * V_DIM
    lam_init = 0.8 - 0.6 * math.exp(-0.3 * 0)
    bf = lambda a: a.astype(BF16)
    row = lambda a: a.reshape(1, -1)

    rows = b + 1
    pad = (-rows) % 8
    cc = jnp.concatenate([c, c_ctx[None, :], jnp.zeros((pad, d), F32)], axis=0)
    mod = _mod_call(cc, w_mod[0], row(b_mod[0]))
    modx = mod[:b].reshape(b, N_MOD, d)
    modc = mod[b:b + 1].reshape(1, N_MOD, d)

    w13_1, w2_1 = bf(w13_ffn1[0]), bf(w2_ffn1[0])
    x1 = _ffn_call(x, modx, row(norm_ffn1[0]), w13_1, w2_1, 0, TILE_FFN1, "ffn1_x")

    ones = _group_ones()
    cos, sin = _rope_tables(l)
    qg = row(jnp.tile(q_norm[0], d // HEAD_DIM)) * (LOG2E / math.sqrt(HEAD_DIM))
    kg = row(jnp.tile(k_norm[0], d // HEAD_DIM))
    w_inb = bf(w_in[0])
    s5w = s5_d.shape[-1]
    q, k, vt, u, ga, gs = _inproj_call(x1, lc, modx, row(norm_mix[0]), w_inb, qg, kg, ones,
                                       cos, sin, s5w, 3, TILE_PROJ)
    k, vt, u = _ctx_call(ctx, modc, row(norm_ffn1[0]), w13_1, w2_1, row(norm_mix[0]),
                         w_inb[:, d:3 * d + s5w], kg, ones, k, vt, u, 3)

    lamv = jnp.stack([lam_q1[0], lam_k1[0], lam_q2[0], lam_k2[0]], axis=0)
    bound = (LOG2E * math.sqrt(HEAD_DIM)) * jnp.max(jnp.abs(q_norm[0])) * jnp.max(jnp.abs(k_norm[0]))
    attn = lambda bounded, tq, tk: functools.partial(
        _attn_call, lamv=lamv, subln=subln[0].reshape(-1, 1), q=q, k=k, vt=vt, lc=lc, lam_init=lam_init,
        tq=tq, tk=tk, bounded=bounded)
    oxt = lax.cond(2.0 * bound <= MAX_EXP2_SPAN, attn(True, TILE_Q_BOUNDED, None),
                   attn(False, TILE_Q_ONLINE, TILE_K_ONLINE), bound.reshape(1, 1))

    mats = _s5_matrices(s5_lam_re[0], s5_lam_im[0], s5_log_dt[0], s5_b_re[0], s5_b_im[0],
                        s5_c_re[0], s5_c_im[0], s5_d[0], _s5_segment((l + lc) // CHUNK))
    y = _s5_call(u, mats, lc)

    return _merge_ffn_call(x1, modx, y, oxt, ga, gs, bf(w_glu[0]), row(b_glu[0]), bf(w_pa[0]), bf(w_ps[0]),
                           bf(w_out[0]), row(norm_ffn2[0]), bf(w13_ffn2[0]), bf(w2_ffn2[0]), 3, TILE_MERGE)
```

```python
import functools
import math

import jax
import jax.numpy as jnp
import numpy as np
from jax import lax
from jax.experimental import pallas as pl
from jax.experimental.pallas import tpu as pltpu

F32 = jnp.float32
BF16 = jnp.bfloat16

EPS = 1e-6
ROPE_BASE = 10000.0
GRID_W = 64
N_HEADS = 8
HEAD_DIM = 64
V_DIM = 2 * HEAD_DIM
S5_GROUP = 16
S5_STATE = 64
N_MOD = 9
LANES = 128
MXU_WIDTH = 256
CHUNK = 8
SUPER = LANES // S5_GROUP
VMEM_CAPACITY = 64 * 1024 * 1024
VMEM_LIMIT = VMEM_CAPACITY - 8 * 1024 * 1024
LOG2E = math.log2(math.e)
MAX_SCORE_LOG2 = 50.0
MAX_LAMBDA = 2.0 ** 20

FFN_ROWS = 256
PROJ_ROWS = 256
TILE_FFN1 = 1024
TILE_PROJ = 1024
TILE_MERGE = 512
TILE_Q_BOUNDED = 2048
ATTN_Q_GROUP = 512
ATTN_K_BLOCK = 256
TILE_Q_ONLINE, TILE_K_ONLINE = 256, 1024


def _cparams(sem):
    return pltpu.CompilerParams(dimension_semantics=sem, vmem_limit_bytes=VMEM_LIMIT)


def _resident(shape):
    nd = len(shape)
    return pl.BlockSpec(shape, lambda *_: (0,) * nd, pipeline_mode=pl.Buffered(1))


def _silu(a):
    return a * jax.nn.sigmoid(a)


def _modulated(x, g, mod, base):
    ms = jnp.mean(x * x, axis=-1, keepdims=True)
    xn = x * lax.rsqrt(ms + EPS) * g
    return xn * (1.0 + mod[base + 1:base + 2, :]) + mod[base:base + 1, :]


def _mod_kernel(c_ref, w_ref, b_ref, o_ref):
    a = _silu(c_ref[...]).astype(BF16)
    o_ref[...] = jnp.dot(a, w_ref[...].astype(BF16), preferred_element_type=F32) + b_ref[...]


def _mod_call(cc, w_mod, b_mod):
    rows, d = cc.shape
    n = w_mod.shape[1]
    tn = d
    return pl.pallas_call(
        _mod_kernel,
        out_shape=jax.ShapeDtypeStruct((rows, n), F32),
        grid=(n // tn,),
        in_specs=[pl.BlockSpec((rows, d), lambda j: (0, 0)),
                  pl.BlockSpec((d, tn), lambda j: (0, j)),
                  pl.BlockSpec((1, tn), lambda j: (0, j))],
        out_specs=pl.BlockSpec((rows, tn), lambda j: (0, j)),
        compiler_params=_cparams(("arbitrary",)),
        name="mod",
    )(cc, w_mod, b_mod)


def _ffn_rows(x, g, mod, base, w13_ref, w2_ref):
    dff = w2_ref.shape[0]
    xm = _modulated(x, g, mod, base).astype(BF16)
    h = jnp.dot(xm, w13_ref[...], preferred_element_type=F32)
    act = (_silu(h[:, :dff]) * h[:, dff:]).astype(BF16)
    return x + (0.5 * mod[base + 2:base + 3, :]) * jnp.dot(act, w2_ref[...], preferred_element_type=F32)


def _ffn_kernel(x_ref, mod_ref, g_ref, w13_ref, w2_ref, o_ref, *, base):
    for r in range(x_ref.shape[1] // FFN_ROWS):
        rs = slice(r * FFN_ROWS, (r + 1) * FFN_ROWS)
        o_ref[0, rs, :] = _ffn_rows(x_ref[0, rs, :], g_ref[...], mod_ref[0], base, w13_ref, w2_ref)


def _ffn_call(x, mod, g, w13, w2, base, tm, name):
    b, l, d = x.shape
    return pl.pallas_call(
        functools.partial(_ffn_kernel, base=base),
        out_shape=jax.ShapeDtypeStruct(x.shape, F32),
        grid=(b, l // tm),
        in_specs=[pl.BlockSpec((1, tm, d), lambda bi, i: (bi, i, 0)),
                  pl.BlockSpec((1, N_MOD, d), lambda bi, i: (bi, 0, 0)),
                  _resident(g.shape), _resident(w13.shape), _resident(w2.shape)],
        out_specs=pl.BlockSpec((1, tm, d), lambda bi, i: (bi, i, 0)),
        compiler_params=_cparams(("parallel", "parallel")),
        name=name,
    )(x, mod, g, w13, w2)


def _qk_norm(t, gain, ones_ref):
    sq = (t * t).astype(BF16)
    w = ones_ref.shape[0]
    ss = jnp.concatenate([jnp.dot(sq[:, j:j + w], ones_ref[...], preferred_element_type=F32)
                          for j in range(0, t.shape[-1], w)], axis=-1)
    return t * lax.rsqrt(ss * (1.0 / HEAD_DIM) + EPS) * gain


def _rope(t, cos, sin_signed):
    n = t.shape[-1]
    half = HEAD_DIM // 2
    lane = lax.broadcasted_iota(jnp.int32, t.shape, 1)
    swapped = jnp.where(lane % HEAD_DIM < half, pltpu.roll(t, n - half, 1), pltpu.roll(t, half, 1))
    reps = n // cos.shape[-1]
    cosf = jnp.concatenate([cos] * reps, axis=-1)
    sinf = jnp.concatenate([sin_signed] * reps, axis=-1)
    return t * cosf + swapped * sinf


def _store_supergroups(u_ref, rs, u):
    for g in range(u_ref.shape[1]):
        u_ref[0, g, rs, :] = u[:, g * LANES:(g + 1) * LANES]


def _inproj_kernel(x_ref, mod_ref, g_ref, w_ref, qg_ref, kg_ref, ones_ref, cos_ref, sin_ref,
                   q_ref, k_ref, vt_ref, u_ref, ga_ref, gs_ref, *, base):
    d = x_ref.shape[-1]
    qc, kc, vc = d, 2 * d, 3 * d
    uc = vc + u_ref.shape[1] * LANES
    for r in range(x_ref.shape[1] // PROJ_ROWS):
        rs = slice(r * PROJ_ROWS, (r + 1) * PROJ_ROWS)
        xm = _modulated(x_ref[0, rs, :], g_ref[...], mod_ref[0], base).astype(BF16)
        proj = lambda lo, hi: jnp.dot(xm, w_ref[:, lo:hi], preferred_element_type=F32)
        cos = cos_ref[rs, :]
        sin = sin_ref[rs, :]
        tq = proj(0, qc)
        tk = proj(qc, kc)
        q_ref[0, rs, :] = _rope(_qk_norm(tq, qg_ref[...], ones_ref), cos, sin).astype(BF16)
        vt_ref[0, :, rs] = proj(kc, vc).astype(BF16).T
        k_ref[0, rs, :] = _rope(_qk_norm(tk, kg_ref[...], ones_ref), cos, sin).astype(BF16)
        _store_supergroups(u_ref, rs, proj(vc, uc))
        ga_ref[0, rs, :] = jax.nn.sigmoid(proj(uc, uc + d)).astype(BF16)
        gs_ref[0, rs, :] = jax.nn.sigmoid(proj(uc + d, uc + 2 * d)).astype(BF16)


def _inproj_call(x, lc, modx, g, w_in, qg, kg, ones, cos, sin, s5w, base, tm):
    b, l, d = x.shape
    la = l + lc
    tok = lambda bi, i: (bi, i, 0)
    out_shape = (jax.ShapeDtypeStruct((b, l, d), BF16),
                 jax.ShapeDtypeStruct((b, la, d), BF16),
                 jax.ShapeDtypeStruct((b, d, la), BF16),
                 jax.ShapeDtypeStruct((b, s5w // LANES, la, LANES), F32),
                 jax.ShapeDtypeStruct((b, l, d), BF16),
                 jax.ShapeDtypeStruct((b, l, d), BF16))
    return pl.pallas_call(
        functools.partial(_inproj_kernel, base=base),
        out_shape=out_shape,
        grid=(b, l // tm),
        in_specs=[pl.BlockSpec((1, tm, d), tok),
                  pl.BlockSpec((1, N_MOD, d), lambda bi, i: (bi, 0, 0)),
                  _resident(g.shape), _resident(w_in.shape), _resident(qg.shape), _resident(kg.shape),
                  _resident(ones.shape),
                  pl.BlockSpec((tm, LANES), lambda bi, i: (i, 0)),
                  pl.BlockSpec((tm, LANES), lambda bi, i: (i, 0))],
        out_specs=(pl.BlockSpec((1, tm, d), tok),
                   pl.BlockSpec((1, tm, d), tok),
                   pl.BlockSpec((1, d, tm), lambda bi, i: (bi, 0, i)),
                   pl.BlockSpec((1, s5w // LANES, tm, LANES), lambda bi, i: (bi, 0, i, 0)),
                   pl.BlockSpec((1, tm, d), tok),
                   pl.BlockSpec((1, tm, d), tok)),
        compiler_params=_cparams(("parallel", "parallel")),
        name="inproj",
    )(x, modx, g, w_in, qg, kg, ones, cos, sin)


def _ctx_kernel(ctx_ref, mod_ref, g1_ref, w13_ref, w2_ref, gm_ref, w_ref, kg_ref, ones_ref,
                k_in, vt_in, u_in, k_ref, vt_ref, u_ref, *, base):
    del k_in, vt_in, u_in
    d = ctx_ref.shape[-1]
    mod = mod_ref[0]
    x1 = _ffn_rows(ctx_ref[0], g1_ref[...], mod, 0, w13_ref, w2_ref)
    xm = _modulated(x1, gm_ref[...], mod, base).astype(BF16)
    p = jnp.dot(xm, w_ref[...], preferred_element_type=F32)
    k_ref[0] = _qk_norm(p[:, :d], kg_ref[...], ones_ref).astype(BF16)
    vt_ref[0] = p[:, d:2 * d].astype(BF16).T
    _store_supergroups(u_ref, slice(None), p[:, 2 * d:])


def _ctx_call(ctx, modc, g1, w13, w2, gm, w_kvu, kg, ones, k, vt, u, base):
    b, lc, d = ctx.shape
    sg = u.shape[1]
    last = k.shape[1] // lc - 1
    anyspec = pl.BlockSpec(memory_space=pl.ANY)
    return pl.pallas_call(
        functools.partial(_ctx_kernel, base=base),
        out_shape=(jax.ShapeDtypeStruct(k.shape, k.dtype), jax.ShapeDtypeStruct(vt.shape, vt.dtype),
                   jax.ShapeDtypeStruct(u.shape, u.dtype)),
        grid=(b,),
        in_specs=[pl.BlockSpec((1, lc, d), lambda bi: (bi, 0, 0)),
                  pl.BlockSpec((1, N_MOD, d), lambda bi: (0, 0, 0)),
                  _resident(g1.shape), _resident(w13.shape), _resident(w2.shape), _resident(gm.shape),
                  _resident(w_kvu.shape), _resident(kg.shape), _resident(ones.shape),
                  anyspec, anyspec, anyspec],
        out_specs=(pl.BlockSpec((1, lc, d), lambda bi: (bi, last, 0)),
                   pl.BlockSpec((1, d, lc), lambda bi: (bi, 0, last)),
                   pl.BlockSpec((1, sg, lc, LANES), lambda bi: (bi, 0, last, 0))),
        input_output_aliases={9: 0, 10: 1, 11: 2},
        compiler_params=_cparams(("parallel",)),
        name="ctx",
    )(ctx, modc, g1, w13, w2, gm, w_kvu, kg, ones, k, vt, u)


def _attn_kernel(lam_ref, subln_ref, q_ref, k_ref, vt_ref, o_ref, *scratch, lc, tk, lam_init, bounded):
    q = q_ref[0]
    la = k_ref.shape[1]
    lane = lax.broadcasted_iota(jnp.int32, q.shape, 1)
    zero = jnp.zeros_like(q)
    q_comp = (jnp.where(lane < HEAD_DIM, q, zero), jnp.where(lane >= HEAD_DIM, q, zero))
    lv = lam_ref[...]
    lam = (jnp.exp(jnp.sum(lv[0:1] * lv[1:2], axis=-1, keepdims=True))
           - jnp.exp(jnp.sum(lv[2:3] * lv[3:4], axis=-1, keepdims=True)) + lam_init)
    blocks = [] if bounded else [(st, tk) for st in range(0, la - lc, tk)] + [(la - lc, lc)]
    nt_dims = (((1,), (1,)), ((), ()))
    heads = []
    if bounded:
        (p_ref,) = scratch
        parts = []
        for h0 in range(0, q.shape[0], ATTN_Q_GROUP):
            ls = []
            for ci, qz in enumerate(q_comp):
                qg = qz[h0:h0 + ATTN_Q_GROUP]
                l = None
                for st in range(0, la, ATTN_K_BLOCK):
                    s = lax.dot_general(k_ref[0, st:st + ATTN_K_BLOCK, :], qg, nt_dims,
                                        preferred_element_type=F32)
                    p = jnp.exp2(s).astype(BF16)
                    lb = jnp.sum(p.astype(F32), axis=0, keepdims=True)
                    l = lb if l is None else l + lb
                    p_ref[ci, st:st + ATTN_K_BLOCK, :] = p
                ls.append(l)
            inv_l1 = 1.0 / ls[0]
            c = (lam * ls[0] * (1.0 / ls[1])).astype(BF16)
            a = p_ref[0] - c * p_ref[1]
            parts.append(jnp.dot(vt_ref[0], a, preferred_element_type=F32) * inv_l1)
        o = jnp.concatenate(parts, axis=-1)
        q_comp = ()
    for qz in q_comp:
        m = l = acc = None
        for st, sz in blocks:
            kb = k_ref[0, st:st + sz, :]
            vb = vt_ref[0, :, st:st + sz]
            s = lax.dot_general(kb, qz, nt_dims, preferred_element_type=F32)
            bm = jnp.max(s, axis=0, keepdims=True)
            if m is None:
                m_new = bm
                p = jnp.exp2(s - m_new)
                l = jnp.sum(p, axis=0, keepdims=True)
                acc = jnp.dot(vb, p.astype(BF16), preferred_element_type=F32)
            else:
                m_new = jnp.maximum(m, bm)
                alpha = jnp.exp2(m - m_new)
                p = jnp.exp2(s - m_new)
                l = alpha * l + jnp.sum(p, axis=0, keepdims=True)
                acc = alpha * acc + jnp.dot(vb, p.astype(BF16), preferred_element_type=F32)
            m = m_new
        heads.append(acc * (1.0 / l))
    if not bounded:
        o = heads[0] - lam * heads[1]
    ms = jnp.mean(o * o, axis=0, keepdims=True)
    on = o * lax.rsqrt(ms + EPS) * (subln_ref[...] * (1.0 - lam_init))
    o_ref[0] = on.astype(BF16)


def _attn_call(lamv, subln, q, k, vt, lc, lam_init, tq, tk, bounded):
    b, l, d = q.shape
    la = k.shape[1]
    return pl.pallas_call(
        functools.partial(_attn_kernel, lc=lc, tk=tk, lam_init=lam_init, bounded=bounded),
        out_shape=jax.ShapeDtypeStruct((b, d, l), BF16),
        grid=(b, N_HEADS, l // tq),
        in_specs=[pl.BlockSpec(lamv.shape, lambda bi, h, i: (0, 0)),
                  pl.BlockSpec(subln.shape, lambda bi, h, i: (0, 0)),
                  pl.BlockSpec((1, tq, V_DIM), lambda bi, h, i: (bi, i, h)),
                  pl.BlockSpec((1, la, V_DIM), lambda bi, h, i: (bi, 0, h)),
                  pl.BlockSpec((1, V_DIM, la), lambda bi, h, i: (bi, h, 0))],
        out_specs=pl.BlockSpec((1, V_DIM, tq), lambda bi, h, i: (bi, h, i)),
        scratch_shapes=[pltpu.VMEM((2, la, ATTN_Q_GROUP), BF16)] if bounded else [],
        compiler_params=_cparams(("parallel", "parallel", "arbitrary")),
        name="attn_bounded" if bounded else "attn_online",
    )(lamv, subln, q, k, vt)


def _s5_matrices(lam_re, lam_im, log_dt, b_re, b_im, c_re, c_im, d_skip, seg):
    t = CHUNK
    g_all, p_n = lam_re.shape[1:]
    c_n = b_re.shape[-1]
    sg = g_all // SUPER
    ks = jnp.arange(t + 1, dtype=F32)[:, None, None]
    mats = []
    for dr in range(2):
        lre, lim = lam_re[dr], lam_im[dr]
        dt = jnp.exp(log_dt[dr])[:, None]
        mag = jnp.exp(ks * (lre * dt))
        akr = mag * jnp.cos(ks * (lim * dt))
        aki = mag * jnp.sin(ks * (lim * dt))
        ar, ai = akr[1], aki[1]
        den = lre * lre + lim * lim
        nr = ar - 1.0
        fr = (nr * lre + ai * lim) / den
        fi = (ai * lre - nr * lim) / den
        bbr = fr[..., None] * b_re[dr] - fi[..., None] * b_im[dr]
        bbi = fr[..., None] * b_im[dr] + fi[..., None] * b_re[dr]
        abr = akr[..., None] * bbr - aki[..., None] * bbi
        abi = akr[..., None] * bbi + aki[..., None] * bbr
        kern = (jnp.einsum('gdp,kgpc->kgdc', c_re[dr], abr)
                - jnp.einsum('gdp,kgpc->kgdc', c_im[dr], abi))
        car = c_re[dr][None] * akr[:, :, None, :] - c_im[dr][None] * aki[:, :, None, :]
        cai = c_re[dr][None] * aki[:, :, None, :] + c_im[dr][None] * akr[:, :, None, :]
        mats.append((akr, aki, abr, abi, kern, car, cai))

    s_idx = jnp.arange(t)
    kf, kb = mats[0][4], mats[1][4]
    d_diag = jnp.eye(c_n, dtype=F32)[None] * d_skip.reshape(g_all, 1, c_n)
    lagk = jnp.concatenate([kb[1:t][::-1], (kf[0] + kb[0] + d_diag)[None], kf[1:t]], axis=0)
    lagk = lagk.transpose(1, 3, 0, 2).reshape(g_all, c_n, (2 * t - 1) * c_n)
    li, ci = np.divmod(np.arange((2 * t - 1) * c_n), c_n)
    ti, cj = np.divmod(np.arange(t * c_n), c_n)
    pick = ((li[None, :, None] == ti[None, None, :] - np.arange(t)[:, None, None] + t - 1)
            & (ci[None, :, None] == cj[None, None, :]))
    m = jnp.einsum('gcx,sxy->gscy', lagk, jnp.asarray(pick, F32))
    m = m.reshape(sg, SUPER, t, c_n, t * c_n).transpose(0, 2, 1, 3, 4)
    m = m.reshape(sg, t * SUPER * c_n, t * c_n)

    def state_in(abr, abi, order):
        wr = abr[order].reshape(t, sg, SUPER, p_n, c_n)
        wi = abi[order].reshape(t, sg, SUPER, p_n, c_n)
        w = jnp.stack([wr, wi], axis=0)
        return w.transpose(2, 1, 3, 5, 0, 4).reshape(sg, t * SUPER * c_n, 2 * p_n)

    def state_out(car, cai, order):
        vr = car[order].reshape(t, sg, SUPER, c_n, p_n)
        vi = -cai[order].reshape(t, sg, SUPER, c_n, p_n)
        v = jnp.stack([vr, vi], axis=0)
        return v.transpose(2, 0, 3, 5, 1, 4).reshape(sg, 2 * SUPER * p_n, t * c_n)

    wf = state_in(mats[0][2], mats[0][3], (t - 1) - s_idx)
    wb = state_in(mats[1][2], mats[1][3], s_idx)
    vf = state_out(mats[0][5], mats[0][6], s_idx + 1)
    vb = state_out(mats[1][5], mats[1][6], t - s_idx)

    def decay_table(dr, powers):
        lre, lim = lam_re[dr], lam_im[dr]
        dt = jnp.exp(log_dt[dr])[:, None]
        kk = (t * powers).astype(F32)[:, None, None]
        mag = jnp.exp(kk * (lre * dt))
        tab = jnp.stack([mag * jnp.cos(kk * (lim * dt)), mag * jnp.sin(kk * (lim * dt))], axis=0)
        return tab.reshape(2, -1, sg, SUPER * p_n).transpose(2, 0, 1, 3)

    j = jnp.arange(seg)
    bf = lambda a: a.astype(BF16)
    return (bf(m), bf(wf), bf(wb), bf(vf), bf(vb), decay_table(0, j + 1), decay_table(1, seg - j))


def _s5_segment(n):
    return 8 * (-(-n // 64))


def _cmul_add(ar, ai, xr, xi, br, bi):
    return ar * xr - ai * xi + br, ar * xi + ai * xr + bi


def _group_replicator(outer, inner):
    col = np.arange(outer * SUPER * inner)
    src = (col // (SUPER * inner)) * inner + col % inner
    return jnp.asarray(np.arange(outer * inner)[:, None] == src[None, :], dtype=BF16)


def _expand_groups(xc, rep, row_block, col_block):
    dense = jnp.dot(xc, rep, preferred_element_type=F32)
    r = lax.broadcasted_iota(jnp.int32, dense.shape, 0)
    c = lax.broadcasted_iota(jnp.int32, dense.shape, 1)
    own = (r // row_block) % SUPER == (c // col_block) % SUPER
    return jnp.where(own, dense, 0.0).astype(BF16)


def _s5_kernel(u_ref, mc_ref, wfc_ref, wbc_ref, vfc_ref, vbc_ref, rep_out_ref, rep_state_ref, pf_ref, pb_ref,
               y_ref, m_ref, wf_ref, wb_ref, vf_ref, vb_ref, sf_ref, sb_ref, *, nx, nc):
    @pl.when(pl.program_id(1) == 0)
    def _build_matrices():
        rep_out, rep_state = rep_out_ref[...], rep_state_ref[...]
        m_ref[...] = _expand_groups(mc_ref[0], rep_out, S5_GROUP, S5_GROUP)
        wf_ref[...] = _expand_groups(wfc_ref[0], rep_state, S5_GROUP, S5_STATE)
        wb_ref[...] = _expand_groups(wbc_ref[0], rep_state, S5_GROUP, S5_STATE)
        vf_ref[...] = _expand_groups(vfc_ref[0], rep_out, S5_STATE, S5_GROUP)
        vb_ref[...] = _expand_groups(vbc_ref[0], rep_out, S5_STATE, S5_GROUP)

    n = nx + nc
    tiles, npad = sf_ref.shape[1:3]
    seg = npad // 8
    pad = npad - n
    half = tiles * LANES

    def load(ref, rows):
        return tuple(jnp.concatenate([ref[ri, c, rows, :] for c in range(tiles)], axis=-1) for ri in range(2))

    def store(ref, rows, re, im):
        for c in range(tiles):
            ref[0, c, rows, :] = re[:, c * LANES:(c + 1) * LANES]
            ref[1, c, rows, :] = im[:, c * LANES:(c + 1) * LANES]

    z = jnp.concatenate([u_ref[0, 0, pl.ds(t, n, stride=CHUNK), :] for t in range(CHUNK)], axis=-1).astype(BF16)
    zl, zc = z[:nx], z[nx:]
    zpad = [jnp.zeros((pad, z.shape[1]), BF16)] if pad else []
    s = jnp.dot(jnp.concatenate([zc, zl] + zpad, axis=0), wf_ref[...], preferred_element_type=F32)
    store(sf_ref, slice(None), s[:, :half], s[:, half:])
    s = jnp.dot(jnp.concatenate(zpad + [zl, zc], axis=0), wb_ref[...], preferred_element_type=F32)
    store(sb_ref, slice(None), s[:, :half], s[:, half:])

    afr, afi = pf_ref[0, 0, 0:1, :], pf_ref[0, 1, 0:1, :]
    abr, abi = pb_ref[0, 0, seg - 1:seg, :], pb_ref[0, 1, seg - 1:seg, :]

    def step(i, carry):
        hr, hi, gr, gi = carry
        rows = pl.ds(i, 8, stride=seg)
        hr, hi = _cmul_add(afr, afi, hr, hi, *load(sf_ref, rows))
        store(sf_ref, rows, hr, hi)
        rows = pl.ds(seg - 1 - i, 8, stride=seg)
        gr, gi = _cmul_add(abr, abi, gr, gi, *load(sb_ref, rows))
        store(sb_ref, rows, gr, gi)
        return hr, hi, gr, gi

    zero = jnp.zeros((8, half), F32)
    hr, hi, gr, gi = lax.fori_loop(0, seg, step, (zero, zero, zero, zero), unroll=True)

    tfr, tfi = pf_ref[0, 0], pf_ref[0, 1]
    tbr, tbi = pb_ref[0, 0], pb_ref[0, 1]
    cr = ci = None
    for s in range(1, 8):
        er, ei = hr[s - 1:s], hi[s - 1:s]
        if cr is None:
            cr, ci = er, ei
        else:
            cr, ci = _cmul_add(tfr[seg - 1:seg], tfi[seg - 1:seg], cr, ci, er, ei)
        rows = slice(s * seg, (s + 1) * seg)
        store(sf_ref, rows, *_cmul_add(tfr, tfi, cr, ci, *load(sf_ref, rows)))
    cr = ci = None
    for s in range(6, -1, -1):
        er, ei = gr[s + 1:s + 2], gi[s + 1:s + 2]
        if cr is None:
            cr, ci = er, ei
        else:
            cr, ci = _cmul_add(tbr[0:1], tbi[0:1], cr, ci, er, ei)
        rows = slice(s * seg, (s + 1) * seg)
        store(sb_ref, rows, *_cmul_add(tbr, tbi, cr, ci, *load(sb_ref, rows)))

    def entering(ref, shift, first):
        cols = [pltpu.roll(ref[ri, c], shift, 0)[first:first + nx] for ri in range(2) for c in range(tiles)]
        return jnp.concatenate(cols, axis=-1).astype(BF16)

    y = (jnp.dot(zl, m_ref[...], preferred_element_type=F32)
         + jnp.dot(entering(sf_ref, 1, nc), vf_ref[...], preferred_element_type=F32)
         + jnp.dot(entering(sb_ref, npad - 1, pad), vb_ref[...], preferred_element_type=F32))
    for t in range(CHUNK):
        y_ref[0, 0, pl.ds(t, nx, stride=CHUNK), :] = y[:, t * LANES:(t + 1) * LANES]


def _s5_call(u, mats, lc):
    b, sg, la, _ = u.shape
    t = CHUNK
    n, nc = la // t, lc // t
    nx = n - nc
    npad = 8 * _s5_segment(n)
    width = t * LANES
    tiles = width // (2 * LANES)
    mc, wfc, wbc, vfc, vbc, pf, pb = mats
    rep_out = _group_replicator(t, S5_GROUP)
    rep_state = _group_replicator(2, S5_STATE)
    wspec = lambda a: pl.BlockSpec((1,) + a.shape[1:], lambda g, bi: (g,) + (0,) * (a.ndim - 1))
    dense = pltpu.VMEM((width, width), BF16)
    states = pltpu.VMEM((2, tiles, npad, LANES), F32)
    return pl.pallas_call(
        functools.partial(_s5_kernel, nx=nx, nc=nc),
        out_shape=jax.ShapeDtypeStruct((b, sg, nx * t, LANES), F32),
        grid=(sg, b),
        in_specs=[pl.BlockSpec((1, 1, la, LANES), lambda g, bi: (bi, g, 0, 0))]
                 + [wspec(a) for a in (mc, wfc, wbc, vfc, vbc)]
                 + [_resident(rep_out.shape), _resident(rep_state.shape), wspec(pf), wspec(pb)],
        out_specs=pl.BlockSpec((1, 1, nx * t, LANES), lambda g, bi: (bi, g, 0, 0)),
        scratch_shapes=[dense] * 5 + [states] * 2,
        compiler_params=_cparams(("arbitrary", "arbitrary")),
        name="s5",
    )(u, mc, wfc, wbc, vfc, vbc, rep_out, rep_state, pf, pb)


def _gelu_tanh(y):
    return 0.5 * y * (1.0 + jnp.tanh(math.sqrt(2.0 / math.pi) * (y + 0.044715 * (y * y * y))))


def _merge_ffn_kernel(x_ref, mod_ref, y_ref, oxt_ref, ga_ref, gs_ref, wglu_ref, bglu_ref, wpa_ref, wps_ref,
                      wout_ref, g2_ref, w13_ref, w2_ref, o_ref, *, base):
    mod = mod_ref[0]
    for r in range(x_ref.shape[1] // FFN_ROWS):
        rs = slice(r * FFN_ROWS, (r + 1) * FFN_ROWS)
        ge =_gelu_tanh(jnp.concatenate([y_ref[0, g, rs, :] for g in range(y_ref.shape[1])], axis=-1))
        z = jnp.dot(ge.astype(BF16), wglu_ref[...], preferred_element_type=F32) + bglu_ref[...]
        sx = (ge * jax.nn.sigmoid(z)).astype(BF16)
        attn = lax.dot_general(oxt_ref[0, :, rs], wpa_ref[...], (((0,), (0,)), ((), ())),
                               preferred_element_type=F32)
        mix = (ga_ref[0, rs, :].astype(F32) * attn
               + gs_ref[0, rs, :].astype(F32) * jnp.dot(sx, wps_ref[...], preferred_element_type=F32))
        out = jnp.dot(mix.astype(BF16), wout_ref[...], preferred_element_type=F32)
        x2 = x_ref[0, rs, :] + mod[base + 2:base + 3, :] * out
        o_ref[0, rs, :] = _ffn_rows(x2, g2_ref[...], mod, base + 3, w13_ref, w2_ref)


def _merge_ffn_call(x, mod, y, oxt, ga, gs, w_glu, b_glu, w_pa, w_ps, w_out, g2, w13, w2, base, tm):
    b, l, d = x.shape
    sg = y.shape[1]
    tok = lambda width: pl.BlockSpec((1, tm, width), lambda bi, i: (bi, i, 0))
    weights = (w_glu, b_glu, w_pa, w_ps, w_out, g2, w13, w2)
    return pl.pallas_call(
        functools.partial(_merge_ffn_kernel, base=base),
        out_shape=jax.ShapeDtypeStruct(x.shape, F32),
        grid=(b, l // tm),
        in_specs=[tok(d), pl.BlockSpec((1, N_MOD, d), lambda bi, i: (bi, 0, 0)),
                  pl.BlockSpec((1, sg, tm, LANES), lambda bi, i: (bi, 0, i, 0)),
                  pl.BlockSpec((1, d, tm), lambda bi, i: (bi, 0, i)), tok(d), tok(d)]
                 + [_resident(w.shape) for w in weights],
        out_specs=tok(d),
        compiler_params=_cparams(("parallel", "parallel")),
        name="merge_ffn2",
    )(x, mod, y, oxt, ga, gs, *weights)


def _rope_tables(l):
    n_freq = HEAD_DIM // 4
    inv = ROPE_BASE ** (-jnp.arange(n_freq, dtype=F32) / n_freq)
    pos = jnp.arange(l, dtype=jnp.int32)
    row = (pos // GRID_W).astype(F32)
    col = (pos % GRID_W).astype(F32)
    ang = jnp.concatenate([row[:, None] * inv, col[:, None] * inv], axis=-1)
    cos, sin = jnp.cos(ang), jnp.sin(ang)
    return jnp.concatenate([cos] * 4, axis=-1), jnp.concatenate([-sin, sin, -sin, sin], axis=-1)


def _group_ones():
    grp = np.arange(MXU_WIDTH) // HEAD_DIM
    return jnp.asarray(grp[:, None] == grp[None, :], dtype=BF16)


def kernel(x, c, ctx, c_ctx, w_mod, b_mod, norm_ffn1, w13_ffn1, w2_ffn1, norm_mix, w_in, q_norm, k_norm, lam_q1, lam_k1, lam_q2, lam_k2, subln, s5_lam_re, s5_lam_im, s5_log_dt, s5_b_re, s5_b_im, s5_c_re, s5_c_im, s5_d, w_glu, b_glu, w_pa, w_ps, w_out, norm_ffn2, w13_ffn2, w2_ffn2):
    assert w_mod.shape[0] == 1, "single-layer block"
    b, l, d = x.shape
    lc = ctx.shape[1]
    assert l % max(TILE_FFN1, TILE_PROJ, TILE_MERGE, TILE_Q_BOUNDED, TILE_K_ONLINE) == 0 and l % lc == 0
    assert lc % (2 * CHUNK) == 0 and d == N_HEADS * V_DIM
    lam_init = 0.8 - 0.6 * math.exp(-0.3 * 0)
    bf = lambda a: a.astype(BF16)
    row = lambda a: a.reshape(1, -1)

    rows = b + 1
    pad = (-rows) % 8
    cc = jnp.concatenate([c, c_ctx[None, :], jnp.zeros((pad, d), F32)], axis=0)
    mod = _mod_call(cc, w_mod[0], row(b_mod[0]))
    modx = mod[:b].reshape(b, N_MOD, d)
    modc = mod[b:b + 1].reshape(1, N_MOD, d)

    w13_1, w2_1 = bf(w13_ffn1[0]), bf(w2_ffn1[0])
    x1 = _ffn_call(x, modx, row(norm_ffn1[0]), w13_1, w2_1, 0, TILE_FFN1, "ffn1_x")

    ones = _group_ones()
    cos, sin = _rope_tables(l)
    qg = row(jnp.tile(q_norm[0], d // HEAD_DIM)) * (LOG2E / math.sqrt(HEAD_DIM))
    kg = row(jnp.tile(k_norm[0], d // HEAD_DIM))
    w_inb = bf(w_in[0])
    s5w = s5_d.shape[-1]
    q, k, vt, u, ga, gs = _inproj_call(x1, lc, modx, row(norm_mix[0]), w_inb, qg, kg, ones,
                                       cos, sin, s5w, 3, TILE_PROJ)
    k, vt, u = _ctx_call(ctx, modc, row(norm_ffn1[0]), w13_1, w2_1, row(norm_mix[0]),
                         w_inb[:, d:3 * d + s5w], kg, ones, k, vt, u, 3)

    lamv = jnp.stack([lam_q1[0], lam_k1[0], lam_q2[0], lam_k2[0]], axis=0)
    bound = (LOG2E * math.sqrt(HEAD_DIM)) * jnp.max(jnp.abs(q_norm[0])) * jnp.max(jnp.abs(k_norm[0]))
    lam = jnp.exp(jnp.sum(lam_q1[0] * lam_k1[0])) - jnp.exp(jnp.sum(lam_q2[0] * lam_k2[0])) + lam_init
    attn = lambda bounded, tq, tk: functools.partial(
        _attn_call, lamv, subln[0].reshape(-1, 1), q, k, vt, lc, lam_init, tq, tk, bounded)
    oxt = lax.cond((bound <= MAX_SCORE_LOG2) & (jnp.abs(lam) <= MAX_LAMBDA),
                   attn(True, TILE_Q_BOUNDED, None), attn(False, TILE_Q_ONLINE, TILE_K_ONLINE))

    mats = _s5_matrices(s5_lam_re[0], s5_lam_im[0], s5_log_dt[0], s5_b_re[0], s5_b_im[0],
                        s5_c_re[0], s5_c_im[0], s5_d[0], _s5_segment((l + lc) // CHUNK))
    y = _s5_call(u, mats, lc)

    return _merge_ffn_call(x1, modx, y, oxt, ga, gs, bf(w_glu[0]), row(b_glu[0]), bf(w_pa[0]), bf(w_ps[0]),
                           bf(w_out[0]), row(norm_ffn2[0]), bf(w13_ffn2[0]), bf(w2_ffn2[0]), 3, TILE_MERGE)
```

```python
import functools
import math

import jax
import jax.numpy as jnp
import numpy as np
from jax import lax
from jax.experimental import pallas as pl
from jax.experimental.pallas import tpu as pltpu

F32 = jnp.float32
BF16 = jnp.bfloat16

EPS = 1e-6
ROPE_BASE = 10000.0
GRID_W = 64
N_HEADS = 8
HEAD_DIM = 64
V_DIM = 2 * HEAD_DIM
S5_GROUP = 16
S5_STATE = 64
N_MOD = 9
LANES = 128
MXU_WIDTH = 256
CHUNK = 8
SUPER = LANES // S5_GROUP
VMEM_CAPACITY = 64 * 1024 * 1024
VMEM_LIMIT = VMEM_CAPACITY - 8 * 1024 * 1024
LOG2E = math.log2(math.e)
MAX_EXP2_SPAN = 100.0

FFN_ROWS = 256
PROJ_ROWS = 256
TILE_FFN1 = 1024
TILE_PROJ = 1024
TILE_MERGE = 512
TILE_Q_BOUNDED = 1024
TILE_Q_ONLINE, TILE_K_ONLINE = 256, 1024


def _cparams(sem):
    return pltpu.CompilerParams(dimension_semantics=sem, vmem_limit_bytes=VMEM_LIMIT)


def _resident(shape):
    nd = len(shape)
    return pl.BlockSpec(shape, lambda *_: (0,) * nd, pipeline_mode=pl.Buffered(1))


def _silu(a):
    return a * jax.nn.sigmoid(a)


def _modulated(x, g, mod, base):
    ms = jnp.mean(x * x, axis=-1, keepdims=True)
    xn = x * lax.rsqrt(ms + EPS) * g
    return xn * (1.0 + mod[base + 1:base + 2, :]) + mod[base:base + 1, :]


def _mod_kernel(c_ref, w_ref, b_ref, o_ref):
    a = _silu(c_ref[...]).astype(BF16)
    o_ref[...] = jnp.dot(a, w_ref[...].astype(BF16), preferred_element_type=F32) + b_ref[...]


def _mod_call(cc, w_mod, b_mod):
    rows, d = cc.shape
    n = w_mod.shape[1]
    tn = d
    return pl.pallas_call(
        _mod_kernel,
        out_shape=jax.ShapeDtypeStruct((rows, n), F32),
        grid=(n // tn,),
        in_specs=[pl.BlockSpec((rows, d), lambda j: (0, 0)),
                  pl.BlockSpec((d, tn), lambda j: (0, j)),
                  pl.BlockSpec((1, tn), lambda j: (0, j))],
        out_specs=pl.BlockSpec((rows, tn), lambda j: (0, j)),
        compiler_params=_cparams(("arbitrary",)),
        name="mod",
    )(cc, w_mod, b_mod)


def _ffn_rows(x, g, mod, base, w13_ref, w2_ref):
    dff = w2_ref.shape[0]
    xm = _modulated(x, g, mod, base).astype(BF16)
    h = jnp.dot(xm, w13_ref[...], preferred_element_type=F32)
    act = (_silu(h[:, :dff]) * h[:, dff:]).astype(BF16)
    return x + (0.5 * mod[base + 2:base + 3, :]) * jnp.dot(act, w2_ref[...], preferred_element_type=F32)


def _ffn_kernel(x_ref, mod_ref, g_ref, w13_ref, w2_ref, o_ref, *, base):
    for r in range(x_ref.shape[1] // FFN_ROWS):
        rs = slice(r * FFN_ROWS, (r + 1) * FFN_ROWS)
        o_ref[0, rs, :] = _ffn_rows(x_ref[0, rs, :], g_ref[...], mod_ref[0], base, w13_ref, w2_ref)


def _ffn_call(x, mod, g, w13, w2, base, tm, name):
    b, l, d = x.shape
    return pl.pallas_call(
        functools.partial(_ffn_kernel, base=base),
        out_shape=jax.ShapeDtypeStruct(x.shape, F32),
        grid=(b, l // tm),
        in_specs=[pl.BlockSpec((1, tm, d), lambda bi, i: (bi, i, 0)),
                  pl.BlockSpec((1, N_MOD, d), lambda bi, i: (bi, 0, 0)),
                  _resident(g.shape), _resident(w13.shape), _resident(w2.shape)],
        out_specs=pl.BlockSpec((1, tm, d), lambda bi, i: (bi, i, 0)),
        compiler_params=_cparams(("parallel", "parallel")),
        name=name,
    )(x, mod, g, w13, w2)


def _qk_norm(t, gain, ones_ref):
    sq = (t * t).astype(BF16)
    w = ones_ref.shape[0]
    ss = jnp.concatenate([jnp.dot(sq[:, j:j + w], ones_ref[...], preferred_element_type=F32)
                          for j in range(0, t.shape[-1], w)], axis=-1)
    return t * lax.rsqrt(ss * (1.0 / HEAD_DIM) + EPS) * gain


def _rope(t, cos, sin_signed):
    n = t.shape[-1]
    half = HEAD_DIM // 2
    lane = lax.broadcasted_iota(jnp.int32, t.shape, 1)
    swapped = jnp.where(lane % HEAD_DIM < half, pltpu.roll(t, n - half, 1), pltpu.roll(t, half, 1))
    reps = n // cos.shape[-1]
    cosf = jnp.concatenate([cos] * reps, axis=-1)
    sinf = jnp.concatenate([sin_signed] * reps, axis=-1)
    return t * cosf + swapped * sinf


def _store_supergroups(u_ref, rs, u):
    for g in range(u_ref.shape[1]):
        u_ref[0, g, rs, :] = u[:, g * LANES:(g + 1) * LANES]


def _inproj_kernel(x_ref, mod_ref, g_ref, w_ref, qg_ref, kg_ref, ones_ref, cos_ref, sin_ref,
                   q_ref, k_ref, vt_ref, u_ref, ga_ref, gs_ref, *, base):
    d = x_ref.shape[-1]
    qc, kc, vc = d, 2 * d, 3 * d
    uc = vc + u_ref.shape[1] * LANES
    for r in range(x_ref.shape[1] // PROJ_ROWS):
        rs = slice(r * PROJ_ROWS, (r + 1) * PROJ_ROWS)
        xm = _modulated(x_ref[0, rs, :], g_ref[...], mod_ref[0], base).astype(BF16)
        proj = lambda lo, hi: jnp.dot(xm, w_ref[:, lo:hi], preferred_element_type=F32)
        cos = cos_ref[rs, :]
        sin = sin_ref[rs, :]
        tq = proj(0, qc)
        tk = proj(qc, kc)
        q_ref[0, rs, :] = _rope(_qk_norm(tq, qg_ref[...], ones_ref), cos, sin).astype(BF16)
        vt_ref[0, :, rs] = proj(kc, vc).astype(BF16).T
        k_ref[0, rs, :] = _rope(_qk_norm(tk, kg_ref[...], ones_ref), cos, sin).astype(BF16)
        _store_supergroups(u_ref, rs, proj(vc, uc))
        ga_ref[0, rs, :] = jax.nn.sigmoid(proj(uc, uc + d)).astype(BF16)
        gs_ref[0, rs, :] = jax.nn.sigmoid(proj(uc + d, uc + 2 * d)).astype(BF16)


def _inproj_call(x, lc, modx, g, w_in, qg, kg, ones, cos, sin, s5w, base, tm):
    b, l, d = x.shape
    la = l + lc
    tok = lambda bi, i: (bi, i, 0)
    out_shape = (jax.ShapeDtypeStruct((b, l, d), BF16),
                 jax.ShapeDtypeStruct((b, la, d), BF16),
                 jax.ShapeDtypeStruct((b, d, la), BF16),
                 jax.ShapeDtypeStruct((b, s5w // LANES, la, LANES), F32),
                 jax.ShapeDtypeStruct((b, l, d), BF16),
                 jax.ShapeDtypeStruct((b, l, d), BF16))
    return pl.pallas_call(
        functools.partial(_inproj_kernel, base=base),
        out_shape=out_shape,
        grid=(b, l // tm),
        in_specs=[pl.BlockSpec((1, tm, d), tok),
                  pl.BlockSpec((1, N_MOD, d), lambda bi, i: (bi, 0, 0)),
                  _resident(g.shape), _resident(w_in.shape), _resident(qg.shape), _resident(kg.shape),
                  _resident(ones.shape),
                  pl.BlockSpec((tm, LANES), lambda bi, i: (i, 0)),
                  pl.BlockSpec((tm, LANES), lambda bi, i: (i, 0))],
        out_specs=(pl.BlockSpec((1, tm, d), tok),
                   pl.BlockSpec((1, tm, d), tok),
                   pl.BlockSpec((1, d, tm), lambda bi, i: (bi, 0, i)),
                   pl.BlockSpec((1, s5w // LANES, tm, LANES), lambda bi, i: (bi, 0, i, 0)),
                   pl.BlockSpec((1, tm, d), tok),
                   pl.BlockSpec((1, tm, d), tok)),
        compiler_params=_cparams(("parallel", "parallel")),
        name="inproj",
    )(x, modx, g, w_in, qg, kg, ones, cos, sin)


def _ctx_kernel(ctx_ref, mod_ref, g1_ref, w13_ref, w2_ref, gm_ref, w_ref, kg_ref, ones_ref,
                k_in, vt_in, u_in, k_ref, vt_ref, u_ref, *, base):
    del k_in, vt_in, u_in
    d = ctx_ref.shape[-1]
    mod = mod_ref[0]
    x1 = _ffn_rows(ctx_ref[0], g1_ref[...], mod, 0, w13_ref, w2_ref)
    xm = _modulated(x1, gm_ref[...], mod, base).astype(BF16)
    p = jnp.dot(xm, w_ref[...], preferred_element_type=F32)
    k_ref[0] = _qk_norm(p[:, :d], kg_ref[...], ones_ref).astype(BF16)
    vt_ref[0] = p[:, d:2 * d].astype(BF16).T
    _store_supergroups(u_ref, slice(None), p[:, 2 * d:])


def _ctx_call(ctx, modc, g1, w13, w2, gm, w_kvu, kg, ones, k, vt, u, base):
    b, lc, d = ctx.shape
    sg = u.shape[1]
    last = k.shape[1] // lc - 1
    anyspec = pl.BlockSpec(memory_space=pl.ANY)
    return pl.pallas_call(
        functools.partial(_ctx_kernel, base=base),
        out_shape=(jax.ShapeDtypeStruct(k.shape, k.dtype), jax.ShapeDtypeStruct(vt.shape, vt.dtype),
                   jax.ShapeDtypeStruct(u.shape, u.dtype)),
        grid=(b,),
        in_specs=[pl.BlockSpec((1, lc, d), lambda bi: (bi, 0, 0)),
                  pl.BlockSpec((1, N_MOD, d), lambda bi: (0, 0, 0)),
                  _resident(g1.shape), _resident(w13.shape), _resident(w2.shape), _resident(gm.shape),
                  _resident(w_kvu.shape), _resident(kg.shape), _resident(ones.shape),
                  anyspec, anyspec, anyspec],
        out_specs=(pl.BlockSpec((1, lc, d), lambda bi: (bi, last, 0)),
                   pl.BlockSpec((1, d, lc), lambda bi: (bi, 0, last)),
                   pl.BlockSpec((1, sg, lc, LANES), lambda bi: (bi, 0, last, 0))),
        input_output_aliases={9: 0, 10: 1, 11: 2},
        compiler_params=_cparams(("parallel",)),
        name="ctx",
    )(ctx, modc, g1, w13, w2, gm, w_kvu, kg, ones, k, vt, u)


def _attn_kernel(bound_ref, lam_ref, subln_ref, q_ref, k_ref, vt_ref, o_ref, *, lc, tk, lam_init, bounded):
    q = q_ref[0]
    la = k_ref.shape[1]
    lane = lax.broadcasted_iota(jnp.int32, q.shape, 1)
    zero = jnp.zeros_like(q)
    q_comp = (jnp.where(lane < HEAD_DIM, q, zero), jnp.where(lane >= HEAD_DIM, q, zero))
    lv = lam_ref[...]
    lam = (jnp.exp(jnp.sum(lv[0:1] * lv[1:2], axis=-1, keepdims=True))
           - jnp.exp(jnp.sum(lv[2:3] * lv[3:4], axis=-1, keepdims=True)) + lam_init)
    blocks = [(0, la)] if bounded else [(st, tk) for st in range(0, la - lc, tk)] + [(la - lc, lc)]
    nt_dims = (((1,), (1,)), ((), ()))
    heads = []
    for qz in q_comp:
        m = l = acc = None
        for st, sz in blocks:
            kb = k_ref[0, st:st + sz, :]
            vb = vt_ref[0, :, st:st + sz]
            s = lax.dot_general(kb, qz, nt_dims, preferred_element_type=F32)
            if bounded:
                p = jnp.exp2(s - bound_ref[...])
                l = jnp.sum(p, axis=0, keepdims=True)
                acc = jnp.dot(vb, p.astype(BF16), preferred_element_type=F32)
                continue
            bm = jnp.max(s, axis=0, keepdims=True)
            if m is None:
                m_new = bm
                p = jnp.exp2(s - m_new)
                l = jnp.sum(p, axis=0, keepdims=True)
                acc = jnp.dot(vb, p.astype(BF16), preferred_element_type=F32)
            else:
                m_new = jnp.maximum(m, bm)
                alpha = jnp.exp2(m - m_new)
                p = jnp.exp2(s - m_new)
                l = alpha * l + jnp.sum(p, axis=0, keepdims=True)
                acc = alpha * acc + jnp.dot(vb, p.astype(BF16), preferred_element_type=F32)
            m = m_new
        heads.append(acc * (1.0 / l))
    o = heads[0] - lam * heads[1]
    ms = jnp.mean(o * o, axis=0, keepdims=True)
    on = o * lax.rsqrt(ms + EPS) * (subln_ref[...] * (1.0 - lam_init))
    o_ref[0] = on.astype(BF16)


def _attn_call(bound, lamv, subln, q, k, vt, lc, lam_init, tq, tk, bounded):
    b, l, d = q.shape
    la = k.shape[1]
    return pl.pallas_call(
        functools.partial(_attn_kernel, lc=lc, tk=tk, lam_init=lam_init, bounded=bounded),
        out_shape=jax.ShapeDtypeStruct((b, d, l), BF16),
        grid=(b, N_HEADS, l // tq),
        in_specs=[pl.BlockSpec(bound.shape, lambda bi, h, i: (0, 0)),
                  pl.BlockSpec(lamv.shape, lambda bi, h, i: (0, 0)),
                  pl.BlockSpec(subln.shape, lambda bi, h, i: (0, 0)),
                  pl.BlockSpec((1, tq, V_DIM), lambda bi, h, i: (bi, i, h)),
                  pl.BlockSpec((1, la, V_DIM), lambda bi, h, i: (bi, 0, h)),
                  pl.BlockSpec((1, V_DIM, la), lambda bi, h, i: (bi, h, 0))],
        out_specs=pl.BlockSpec((1, V_DIM, tq), lambda bi, h, i: (bi, h, i)),
        compiler_params=_cparams(("parallel", "parallel", "arbitrary")),
        name="attn_bounded" if bounded else "attn_online",
    )(bound, lamv, subln, q, k, vt)


def _s5_matrices(lam_re, lam_im, log_dt, b_re, b_im, c_re, c_im, d_skip, seg):
    t = CHUNK
    g_all, p_n = lam_re.shape[1:]
    c_n = b_re.shape[-1]
    sg = g_all // SUPER
    ks = jnp.arange(t + 1, dtype=F32)[:, None, None]
    mats = []
    for dr in range(2):
        lre, lim = lam_re[dr], lam_im[dr]
        dt = jnp.exp(log_dt[dr])[:, None]
        mag = jnp.exp(ks * (lre * dt))
        akr = mag * jnp.cos(ks * (lim * dt))
        aki = mag * jnp.sin(ks * (lim * dt))
        ar, ai = akr[1], aki[1]
        den = lre * lre + lim * lim
        nr = ar - 1.0
        fr = (nr * lre + ai * lim) / den
        fi = (ai * lre - nr * lim) / den
        bbr = fr[..., None] * b_re[dr] - fi[..., None] * b_im[dr]
        bbi = fr[..., None] * b_im[dr] + fi[..., None] * b_re[dr]
        abr = akr[..., None] * bbr - aki[..., None] * bbi
        abi = akr[..., None] * bbi + aki[..., None] * bbr
        kern = (jnp.einsum('gdp,kgpc->kgdc', c_re[dr], abr)
                - jnp.einsum('gdp,kgpc->kgdc', c_im[dr], abi))
        car = c_re[dr][None] * akr[:, :, None, :] - c_im[dr][None] * aki[:, :, None, :]
        cai = c_re[dr][None] * aki[:, :, None, :] + c_im[dr][None] * akr[:, :, None, :]
        mats.append((akr, aki, abr, abi, kern, car, cai))

    s_idx = jnp.arange(t)
    kf, kb = mats[0][4], mats[1][4]
    d_diag = jnp.eye(c_n, dtype=F32)[None] * d_skip.reshape(g_all, 1, c_n)
    lagk = jnp.concatenate([kb[1:t][::-1], (kf[0] + kb[0] + d_diag)[None], kf[1:t]], axis=0)
    lagk = lagk.transpose(1, 3, 0, 2).reshape(g_all, c_n, (2 * t - 1) * c_n)
    li, ci = np.divmod(np.arange((2 * t - 1) * c_n), c_n)
    ti, cj = np.divmod(np.arange(t * c_n), c_n)
    pick = ((li[None, :, None] == ti[None, None, :] - np.arange(t)[:, None, None] + t - 1)
            & (ci[None, :, None] == cj[None, None, :]))
    m = jnp.einsum('gcx,sxy->gscy', lagk, jnp.asarray(pick, F32))
    m = m.reshape(sg, SUPER, t, c_n, t * c_n).transpose(0, 2, 1, 3, 4)
    m = m.reshape(sg, t * SUPER * c_n, t * c_n)

    def state_in(abr, abi, order):
        wr = abr[order].reshape(t, sg, SUPER, p_n, c_n)
        wi = abi[order].reshape(t, sg, SUPER, p_n, c_n)
        w = jnp.stack([wr, wi], axis=0)
        return w.transpose(2, 1, 3, 5, 0, 4).reshape(sg, t * SUPER * c_n, 2 * p_n)

    def state_out(car, cai, order):
        vr = car[order].reshape(t, sg, SUPER, c_n, p_n)
        vi = -cai[order].reshape(t, sg, SUPER, c_n, p_n)
        v = jnp.stack([vr, vi], axis=0)
        return v.transpose(2, 0, 3, 5, 1, 4).reshape(sg, 2 * SUPER * p_n, t * c_n)

    wf = state_in(mats[0][2], mats[0][3], (t - 1) - s_idx)
    wb = state_in(mats[1][2], mats[1][3], s_idx)
    vf = state_out(mats[0][5], mats[0][6], s_idx + 1)
    vb = state_out(mats[1][5], mats[1][6], t - s_idx)

    def decay_table(dr, powers):
        lre, lim = lam_re[dr], lam_im[dr]
        dt = jnp.exp(log_dt[dr])[:, None]
        kk = (t * powers).astype(F32)[:, None, None]
        mag = jnp.exp(kk * (lre * dt))
        tab = jnp.stack([mag * jnp.cos(kk * (lim * dt)), mag * jnp.sin(kk * (lim * dt))], axis=0)
        return tab.reshape(2, -1, sg, SUPER * p_n).transpose(2, 0, 1, 3)

    j = jnp.arange(seg)
    bf = lambda a: a.astype(BF16)
    return (bf(m), bf(wf), bf(wb), bf(vf), bf(vb), decay_table(0, j + 1), decay_table(1, seg - j))


def _s5_segment(n):
    return 8 * (-(-n // 64))


def _cmul_add(ar, ai, xr, xi, br, bi):
    return ar * xr - ai * xi + br, ar * xi + ai * xr + bi


def _group_replicator(outer, inner):
    col = np.arange(outer * SUPER * inner)
    src = (col // (SUPER * inner)) * inner + col % inner
    return jnp.asarray(np.arange(outer * inner)[:, None] == src[None, :], dtype=BF16)


def _expand_groups(xc, rep, row_block, col_block):
    dense = jnp.dot(xc, rep, preferred_element_type=F32)
    r = lax.broadcasted_iota(jnp.int32, dense.shape, 0)
    c = lax.broadcasted_iota(jnp.int32, dense.shape, 1)
    own = (r // row_block) % SUPER == (c // col_block) % SUPER
    return jnp.where(own, dense, 0.0).astype(BF16)


def _s5_kernel(u_ref, mc_ref, wfc_ref, wbc_ref, vfc_ref, vbc_ref, rep_out_ref, rep_state_ref, pf_ref, pb_ref,
               y_ref, m_ref, wf_ref, wb_ref, vf_ref, vb_ref, sf_ref, sb_ref, *, nx, nc):
    @pl.when(pl.program_id(1) == 0)
    def _build_matrices():
        rep_out, rep_state = rep_out_ref[...], rep_state_ref[...]
        m_ref[...] = _expand_groups(mc_ref[0], rep_out, S5_GROUP, S5_GROUP)
        wf_ref[...] = _expand_groups(wfc_ref[0], rep_state, S5_GROUP, S5_STATE)
        wb_ref[...] = _expand_groups(wbc_ref[0], rep_state, S5_GROUP, S5_STATE)
        vf_ref[...] = _expand_groups(vfc_ref[0], rep_out, S5_STATE, S5_GROUP)
        vb_ref[...] = _expand_groups(vbc_ref[0], rep_out, S5_STATE, S5_GROUP)

    n = nx + nc
    tiles, npad = sf_ref.shape[1:3]
    seg = npad // 8
    pad = npad - n
    half = tiles * LANES

    def load(ref, rows):
        return tuple(jnp.concatenate([ref[ri, c, rows, :] for c in range(tiles)], axis=-1) for ri in range(2))

    def store(ref, rows, re, im):
        for c in range(tiles):
            ref[0, c, rows, :] = re[:, c * LANES:(c + 1) * LANES]
            ref[1, c, rows, :] = im[:, c * LANES:(c + 1) * LANES]

    z = jnp.concatenate([u_ref[0, 0, pl.ds(t, n, stride=CHUNK), :] for t in range(CHUNK)], axis=-1).astype(BF16)
    zl, zc = z[:nx], z[nx:]
    zpad = [jnp.zeros((pad, z.shape[1]), BF16)] if pad else []
    s = jnp.dot(jnp.concatenate([zc, zl] + zpad, axis=0), wf_ref[...], preferred_element_type=F32)
    store(sf_ref, slice(None), s[:, :half], s[:, half:])
    s = jnp.dot(jnp.concatenate(zpad + [zl, zc], axis=0), wb_ref[...], preferred_element_type=F32)
    store(sb_ref, slice(None), s[:, :half], s[:, half:])

    afr, afi = pf_ref[0, 0, 0:1, :], pf_ref[0, 1, 0:1, :]
    abr, abi = pb_ref[0, 0, seg - 1:seg, :], pb_ref[0, 1, seg - 1:seg, :]

    def step(i, carry):
        hr, hi, gr, gi = carry
        rows = pl.ds(i, 8, stride=seg)
        hr, hi = _cmul_add(afr, afi, hr, hi, *load(sf_ref, rows))
        store(sf_ref, rows, hr, hi)
        rows = pl.ds(seg - 1 - i, 8, stride=seg)
        gr, gi = _cmul_add(abr, abi, gr, gi, *load(sb_ref, rows))
        store(sb_ref, rows, gr, gi)
        return hr, hi, gr, gi

    zero = jnp.zeros((8, half), F32)
    hr, hi, gr, gi = lax.fori_loop(0, seg, step, (zero, zero, zero, zero), unroll=True)

    tfr, tfi = pf_ref[0, 0], pf_ref[0, 1]
    tbr, tbi = pb_ref[0, 0], pb_ref[0, 1]
    cr = ci = None
    for s in range(1, 8):
        er, ei = hr[s - 1:s], hi[s - 1:s]
        if cr is None:
            cr, ci = er, ei
        else:
            cr, ci = _cmul_add(tfr[seg - 1:seg], tfi[seg - 1:seg], cr, ci, er, ei)
        rows = slice(s * seg, (s + 1) * seg)
        store(sf_ref, rows, *_cmul_add(tfr, tfi, cr, ci, *load(sf_ref, rows)))
    cr = ci = None
    for s in range(6, -1, -1):
        er, ei = gr[s + 1:s + 2], gi[s + 1:s + 2]
        if cr is None:
            cr, ci = er, ei
        else:
            cr, ci = _cmul_add(tbr[0:1], tbi[0:1], cr, ci, er, ei)
        rows = slice(s * seg, (s + 1) * seg)
        store(sb_ref, rows, *_cmul_add(tbr, tbi, cr, ci, *load(sb_ref, rows)))

    def entering(ref, shift, first):
        cols = [pltpu.roll(ref[ri, c], shift, 0)[first:first + nx] for ri in range(2) for c in range(tiles)]
        return jnp.concatenate(cols, axis=-1).astype(BF16)

    y = (jnp.dot(zl, m_ref[...], preferred_element_type=F32)
         + jnp.dot(entering(sf_ref, 1, nc), vf_ref[...], preferred_element_type=F32)
         + jnp.dot(entering(sb_ref, npad - 1, pad), vb_ref[...], preferred_element_type=F32))
    for t in range(CHUNK):
        y_ref[0, 0, pl.ds(t, nx, stride=CHUNK), :] = y[:, t * LANES:(t + 1) * LANES]


def _s5_call(u, mats, lc):
    b, sg, la, _ = u.shape
    t = CHUNK
    n, nc = la // t, lc // t
    nx = n - nc
    npad = 8 * _s5_segment(n)
    width = t * LANES
    tiles = width // (2 * LANES)
    mc, wfc, wbc, vfc, vbc, pf, pb = mats
    rep_out = _group_replicator(t, S5_GROUP)
    rep_state = _group_replicator(2, S5_STATE)
    wspec = lambda a: pl.BlockSpec((1,) + a.shape[1:], lambda g, bi: (g,) + (0,) * (a.ndim - 1))
    dense = pltpu.VMEM((width, width), BF16)
    states = pltpu.VMEM((2, tiles, npad, LANES), F32)
    return pl.pallas_call(
        functools.partial(_s5_kernel, nx=nx, nc=nc),
        out_shape=jax.ShapeDtypeStruct((b, sg, nx * t, LANES), F32),
        grid=(sg, b),
        in_specs=[pl.BlockSpec((1, 1, la, LANES), lambda g, bi: (bi, g, 0, 0))]
                 + [wspec(a) for a in (mc, wfc, wbc, vfc, vbc)]
                 + [_resident(rep_out.shape), _resident(rep_state.shape), wspec(pf), wspec(pb)],
        out_specs=pl.BlockSpec((1, 1, nx * t, LANES), lambda g, bi: (bi, g, 0, 0)),
        scratch_shapes=[dense] * 5 + [states] * 2,
        compiler_params=_cparams(("arbitrary", "arbitrary")),
        name="s5",
    )(u, mc, wfc, wbc, vfc, vbc, rep_out, rep_state, pf, pb)


def _gelu_tanh(y):
    return 0.5 * y * (1.0 + jnp.tanh(math.sqrt(2.0 / math.pi) * (y + 0.044715 * (y * y * y))))


def _merge_ffn_kernel(x_ref, mod_ref, y_ref, oxt_ref, ga_ref, gs_ref, wglu_ref, bglu_ref, wpa_ref, wps_ref,
                      wout_ref, g2_ref, w13_ref, w2_ref, o_ref, *, base):
    mod = mod_ref[0]
    for r in range(x_ref.shape[1] // FFN_ROWS):
        rs = slice(r * FFN_ROWS, (r + 1) * FFN_ROWS)
        ge =_gelu_tanh(jnp.concatenate([y_ref[0, g, rs, :] for g in range(y_ref.shape[1])], axis=-1))
        z = jnp.dot(ge.astype(BF16), wglu_ref[...], preferred_element_type=F32) + bglu_ref[...]
        sx = (ge * jax.nn.sigmoid(z)).astype(BF16)
        attn = lax.dot_general(oxt_ref[0, :, rs], wpa_ref[...], (((0,), (0,)), ((), ())),
                               preferred_element_type=F32)
        mix = (ga_ref[0, rs, :].astype(F32) * attn
               + gs_ref[0, rs, :].astype(F32) * jnp.dot(sx, wps_ref[...], preferred_element_type=F32))
        out = jnp.dot(mix.astype(BF16), wout_ref[...], preferred_element_type=F32)
        x2 = x_ref[0, rs, :] + mod[base + 2:base + 3, :] * out
        o_ref[0, rs, :] = _ffn_rows(x2, g2_ref[...], mod, base + 3, w13_ref, w2_ref)


def _merge_ffn_call(x, mod, y, oxt, ga, gs, w_glu, b_glu, w_pa, w_ps, w_out, g2, w13, w2, base, tm):
    b, l, d = x.shape
    sg = y.shape[1]
    tok = lambda width: pl.BlockSpec((1, tm, width), lambda bi, i: (bi, i, 0))
    weights = (w_glu, b_glu, w_pa, w_ps, w_out, g2, w13, w2)
    return pl.pallas_call(
        functools.partial(_merge_ffn_kernel, base=base),
        out_shape=jax.ShapeDtypeStruct(x.shape, F32),
        grid=(b, l // tm),
        in_specs=[tok(d), pl.BlockSpec((1, N_MOD, d), lambda bi, i: (bi, 0, 0)),
                  pl.BlockSpec((1, sg, tm, LANES), lambda bi, i: (bi, 0, i, 0)),
                  pl.BlockSpec((1, d, tm), lambda bi, i: (bi, 0, i)), tok(d), tok(d)]
                 + [_resident(w.shape) for w in weights],
        out_specs=tok(d),
        compiler_params=_cparams(("parallel", "parallel")),
        name="merge_ffn2",
    )(x, mod, y, oxt, ga, gs, *weights)


def _rope_tables(l):
    n_freq = HEAD_DIM // 4
    inv = ROPE_BASE ** (-jnp.arange(n_freq, dtype=F32) / n_freq)
    pos = jnp.arange(l, dtype=jnp.int32)
    row = (pos // GRID_W).astype(F32)
    col = (pos % GRID_W).astype(F32)
    ang = jnp.concatenate([row[:, None] * inv, col[:, None] * inv], axis=-1)
    cos, sin = jnp.cos(ang), jnp.sin(ang)
    return jnp.concatenate([cos] * 4, axis=-1), jnp.concatenate([-sin, sin, -sin, sin], axis=-1)


def _group_ones():
    grp = np.arange(MXU_WIDTH) // HEAD_DIM
    return jnp.asarray(grp[:, None] == grp[None, :], dtype=BF16)


def kernel(x, c, ctx, c_ctx, w_mod, b_mod, norm_ffn1, w13_ffn1, w2_ffn1, norm_mix, w_in, q_norm, k_norm, lam_q1, lam_k1, lam_q2, lam_k2, subln, s5_lam_re, s5_lam_im, s5_log_dt, s5_b_re, s5_b_im, s5_c_re, s5_c_im, s5_d, w_glu, b_glu, w_pa, w_ps, w_out, norm_ffn2, w13_ffn2, w2_ffn2):
    assert w_mod.shape[0] == 1, "single-layer block"
    b, l, d = x.shape
    lc = ctx.shape[1]
    assert l % max(TILE_FFN1, TILE_PROJ, TILE_MERGE, TILE_Q_BOUNDED, TILE_K_ONLINE) == 0 and l % lc == 0
    assert lc % (2 * CHUNK) == 0 and d == N_HEADS * V_DIM
    lam_init = 0.8 - 0.6 * math.exp(-0.3 * 0)
    bf = lambda a: a.astype(BF16)
    row = lambda a: a.reshape(1, -1)

    rows = b + 1
    pad = (-rows) % 8
    cc = jnp.concatenate([c, c_ctx[None, :], jnp.zeros((pad, d), F32)], axis=0)
    mod = _mod_call(cc, w_mod[0], row(b_mod[0]))
    modx = mod[:b].reshape(b, N_MOD, d)
    modc = mod[b:b + 1].reshape(1, N_MOD, d)

    w13_1, w2_1 = bf(w13_ffn1[0]), bf(w2_ffn1[0])
    x1 = _ffn_call(x, modx, row(norm_ffn1[0]), w13_1, w2_1, 0, TILE_FFN1, "ffn1_x")

    ones = _group_ones()
    cos, sin = _rope_tables(l)
    qg = row(jnp.tile(q_norm[0], d // HEAD_DIM)) * (LOG2E / math.sqrt(HEAD_DIM))
    kg = row(jnp.tile(k_norm[0], d // HEAD_DIM))
    w_inb = bf(w_in[0])
    s5w = s5_d.shape[-1]
    q, k, vt, u, ga, gs = _inproj_call(x1, lc, modx, row(norm_mix[0]), w_inb, qg, kg, ones,
                                       cos, sin, s5w, 3, TILE_PROJ)
    k, vt, u = _ctx_call(ctx, modc, row(norm_ffn1[0]), w13_1, w2_1, row(norm_mix[0]),
                         w_inb[:, d:3 * d + s5w], kg, ones, k, vt, u, 3)

    lamv = jnp.stack([lam_q1[0], lam_k1[0], lam_q2[0], lam_k2[0]], axis=0)
    bound = (LOG2E * math.sqrt(HEAD_DIM)) * jnp.max(jnp.abs(q_norm[0])) * jnp.max(jnp.abs(k_norm[0]))
    attn = lambda bounded, tq, tk: functools.partial(
        _attn_call, lamv=lamv, subln=subln[0].reshape(-1, 1), q=q, k=k, vt=vt, lc=lc, lam_init=lam_init,
        tq=tq, tk=tk, bounded=bounded)
    oxt = lax.cond(2.0 * bound <= MAX_EXP2_SPAN, attn(True, TILE_Q_BOUNDED, None),
                   attn(False, TILE_Q_ONLINE, TILE_K_ONLINE), bound.reshape(1, 1))

    mats = _s5_matrices(s5_lam_re[0], s5_lam_im[0], s5_log_dt[0], s5_b_re[0], s5_b_im[0],
                        s5_c_re[0], s5_c_im[0], s5_d[0], _s5_segment((l + lc) // CHUNK))
    y = _s5_call(u, mats, lc)

    return _merge_ffn_call(x1, modx, y, oxt, ga, gs, bf(w_glu[0]), row(b_glu[0]), bf(w_pa[0]), bf(w_ps[0]),
                           bf(w_out[0]), row(norm_ffn2[0]), bf(w13_ffn2[0]), bf(w2_ffn2[0]), 3, TILE_MERGE)
```

```python
import functools
import math

import jax
import jax.numpy as jnp
import numpy as np
from jax import lax
from jax.experimental import pallas as pl
from jax.experimental.pallas import tpu as pltpu

F32 = jnp.float32
BF16 = jnp.bfloat16

EPS = 1e-6
ROPE_BASE = 10000.0
GRID_W = 64
N_HEADS = 8
HEAD_DIM = 64
V_DIM = 2 * HEAD_DIM
S5_GROUP = 16
S5_STATE = 64
N_MOD = 9
LANES = 128
MXU_WIDTH = 256
CHUNK = 8
SUPER = LANES // S5_GROUP
VMEM_CAPACITY = 64 * 1024 * 1024
VMEM_LIMIT = VMEM_CAPACITY - 8 * 1024 * 1024
LOG2E = math.log2(math.e)
MAX_EXP2_SPAN = 100.0

FFN_ROWS = 256
PROJ_ROWS = 256
TILE_FFN1 = 1024
TILE_PROJ = 1024
TILE_MERGE = 512
TILE_Q_BOUNDED = 1024
TILE_Q_ONLINE, TILE_K_ONLINE = 256, 1024


def _cparams(sem):
    return pltpu.CompilerParams(dimension_semantics=sem, vmem_limit_bytes=VMEM_LIMIT)


def _resident(shape):
    nd = len(shape)
    return pl.BlockSpec(shape, lambda *_: (0,) * nd, pipeline_mode=pl.Buffered(1))


def _silu(a):
    return a * jax.nn.sigmoid(a)


def _modulated(x, g, mod, base):
    ms = jnp.mean(x * x, axis=-1, keepdims=True)
    xn = x * lax.rsqrt(ms + EPS) * g
    return xn * (1.0 + mod[base + 1:base + 2, :]) + mod[base:base + 1, :]


def _mod_kernel(c_ref, w_ref, b_ref, o_ref):
    a = _silu(c_ref[...]).astype(BF16)
    o_ref[...] = jnp.dot(a, w_ref[...].astype(BF16), preferred_element_type=F32) + b_ref[...]


def _mod_call(cc, w_mod, b_mod):
    rows, d = cc.shape
    n = w_mod.shape[1]
    tn = d
    return pl.pallas_call(
        _mod_kernel,
        out_shape=jax.ShapeDtypeStruct((rows, n), F32),
        grid=(n // tn,),
        in_specs=[pl.BlockSpec((rows, d), lambda j: (0, 0)),
                  pl.BlockSpec((d, tn), lambda j: (0, j)),
                  pl.BlockSpec((1, tn), lambda j: (0, j))],
        out_specs=pl.BlockSpec((rows, tn), lambda j: (0, j)),
        compiler_params=_cparams(("arbitrary",)),
        name="mod",
    )(cc, w_mod, b_mod)


def _ffn_rows(x, g, mod, base, w13_ref, w2_ref):
    dff = w2_ref.shape[0]
    xm = _modulated(x, g, mod, base).astype(BF16)
    h = jnp.dot(xm, w13_ref[...], preferred_element_type=F32)
    act = (_silu(h[:, :dff]) * h[:, dff:]).astype(BF16)
    return x + (0.5 * mod[base + 2:base + 3, :]) * jnp.dot(act, w2_ref[...], preferred_element_type=F32)


def _ffn_kernel(x_ref, mod_ref, g_ref, w13_ref, w2_ref, o_ref, *, base):
    for r in range(x_ref.shape[1] // FFN_ROWS):
        rs = slice(r * FFN_ROWS, (r + 1) * FFN_ROWS)
        o_ref[0, rs, :] = _ffn_rows(x_ref[0, rs, :], g_ref[...], mod_ref[0], base, w13_ref, w2_ref)


def _ffn_call(x, mod, g, w13, w2, base, tm, name):
    b, l, d = x.shape
    return pl.pallas_call(
        functools.partial(_ffn_kernel, base=base),
        out_shape=jax.ShapeDtypeStruct(x.shape, F32),
        grid=(b, l // tm),
        in_specs=[pl.BlockSpec((1, tm, d), lambda bi, i: (bi, i, 0)),
                  pl.BlockSpec((1, N_MOD, d), lambda bi, i: (bi, 0, 0)),
                  _resident(g.shape), _resident(w13.shape), _resident(w2.shape)],
        out_specs=pl.BlockSpec((1, tm, d), lambda bi, i: (bi, i, 0)),
        compiler_params=_cparams(("parallel", "parallel")),
        name=name,
    )(x, mod, g, w13, w2)


def _qk_norm(t, gain, ones_ref):
    sq = (t * t).astype(BF16)
    w = ones_ref.shape[0]
    ss = jnp.concatenate([jnp.dot(sq[:, j:j + w], ones_ref[...], preferred_element_type=F32)
                          for j in range(0, t.shape[-1], w)], axis=-1)
    return t * lax.rsqrt(ss * (1.0 / HEAD_DIM) + EPS) * gain


def _rope(t, cos, sin_signed):
    n = t.shape[-1]
    half = HEAD_DIM // 2
    lane = lax.broadcasted_iota(jnp.int32, t.shape, 1)
    swapped = jnp.where(lane % HEAD_DIM < half, pltpu.roll(t, n - half, 1), pltpu.roll(t, half, 1))
    reps = n // cos.shape[-1]
    cosf = jnp.concatenate([cos] * reps, axis=-1)
    sinf = jnp.concatenate([sin_signed] * reps, axis=-1)
    return t * cosf + swapped * sinf


def _store_supergroups(u_ref, rs, u):
    for g in range(u_ref.shape[1]):
        u_ref[0, g, rs, :] = u[:, g * LANES:(g + 1) * LANES]


def _inproj_kernel(x_ref, mod_ref, g_ref, w_ref, qg_ref, kg_ref, ones_ref, cos_ref, sin_ref,
                   q_ref, k_ref, vt_ref, u_ref, ga_ref, gs_ref, *, base):
    d = x_ref.shape[-1]
    qc, kc, vc = d, 2 * d, 3 * d
    uc = vc + u_ref.shape[1] * LANES
    for r in range(x_ref.shape[1] // PROJ_ROWS):
        rs = slice(r * PROJ_ROWS, (r + 1) * PROJ_ROWS)
        xm = _modulated(x_ref[0, rs, :], g_ref[...], mod_ref[0], base).astype(BF16)
        proj = lambda lo, hi: jnp.dot(xm, w_ref[:, lo:hi], preferred_element_type=F32)
        cos = cos_ref[rs, :]
        sin = sin_ref[rs, :]
        tq = proj(0, qc)
        tk = proj(qc, kc)
        q_ref[0, rs, :] = _rope(_qk_norm(tq, qg_ref[...], ones_ref), cos, sin).astype(BF16)
        vt_ref[0, :, rs] = proj(kc, vc).astype(BF16).T
        k_ref[0, rs, :] = _rope(_qk_norm(tk, kg_ref[...], ones_ref), cos, sin).astype(BF16)
        _store_supergroups(u_ref, rs, proj(vc, uc))
        ga_ref[0, rs, :] = jax.nn.sigmoid(proj(uc, uc + d)).astype(BF16)
        gs_ref[0, rs, :] = jax.nn.sigmoid(proj(uc + d, uc + 2 * d)).astype(BF16)


def _inproj_call(x, modx, g, w_in, qg, kg, ones, cos, sin, s5w, base, tm):
    b, l, d = x.shape
    tok = lambda bi, i: (bi, i, 0)
    out_shape = (jax.ShapeDtypeStruct((b, l, d), BF16),
                 jax.ShapeDtypeStruct((b, l, d), BF16),
                 jax.ShapeDtypeStruct((b, d, l), BF16),
                 jax.ShapeDtypeStruct((b, s5w // LANES, l, LANES), F32),
                 jax.ShapeDtypeStruct((b, l, d), BF16),
                 jax.ShapeDtypeStruct((b, l, d), BF16))
    return pl.pallas_call(
        functools.partial(_inproj_kernel, base=base),
        out_shape=out_shape,
        grid=(b, l // tm),
        in_specs=[pl.BlockSpec((1, tm, d), tok),
                  pl.BlockSpec((1, N_MOD, d), lambda bi, i: (bi, 0, 0)),
                  _resident(g.shape), _resident(w_in.shape), _resident(qg.shape), _resident(kg.shape),
                  _resident(ones.shape),
                  pl.BlockSpec((tm, LANES), lambda bi, i: (i, 0)),
                  pl.BlockSpec((tm, LANES), lambda bi, i: (i, 0))],
        out_specs=(pl.BlockSpec((1, tm, d), tok),
                   pl.BlockSpec((1, tm, d), tok),
                   pl.BlockSpec((1, d, tm), lambda bi, i: (bi, 0, i)),
                   pl.BlockSpec((1, s5w // LANES, tm, LANES), lambda bi, i: (bi, 0, i, 0)),
                   pl.BlockSpec((1, tm, d), tok),
                   pl.BlockSpec((1, tm, d), tok)),
        compiler_params=_cparams(("parallel", "parallel")),
        name="inproj",
    )(x, modx, g, w_in, qg, kg, ones, cos, sin)


def _ctx_kernel(ctx_ref, mod_ref, g1_ref, w13_ref, w2_ref, gm_ref, w_ref, kg_ref, ones_ref,
                k_ref, vt_ref, u_ref, *, base):
    d = ctx_ref.shape[-1]
    mod = mod_ref[0]
    x1 = _ffn_rows(ctx_ref[0], g1_ref[...], mod, 0, w13_ref, w2_ref)
    xm = _modulated(x1, gm_ref[...], mod, base).astype(BF16)
    p = jnp.dot(xm, w_ref[...], preferred_element_type=F32)
    k_ref[0] = _qk_norm(p[:, :d], kg_ref[...], ones_ref).astype(BF16)
    vt_ref[0] = p[:, d:2 * d].astype(BF16).T
    _store_supergroups(u_ref, slice(None), p[:, 2 * d:])


def _ctx_call(ctx, modc, g1, w13, w2, gm, w_kvu, kg, ones, s5w, base):
    b, lc, d = ctx.shape
    sg = s5w // LANES
    return pl.pallas_call(
        functools.partial(_ctx_kernel, base=base),
        out_shape=(jax.ShapeDtypeStruct((b, lc, d), BF16), jax.ShapeDtypeStruct((b, d, lc), BF16),
                   jax.ShapeDtypeStruct((b, sg, lc, LANES), F32)),
        grid=(b,),
        in_specs=[pl.BlockSpec((1, lc, d), lambda bi: (bi, 0, 0)),
                  pl.BlockSpec((1, N_MOD, d), lambda bi: (0, 0, 0)),
                  _resident(g1.shape), _resident(w13.shape), _resident(w2.shape), _resident(gm.shape),
                  _resident(w_kvu.shape), _resident(kg.shape), _resident(ones.shape)],
        out_specs=(pl.BlockSpec((1, lc, d), lambda bi: (bi, 0, 0)),
                   pl.BlockSpec((1, d, lc), lambda bi: (bi, 0, 0)),
                   pl.BlockSpec((1, sg, lc, LANES), lambda bi: (bi, 0, 0, 0))),
        compiler_params=_cparams(("parallel",)),
        name="ctx",
    )(ctx, modc, g1, w13, w2, gm, w_kvu, kg, ones)


def _attn_kernel(bound_ref, lam_ref, subln_ref, q_ref, k_ref, kc_ref, vt_ref, vtc_ref, o_ref, *, tk, lam_init,
                 bounded):
    q = q_ref[0]
    lc = kc_ref.shape[1]
    k_all = jnp.concatenate([k_ref[0], kc_ref[0]], axis=0)
    vt_all = jnp.concatenate([vt_ref[0], vtc_ref[0]], axis=1)
    la = k_all.shape[0]
    lane = lax.broadcasted_iota(jnp.int32, q.shape, 1)
    zero = jnp.zeros_like(q)
    q_comp = (jnp.where(lane < HEAD_DIM, q, zero), jnp.where(lane >= HEAD_DIM, q, zero))
    lv = lam_ref[...]
    lam = (jnp.exp(jnp.sum(lv[0:1] * lv[1:2], axis=-1, keepdims=True))
           - jnp.exp(jnp.sum(lv[2:3] * lv[3:4], axis=-1, keepdims=True)) + lam_init)
    blocks = [(0, la)] if bounded else [(st, tk) for st in range(0, la - lc, tk)] + [(la - lc, lc)]
    nt_dims = (((1,), (1,)), ((), ()))
    heads = []
    for qz in q_comp:
        m = l = acc = None
        for st, sz in blocks:
            kb = k_all[st:st + sz, :]
            vb = vt_all[:, st:st + sz]
            s = lax.dot_general(kb, qz, nt_dims, preferred_element_type=F32)
            if bounded:
                p = jnp.exp2(s - bound_ref[...])
                l = jnp.sum(p, axis=0, keepdims=True)
                acc = jnp.dot(vb, p.astype(BF16), preferred_element_type=F32)
                continue
            bm = jnp.max(s, axis=0, keepdims=True)
            if m is None:
                m_new = bm
                p = jnp.exp2(s - m_new)
                l = jnp.sum(p, axis=0, keepdims=True)
                acc = jnp.dot(vb, p.astype(BF16), preferred_element_type=F32)
            else:
                m_new = jnp.maximum(m, bm)
                alpha = jnp.exp2(m - m_new)
                p = jnp.exp2(s - m_new)
                l = alpha * l + jnp.sum(p, axis=0, keepdims=True)
                acc = alpha * acc + jnp.dot(vb, p.astype(BF16), preferred_element_type=F32)
            m = m_new
        heads.append(acc * (1.0 / l))
    o = heads[0] - lam * heads[1]
    ms = jnp.mean(o * o, axis=0, keepdims=True)
    on = o * lax.rsqrt(ms + EPS) * (subln_ref[...] * (1.0 - lam_init))
    o_ref[0] = on.astype(BF16)


def _attn_call(bound, lamv, subln, q, k, kc, vt, vtc, lam_init, tq, tk, bounded):
    b, l, d = q.shape
    lc = kc.shape[1]
    return pl.pallas_call(
        functools.partial(_attn_kernel, tk=tk, lam_init=lam_init, bounded=bounded),
        out_shape=jax.ShapeDtypeStruct((b, d, l), BF16),
        grid=(b, N_HEADS, l // tq),
        in_specs=[pl.BlockSpec(bound.shape, lambda bi, h, i: (0, 0)),
                  pl.BlockSpec(lamv.shape, lambda bi, h, i: (0, 0)),
                  pl.BlockSpec(subln.shape, lambda bi, h, i: (0, 0)),
                  pl.BlockSpec((1, tq, V_DIM), lambda bi, h, i: (bi, i, h)),
                  pl.BlockSpec((1, l, V_DIM), lambda bi, h, i: (bi, 0, h)),
                  pl.BlockSpec((1, lc, V_DIM), lambda bi, h, i: (bi, 0, h)),
                  pl.BlockSpec((1, V_DIM, l), lambda bi, h, i: (bi, h, 0)),
                  pl.BlockSpec((1, V_DIM, lc), lambda bi, h, i: (bi, h, 0))],
        out_specs=pl.BlockSpec((1, V_DIM, tq), lambda bi, h, i: (bi, h, i)),
        compiler_params=_cparams(("parallel", "parallel", "arbitrary")),
        name="attn_bounded" if bounded else "attn_online",
    )(bound, lamv, subln, q, k, kc, vt, vtc)


def _s5_matrices(lam_re, lam_im, log_dt, b_re, b_im, c_re, c_im, d_skip, seg):
    t = CHUNK
    g_all, p_n = lam_re.shape[1:]
    c_n = b_re.shape[-1]
    sg = g_all // SUPER
    ks = jnp.arange(t + 1, dtype=F32)[:, None, None]
    mats = []
    for dr in range(2):
        lre, lim = lam_re[dr], lam_im[dr]
        dt = jnp.exp(log_dt[dr])[:, None]
        mag = jnp.exp(ks * (lre * dt))
        akr = mag * jnp.cos(ks * (lim * dt))
        aki = mag * jnp.sin(ks * (lim * dt))
        ar, ai = akr[1], aki[1]
        den = lre * lre + lim * lim
        nr = ar - 1.0
        fr = (nr * lre + ai * lim) / den
        fi = (ai * lre - nr * lim) / den
        bbr = fr[..., None] * b_re[dr] - fi[..., None] * b_im[dr]
        bbi = fr[..., None] * b_im[dr] + fi[..., None] * b_re[dr]
        abr = akr[..., None] * bbr - aki[..., None] * bbi
        abi = akr[..., None] * bbi + aki[..., None] * bbr
        kern = (jnp.einsum('gdp,kgpc->kgdc', c_re[dr], abr)
                - jnp.einsum('gdp,kgpc->kgdc', c_im[dr], abi))
        car = c_re[dr][None] * akr[:, :, None, :] - c_im[dr][None] * aki[:, :, None, :]
        cai = c_re[dr][None] * aki[:, :, None, :] + c_im[dr][None] * akr[:, :, None, :]
        mats.append((akr, aki, abr, abi, kern, car, cai))

    s_idx = jnp.arange(t)
    kf, kb = mats[0][4], mats[1][4]
    d_diag = jnp.eye(c_n, dtype=F32)[None] * d_skip.reshape(g_all, 1, c_n)
    lagk = jnp.concatenate([kb[1:t][::-1], (kf[0] + kb[0] + d_diag)[None], kf[1:t]], axis=0)
    lagk = lagk.transpose(1, 3, 0, 2).reshape(g_all, c_n, (2 * t - 1) * c_n)
    li, ci = np.divmod(np.arange((2 * t - 1) * c_n), c_n)
    ti, cj = np.divmod(np.arange(t * c_n), c_n)
    pick = ((li[None, :, None] == ti[None, None, :] - np.arange(t)[:, None, None] + t - 1)
            & (ci[None, :, None] == cj[None, None, :]))
    m = jnp.einsum('gcx,sxy->gscy', lagk, jnp.asarray(pick, F32))
    m = m.reshape(sg, SUPER, t, c_n, t * c_n).transpose(0, 2, 1, 3, 4)
    m = m.reshape(sg, t * SUPER * c_n, t * c_n)

    def state_in(abr, abi, order):
        wr = abr[order].reshape(t, sg, SUPER, p_n, c_n)
        wi = abi[order].reshape(t, sg, SUPER, p_n, c_n)
        w = jnp.stack([wr, wi], axis=0)
        return w.transpose(2, 1, 3, 5, 0, 4).reshape(sg, t * SUPER * c_n, 2 * p_n)

    def state_out(car, cai, order):
        vr = car[order].reshape(t, sg, SUPER, c_n, p_n)
        vi = -cai[order].reshape(t, sg, SUPER, c_n, p_n)
        v = jnp.stack([vr, vi], axis=0)
        return v.transpose(2, 0, 3, 5, 1, 4).reshape(sg, 2 * SUPER * p_n, t * c_n)

    wf = state_in(mats[0][2], mats[0][3], (t - 1) - s_idx)
    wb = state_in(mats[1][2], mats[1][3], s_idx)
    vf = state_out(mats[0][5], mats[0][6], s_idx + 1)
    vb = state_out(mats[1][5], mats[1][6], t - s_idx)

    def decay_table(dr, powers):
        lre, lim = lam_re[dr], lam_im[dr]
        dt = jnp.exp(log_dt[dr])[:, None]
        kk = (t * powers).astype(F32)[:, None, None]
        mag = jnp.exp(kk * (lre * dt))
        tab = jnp.stack([mag * jnp.cos(kk * (lim * dt)), mag * jnp.sin(kk * (lim * dt))], axis=0)
        return tab.reshape(2, -1, sg, SUPER * p_n).transpose(2, 0, 1, 3)

    j = jnp.arange(seg)
    bf = lambda a: a.astype(BF16)
    return (bf(m), bf(wf), bf(wb), bf(vf), bf(vb), decay_table(0, j + 1), decay_table(1, seg - j))


def _s5_segment(n):
    return 8 * (-(-n // 64))


def _cmul_add(ar, ai, xr, xi, br, bi):
    return ar * xr - ai * xi + br, ar * xi + ai * xr + bi


def _group_replicator(outer, inner):
    col = np.arange(outer * SUPER * inner)
    src = (col // (SUPER * inner)) * inner + col % inner
    return jnp.asarray(np.arange(outer * inner)[:, None] == src[None, :], dtype=BF16)


def _expand_groups(xc, rep, row_block, col_block):
    dense = jnp.dot(xc, rep, preferred_element_type=F32)
    r = lax.broadcasted_iota(jnp.int32, dense.shape, 0)
    c = lax.broadcasted_iota(jnp.int32, dense.shape, 1)
    own = (r // row_block) % SUPER == (c // col_block) % SUPER
    return jnp.where(own, dense, 0.0).astype(BF16)


def _s5_kernel(u_ref, uc_ref, mc_ref, wfc_ref, wbc_ref, vfc_ref, vbc_ref, rep_out_ref, rep_state_ref, pf_ref,
               pb_ref, y_ref, m_ref, wf_ref, wb_ref, vf_ref, vb_ref, sf_ref, sb_ref, *, nx, nc):
    @pl.when(pl.program_id(1) == 0)
    def _build_matrices():
        rep_out, rep_state = rep_out_ref[...], rep_state_ref[...]
        m_ref[...] = _expand_groups(mc_ref[0], rep_out, S5_GROUP, S5_GROUP)
        wf_ref[...] = _expand_groups(wfc_ref[0], rep_state, S5_GROUP, S5_STATE)
        wb_ref[...] = _expand_groups(wbc_ref[0], rep_state, S5_GROUP, S5_STATE)
        vf_ref[...] = _expand_groups(vfc_ref[0], rep_out, S5_STATE, S5_GROUP)
        vb_ref[...] = _expand_groups(vbc_ref[0], rep_out, S5_STATE, S5_GROUP)

    n = nx + nc
    tiles, npad = sf_ref.shape[1:3]
    seg = npad // 8
    pad = npad - n
    half = tiles * LANES

    def load(ref, rows):
        return tuple(jnp.concatenate([ref[ri, c, rows, :] for c in range(tiles)], axis=-1) for ri in range(2))

    def store(ref, rows, re, im):
        for c in range(tiles):
            ref[0, c, rows, :] = re[:, c * LANES:(c + 1) * LANES]
            ref[1, c, rows, :] = im[:, c * LANES:(c + 1) * LANES]

    chunks = lambda ref, rows: jnp.concatenate(
        [ref[0, 0, pl.ds(t, rows, stride=CHUNK), :] for t in range(CHUNK)], axis=-1).astype(BF16)
    zl, zc = chunks(u_ref, nx), chunks(uc_ref, nc)
    zpad = [jnp.zeros((pad, zl.shape[1]), BF16)] if pad else []
    s = jnp.dot(jnp.concatenate([zc, zl] + zpad, axis=0), wf_ref[...], preferred_element_type=F32)
    store(sf_ref, slice(None), s[:, :half], s[:, half:])
    s = jnp.dot(jnp.concatenate(zpad + [zl, zc], axis=0), wb_ref[...], preferred_element_type=F32)
    store(sb_ref, slice(None), s[:, :half], s[:, half:])

    afr, afi = pf_ref[0, 0, 0:1, :], pf_ref[0, 1, 0:1, :]
    abr, abi = pb_ref[0, 0, seg - 1:seg, :], pb_ref[0, 1, seg - 1:seg, :]

    def step(i, carry):
        hr, hi, gr, gi = carry
        rows = pl.ds(i, 8, stride=seg)
        hr, hi = _cmul_add(afr, afi, hr, hi, *load(sf_ref, rows))
        store(sf_ref, rows, hr, hi)
        rows = pl.ds(seg - 1 - i, 8, stride=seg)
        gr, gi = _cmul_add(abr, abi, gr, gi, *load(sb_ref, rows))
        store(sb_ref, rows, gr, gi)
        return hr, hi, gr, gi

    zero = jnp.zeros((8, half), F32)
    hr, hi, gr, gi = lax.fori_loop(0, seg, step, (zero, zero, zero, zero), unroll=True)

    tfr, tfi = pf_ref[0, 0], pf_ref[0, 1]
    tbr, tbi = pb_ref[0, 0], pb_ref[0, 1]
    cr = ci = None
    for s in range(1, 8):
        er, ei = hr[s - 1:s], hi[s - 1:s]
        if cr is None:
            cr, ci = er, ei
        else:
            cr, ci = _cmul_add(tfr[seg - 1:seg], tfi[seg - 1:seg], cr, ci, er, ei)
        rows = slice(s * seg, (s + 1) * seg)
        store(sf_ref, rows, *_cmul_add(tfr, tfi, cr, ci, *load(sf_ref, rows)))
    cr = ci = None
    for s in range(6, -1, -1):
        er, ei = gr[s + 1:s + 2], gi[s + 1:s + 2]
        if cr is None:
            cr, ci = er, ei
        else:
            cr, ci = _cmul_add(tbr[0:1], tbi[0:1], cr, ci, er, ei)
        rows = slice(s * seg, (s + 1) * seg)
        store(sb_ref, rows, *_cmul_add(tbr, tbi, cr, ci, *load(sb_ref, rows)))

    def entering(ref, shift, first):
        cols = [pltpu.roll(ref[ri, c], shift, 0)[first:first + nx] for ri in range(2) for c in range(tiles)]
        return jnp.concatenate(cols, axis=-1).astype(BF16)

    y = (jnp.dot(zl, m_ref[...], preferred_element_type=F32)
         + jnp.dot(entering(sf_ref, 1, nc), vf_ref[...], preferred_element_type=F32)
         + jnp.dot(entering(sb_ref, npad - 1, pad), vb_ref[...], preferred_element_type=F32))
    for t in range(CHUNK):
        y_ref[0, 0, pl.ds(t, nx, stride=CHUNK), :] = y[:, t * LANES:(t + 1) * LANES]


def _s5_call(u, uc, mats):
    b, sg, l, _ = u.shape
    lc = uc.shape[2]
    t = CHUNK
    nx, nc = l // t, lc // t
    npad = 8 * _s5_segment(nx + nc)
    width = t * LANES
    tiles = width // (2 * LANES)
    mc, wfc, wbc, vfc, vbc, pf, pb = mats
    rep_out = _group_replicator(t, S5_GROUP)
    rep_state = _group_replicator(2, S5_STATE)
    wspec = lambda a: pl.BlockSpec((1,) + a.shape[1:], lambda g, bi: (g,) + (0,) * (a.ndim - 1))
    dense = pltpu.VMEM((width, width), BF16)
    states = pltpu.VMEM((2, tiles, npad, LANES), F32)
    return pl.pallas_call(
        functools.partial(_s5_kernel, nx=nx, nc=nc),
        out_shape=jax.ShapeDtypeStruct((b, sg, nx * t, LANES), F32),
        grid=(sg, b),
        in_specs=[pl.BlockSpec((1, 1, l, LANES), lambda g, bi: (bi, g, 0, 0)),
                  pl.BlockSpec((1, 1, lc, LANES), lambda g, bi: (bi, g, 0, 0))]
                 + [wspec(a) for a in (mc, wfc, wbc, vfc, vbc)]
                 + [_resident(rep_out.shape), _resident(rep_state.shape), wspec(pf), wspec(pb)],
        out_specs=pl.BlockSpec((1, 1, nx * t, LANES), lambda g, bi: (bi, g, 0, 0)),
        scratch_shapes=[dense] * 5 + [states] * 2,
        compiler_params=_cparams(("arbitrary", "arbitrary")),
        name="s5",
    )(u, uc, mc, wfc, wbc, vfc, vbc, rep_out, rep_state, pf, pb)


def _gelu_tanh(y):
    return 0.5 * y * (1.0 + jnp.tanh(math.sqrt(2.0 / math.pi) * (y + 0.044715 * (y * y * y))))


def _merge_ffn_kernel(x_ref, mod_ref, y_ref, oxt_ref, ga_ref, gs_ref, wglu_ref, bglu_ref, wpa_ref, wps_ref,
                      wout_ref, g2_ref, w13_ref, w2_ref, o_ref, *, base):
    mod = mod_ref[0]
    for r in range(x_ref.shape[1] // FFN_ROWS):
        rs = slice(r * FFN_ROWS, (r + 1) * FFN_ROWS)
        ge =_gelu_tanh(jnp.concatenate([y_ref[0, g, rs, :] for g in range(y_ref.shape[1])], axis=-1))
        z = jnp.dot(ge.astype(BF16), wglu_ref[...], preferred_element_type=F32) + bglu_ref[...]
        sx = (ge * jax.nn.sigmoid(z)).astype(BF16)
        attn = lax.dot_general(oxt_ref[0, :, rs], wpa_ref[...], (((0,), (0,)), ((), ())),
                               preferred_element_type=F32)
        mix = (ga_ref[0, rs, :].astype(F32) * attn
               + gs_ref[0, rs, :].astype(F32) * jnp.dot(sx, wps_ref[...], preferred_element_type=F32))
        out = jnp.dot(mix.astype(BF16), wout_ref[...], preferred_element_type=F32)
        x2 = x_ref[0, rs, :] + mod[base + 2:base + 3, :] * out
        o_ref[0, rs, :] = _ffn_rows(x2, g2_ref[...], mod, base + 3, w13_ref, w2_ref)


def _merge_ffn_call(x, mod, y, oxt, ga, gs, w_glu, b_glu, w_pa, w_ps, w_out, g2, w13, w2, base, tm):
    b, l, d = x.shape
    sg = y.shape[1]
    tok = lambda width: pl.BlockSpec((1, tm, width), lambda bi, i: (bi, i, 0))
    weights = (w_glu, b_glu, w_pa, w_ps, w_out, g2, w13, w2)
    return pl.pallas_call(
        functools.partial(_merge_ffn_kernel, base=base),
        out_shape=jax.ShapeDtypeStruct(x.shape, F32),
        grid=(b, l // tm),
        in_specs=[tok(d), pl.BlockSpec((1, N_MOD, d), lambda bi, i: (bi, 0, 0)),
                  pl.BlockSpec((1, sg, tm, LANES), lambda bi, i: (bi, 0, i, 0)),
                  pl.BlockSpec((1, d, tm), lambda bi, i: (bi, 0, i)), tok(d), tok(d)]
                 + [_resident(w.shape) for w in weights],
        out_specs=tok(d),
        compiler_params=_cparams(("parallel", "parallel")),
        name="merge_ffn2",
    )(x, mod, y, oxt, ga, gs, *weights)


def _rope_tables(l):
    n_freq = HEAD_DIM // 4
    inv = ROPE_BASE ** (-jnp.arange(n_freq, dtype=F32) / n_freq)
    pos = jnp.arange(l, dtype=jnp.int32)
    row = (pos // GRID_W).astype(F32)
    col = (pos % GRID_W).astype(F32)
    ang = jnp.concatenate([row[:, None] * inv, col[:, None] * inv], axis=-1)
    cos, sin = jnp.cos(ang), jnp.sin(ang)
    return jnp.concatenate([cos] * 4, axis=-1), jnp.concatenate([-sin, sin, -sin, sin], axis=-1)


def _group_ones():
    grp = np.arange(MXU_WIDTH) // HEAD_DIM
    return jnp.asarray(grp[:, None] == grp[None, :], dtype=BF16)


def kernel(x, c, ctx, c_ctx, w_mod, b_mod, norm_ffn1, w13_ffn1, w2_ffn1, norm_mix, w_in, q_norm, k_norm, lam_q1, lam_k1, lam_q2, lam_k2, subln, s5_lam_re, s5_lam_im, s5_log_dt, s5_b_re, s5_b_im, s5_c_re, s5_c_im, s5_d, w_glu, b_glu, w_pa, w_ps, w_out, norm_ffn2, w13_ffn2, w2_ffn2):
    assert w_mod.shape[0] == 1, "single-layer block"
    b, l, d = x.shape
    lc = ctx.shape[1]
    assert l % max(TILE_FFN1, TILE_PROJ, TILE_MERGE, TILE_Q_BOUNDED, TILE_K_ONLINE) == 0 and l % lc == 0
    assert lc % (2 * CHUNK) == 0 and d == N_HEADS * V_DIM
    lam_init = 0.8 - 0.6 * math.exp(-0.3 * 0)
    bf = lambda a: a.astype(BF16)
    row = lambda a: a.reshape(1, -1)

    rows = b + 1
    pad = (-rows) % 8
    cc = jnp.concatenate([c, c_ctx[None, :], jnp.zeros((pad, d), F32)], axis=0)
    mod = _mod_call(cc, w_mod[0], row(b_mod[0]))
    modx = mod[:b].reshape(b, N_MOD, d)
    modc = mod[b:b + 1].reshape(1, N_MOD, d)

    w13_1, w2_1 = bf(w13_ffn1[0]), bf(w2_ffn1[0])
    x1 = _ffn_call(x, modx, row(norm_ffn1[0]), w13_1, w2_1, 0, TILE_FFN1, "ffn1_x")

    ones = _group_ones()
    cos, sin = _rope_tables(l)
    qg = row(jnp.tile(q_norm[0], d // HEAD_DIM)) * (LOG2E / math.sqrt(HEAD_DIM))
    kg = row(jnp.tile(k_norm[0], d // HEAD_DIM))
    w_inb = bf(w_in[0])
    s5w = s5_d.shape[-1]
    q, k, vt, u, ga, gs = _inproj_call(x1, modx, row(norm_mix[0]), w_inb, qg, kg, ones,
                                       cos, sin, s5w, 3, TILE_PROJ)
    kc, vtc, uc = _ctx_call(ctx, modc, row(norm_ffn1[0]), w13_1, w2_1, row(norm_mix[0]),
                            w_inb[:, d:3 * d + s5w], kg, ones, s5w, 3)

    lamv = jnp.stack([lam_q1[0], lam_k1[0], lam_q2[0], lam_k2[0]], axis=0)
    bound = (LOG2E * math.sqrt(HEAD_DIM)) * jnp.max(jnp.abs(q_norm[0])) * jnp.max(jnp.abs(k_norm[0]))
    attn = lambda bounded, tq, tk: functools.partial(
        _attn_call, lamv=lamv, subln=subln[0].reshape(-1, 1), q=q, k=k, kc=kc, vt=vt, vtc=vtc,
        lam_init=lam_init, tq=tq, tk=tk, bounded=bounded)
    oxt = lax.cond(2.0 * bound <= MAX_EXP2_SPAN, attn(True, TILE_Q_BOUNDED, None),
                   attn(False, TILE_Q_ONLINE, TILE_K_ONLINE), bound.reshape(1, 1))

    mats = _s5_matrices(s5_lam_re[0], s5_lam_im[0], s5_log_dt[0], s5_b_re[0], s5_b_im[0],
                        s5_c_re[0], s5_c_im[0], s5_d[0], _s5_segment((l + lc) // CHUNK))
    y = _s5_call(u, uc, mats)

    return _merge_ffn_call(x1, modx, y, oxt, ga, gs, bf(w_glu[0]), row(b_glu[0]), bf(w_pa[0]), bf(w_ps[0]),
                           bf(w_out[0]), row(norm_ffn2[0]), bf(w13_ffn2[0]), bf(w2_ffn2[0]), 3, TILE_MERGE)
```

```python
import functools
import math

import jax
import jax.numpy as jnp
import numpy as np
from jax import lax
from jax.experimental import pallas as pl
from jax.experimental.pallas import tpu as pltpu

F32 = jnp.float32
BF16 = jnp.bfloat16

EPS = 1e-6
ROPE_BASE = 10000.0
GRID_W = 64
N_HEADS = 8
HEAD_DIM = 64
V_DIM = 2 * HEAD_DIM
S5_GROUP = 16
S5_STATE = 64
N_MOD = 9
LANES = 128
MXU_WIDTH = 256
CHUNK = 8
SUPER = LANES // S5_GROUP
VMEM_CAPACITY = 64 * 1024 * 1024
VMEM_LIMIT = VMEM_CAPACITY - 8 * 1024 * 1024
LOG2E = math.log2(math.e)
MAX_EXP2_SPAN = 100.0

FFN_ROWS = 256
PROJ_ROWS = 256
TILE_FFN1 = 1024
TILE_PROJ = 1024
TILE_MERGE = 512
TILE_Q_BOUNDED = 1024
TILE_Q_ONLINE, TILE_K_ONLINE = 256, 1024


def _cparams(sem):
    return pltpu.CompilerParams(dimension_semantics=sem, vmem_limit_bytes=VMEM_LIMIT)


def _resident(shape):
    nd = len(shape)
    return pl.BlockSpec(shape, lambda *_: (0,) * nd, pipeline_mode=pl.Buffered(1))


def _silu(a):
    return a * jax.nn.sigmoid(a)


def _modulated(x, g, mod, base):
    ms = jnp.mean(x * x, axis=-1, keepdims=True)
    xn = x * lax.rsqrt(ms + EPS) * g
    return xn * (1.0 + mod[base + 1:base + 2, :]) + mod[base:base + 1, :]


def _mod_kernel(c_ref, w_ref, b_ref, o_ref):
    a = _silu(c_ref[...]).astype(BF16)
    o_ref[...] = jnp.dot(a, w_ref[...].astype(BF16), preferred_element_type=F32) + b_ref[...]


def _mod_call(cc, w_mod, b_mod):
    rows, d = cc.shape
    n = w_mod.shape[1]
    tn = d
    return pl.pallas_call(
        _mod_kernel,
        out_shape=jax.ShapeDtypeStruct((rows, n), F32),
        grid=(n // tn,),
        in_specs=[pl.BlockSpec((rows, d), lambda j: (0, 0)),
                  pl.BlockSpec((d, tn), lambda j: (0, j)),
                  pl.BlockSpec((1, tn), lambda j: (0, j))],
        out_specs=pl.BlockSpec((rows, tn), lambda j: (0, j)),
        compiler_params=_cparams(("arbitrary",)),
        name="mod",
    )(cc, w_mod, b_mod)


def _ffn_rows(x, g, mod, base, w13_ref, w2_ref):
    dff = w2_ref.shape[0]
    xm = _modulated(x, g, mod, base).astype(BF16)
    h = jnp.dot(xm, w13_ref[...], preferred_element_type=F32)
    act = (_silu(h[:, :dff]) * h[:, dff:]).astype(BF16)
    return x + (0.5 * mod[base + 2:base + 3, :]) * jnp.dot(act, w2_ref[...], preferred_element_type=F32)


def _ffn_kernel(x_ref, mod_ref, g_ref, w13_ref, w2_ref, o_ref, *, base):
    for r in range(x_ref.shape[1] // FFN_ROWS):
        rs = slice(r * FFN_ROWS, (r + 1) * FFN_ROWS)
        o_ref[0, rs, :] = _ffn_rows(x_ref[0, rs, :], g_ref[...], mod_ref[0], base, w13_ref, w2_ref)


def _ffn_call(x, mod, g, w13, w2, base, tm, name):
    b, l, d = x.shape
    return pl.pallas_call(
        functools.partial(_ffn_kernel, base=base),
        out_shape=jax.ShapeDtypeStruct(x.shape, F32),
        grid=(b, l // tm),
        in_specs=[pl.BlockSpec((1, tm, d), lambda bi, i: (bi, i, 0)),
                  pl.BlockSpec((1, N_MOD, d), lambda bi, i: (bi, 0, 0)),
                  _resident(g.shape), _resident(w13.shape), _resident(w2.shape)],
        out_specs=pl.BlockSpec((1, tm, d), lambda bi, i: (bi, i, 0)),
        compiler_params=_cparams(("parallel", "parallel")),
        name=name,
    )(x, mod, g, w13, w2)


def _qk_norm(t, gain, ones_ref):
    sq = (t * t).astype(BF16)
    w = ones_ref.shape[0]
    ss = jnp.concatenate([jnp.dot(sq[:, j:j + w], ones_ref[...], preferred_element_type=F32)
                          for j in range(0, t.shape[-1], w)], axis=-1)
    return t * lax.rsqrt(ss * (1.0 / HEAD_DIM) + EPS) * gain


def _rope(t, cos, sin_signed):
    n = t.shape[-1]
    half = HEAD_DIM // 2
    lane = lax.broadcasted_iota(jnp.int32, t.shape, 1)
    swapped = jnp.where(lane % HEAD_DIM < half, pltpu.roll(t, n - half, 1), pltpu.roll(t, half, 1))
    reps = n // cos.shape[-1]
    cosf = jnp.concatenate([cos] * reps, axis=-1)
    sinf = jnp.concatenate([sin_signed] * reps, axis=-1)
    return t * cosf + swapped * sinf


def _store_supergroups(u_ref, rs, u):
    for g in range(u_ref.shape[1]):
        u_ref[0, g, rs, :] = u[:, g * LANES:(g + 1) * LANES]


def _inproj_kernel(x_ref, mod_ref, g_ref, w_ref, qg_ref, kg_ref, ones_ref, cos_ref, sin_ref,
                   q_ref, k_ref, vt_ref, u_ref, ga_ref, gs_ref, *, base):
    d = x_ref.shape[-1]
    qc, kc, vc = d, 2 * d, 3 * d
    uc = vc + u_ref.shape[1] * LANES
    for r in range(x_ref.shape[1] // PROJ_ROWS):
        rs = slice(r * PROJ_ROWS, (r + 1) * PROJ_ROWS)
        xm = _modulated(x_ref[0, rs, :], g_ref[...], mod_ref[0], base).astype(BF16)
        proj = lambda lo, hi: jnp.dot(xm, w_ref[:, lo:hi], preferred_element_type=F32)
        cos = cos_ref[rs, :]
        sin = sin_ref[rs, :]
        tq = proj(0, qc)
        tk = proj(qc, kc)
        q_ref[0, rs, :] = _rope(_qk_norm(tq, qg_ref[...], ones_ref), cos, sin).astype(BF16)
        vt_ref[0, :, rs] = proj(kc, vc).astype(BF16).T
        k_ref[0, rs, :] = _rope(_qk_norm(tk, kg_ref[...], ones_ref), cos, sin).astype(BF16)
        _store_supergroups(u_ref, rs, proj(vc, uc))
        ga_ref[0, rs, :] = jax.nn.sigmoid(proj(uc, uc + d)).astype(BF16)
        gs_ref[0, rs, :] = jax.nn.sigmoid(proj(uc + d, uc + 2 * d)).astype(BF16)


def _inproj_call(x, modx, g, w_in, qg, kg, ones, cos, sin, s5w, base, tm):
    b, l, d = x.shape
    tok = lambda bi, i: (bi, i, 0)
    out_shape = (jax.ShapeDtypeStruct((b, l, d), BF16),
                 jax.ShapeDtypeStruct((b, l, d), BF16),
                 jax.ShapeDtypeStruct((b, d, l), BF16),
                 jax.ShapeDtypeStruct((b, s5w // LANES, l, LANES), F32),
                 jax.ShapeDtypeStruct((b, l, d), BF16),
                 jax.ShapeDtypeStruct((b, l, d), BF16))
    return pl.pallas_call(
        functools.partial(_inproj_kernel, base=base),
        out_shape=out_shape,
        grid=(b, l // tm),
        in_specs=[pl.BlockSpec((1, tm, d), tok),
                  pl.BlockSpec((1, N_MOD, d), lambda bi, i: (bi, 0, 0)),
                  _resident(g.shape), _resident(w_in.shape), _resident(qg.shape), _resident(kg.shape),
                  _resident(ones.shape),
                  pl.BlockSpec((tm, LANES), lambda bi, i: (i, 0)),
                  pl.BlockSpec((tm, LANES), lambda bi, i: (i, 0))],
        out_specs=(pl.BlockSpec((1, tm, d), tok),
                   pl.BlockSpec((1, tm, d), tok),
                   pl.BlockSpec((1, d, tm), lambda bi, i: (bi, 0, i)),
                   pl.BlockSpec((1, s5w // LANES, tm, LANES), lambda bi, i: (bi, 0, i, 0)),
                   pl.BlockSpec((1, tm, d), tok),
                   pl.BlockSpec((1, tm, d), tok)),
        compiler_params=_cparams(("parallel", "parallel")),
        name="inproj",
    )(x, modx, g, w_in, qg, kg, ones, cos, sin)


def _ctx_kernel(ctx_ref, mod_ref, g1_ref, w13_ref, w2_ref, gm_ref, w_ref, kg_ref, ones_ref,
                k_ref, vt_ref, u_ref, *, base):
    d = ctx_ref.shape[-1]
    mod = mod_ref[0]
    x1 = _ffn_rows(ctx_ref[0], g1_ref[...], mod, 0, w13_ref, w2_ref)
    xm = _modulated(x1, gm_ref[...], mod, base).astype(BF16)
    p = jnp.dot(xm, w_ref[...], preferred_element_type=F32)
    k_ref[0] = _qk_norm(p[:, :d], kg_ref[...], ones_ref).astype(BF16)
    vt_ref[0] = p[:, d:2 * d].astype(BF16).T
    _store_supergroups(u_ref, slice(None), p[:, 2 * d:])


def _ctx_call(ctx, modc, g1, w13, w2, gm, w_kvu, kg, ones, s5w, base):
    b, lc, d = ctx.shape
    sg = s5w // LANES
    return pl.pallas_call(
        functools.partial(_ctx_kernel, base=base),
        out_shape=(jax.ShapeDtypeStruct((b, lc, d), BF16), jax.ShapeDtypeStruct((b, d, lc), BF16),
                   jax.ShapeDtypeStruct((b, sg, lc, LANES), F32)),
        grid=(b,),
        in_specs=[pl.BlockSpec((1, lc, d), lambda bi: (bi, 0, 0)),
                  pl.BlockSpec((1, N_MOD, d), lambda bi: (0, 0, 0)),
                  _resident(g1.shape), _resident(w13.shape), _resident(w2.shape), _resident(gm.shape),
                  _resident(w_kvu.shape), _resident(kg.shape), _resident(ones.shape)],
        out_specs=(pl.BlockSpec((1, lc, d), lambda bi: (bi, 0, 0)),
                   pl.BlockSpec((1, d, lc), lambda bi: (bi, 0, 0)),
                   pl.BlockSpec((1, sg, lc, LANES), lambda bi: (bi, 0, 0, 0))),
        compiler_params=_cparams(("parallel",)),
        name="ctx",
    )(ctx, modc, g1, w13, w2, gm, w_kvu, kg, ones)


def _attn_kernel(bound_ref, lam_ref, subln_ref, q_ref, k_ref, kc_ref, vt_ref, vtc_ref, o_ref, *, tk, lam_init,
                 bounded):
    q = q_ref[0]
    lc = kc_ref.shape[1]
    k_all = jnp.concatenate([k_ref[0], kc_ref[0]], axis=0)
    vt_all = jnp.concatenate([vt_ref[0], vtc_ref[0]], axis=1)
    la = k_all.shape[0]
    lane = lax.broadcasted_iota(jnp.int32, q.shape, 1)
    zero = jnp.zeros_like(q)
    q_comp = (jnp.where(lane < HEAD_DIM, q, zero), jnp.where(lane >= HEAD_DIM, q, zero))
    lv = lam_ref[...]
    lam = (jnp.exp(jnp.sum(lv[0:1] * lv[1:2], axis=-1, keepdims=True))
           - jnp.exp(jnp.sum(lv[2:3] * lv[3:4], axis=-1, keepdims=True)) + lam_init)
    blocks = [(0, la)] if bounded else [(st, tk) for st in range(0, la - lc, tk)] + [(la - lc, lc)]
    nt_dims = (((1,), (1,)), ((), ()))
    heads = []
    for qz in q_comp:
        m = l = acc = None
        for st, sz in blocks:
            kb = k_all[st:st + sz, :]
            vb = vt_all[:, st:st + sz]
            s = lax.dot_general(kb, qz, nt_dims, preferred_element_type=F32)
            if bounded:
                p = jnp.exp2(s - bound_ref[...])
                l = jnp.sum(p, axis=0, keepdims=True)
                acc = jnp.dot(vb, p.astype(BF16), preferred_element_type=F32)
                continue
            bm = jnp.max(s, axis=0, keepdims=True)
            if m is None:
                m_new = bm
                p = jnp.exp2(s - m_new)
                l = jnp.sum(p, axis=0, keepdims=True)
                acc = jnp.dot(vb, p.astype(BF16), preferred_element_type=F32)
            else:
                m_new = jnp.maximum(m, bm)
                alpha = jnp.exp2(m - m_new)
                p = jnp.exp2(s - m_new)
                l = alpha * l + jnp.sum(p, axis=0, keepdims=True)
                acc = alpha * acc + jnp.dot(vb, p.astype(BF16), preferred_element_type=F32)
            m = m_new
        heads.append(acc * (1.0 / l))
    o = heads[0] - lam * heads[1]
    ms = jnp.mean(o * o, axis=0, keepdims=True)
    on = o * lax.rsqrt(ms + EPS) * (subln_ref[...] * (1.0 - lam_init))
    o_ref[0] = on.astype(BF16)


def _attn_call(bound, lamv, subln, q, k, kc, vt, vtc, lam_init, tq, tk, bounded):
    b, l, d = q.shape
    lc = kc.shape[1]
    return pl.pallas_call(
        functools.partial(_attn_kernel, tk=tk, lam_init=lam_init, bounded=bounded),
        out_shape=jax.ShapeDtypeStruct((b, d, l), BF16),
        grid=(b, N_HEADS, l // tq),
        in_specs=[pl.BlockSpec(bound.shape, lambda bi, h, i: (0, 0)),
                  pl.BlockSpec(lamv.shape, lambda bi, h, i: (0, 0)),
                  pl.BlockSpec(subln.shape, lambda bi, h, i: (0, 0)),
                  pl.BlockSpec((1, tq, V_DIM), lambda bi, h, i: (bi, i, h)),
                  pl.BlockSpec((1, l, V_DIM), lambda bi, h, i: (bi, 0, h)),
                  pl.BlockSpec((1, lc, V_DIM), lambda bi, h, i: (bi, 0, h)),
                  pl.BlockSpec((1, V_DIM, l), lambda bi, h, i: (bi, h, 0)),
                  pl.BlockSpec((1, V_DIM, lc), lambda bi, h, i: (bi, h, 0))],
        out_specs=pl.BlockSpec((1, V_DIM, tq), lambda bi, h, i: (bi, h, i)),
        compiler_params=_cparams(("parallel", "parallel", "arbitrary")),
        name="attn_bounded" if bounded else "attn_online",
    )(bound, lamv, subln, q, k, kc, vt, vtc)


def _s5_matrices(lam_re, lam_im, log_dt, b_re, b_im, c_re, c_im, d_skip, seg):
    t = CHUNK
    g_all, p_n = lam_re.shape[1:]
    c_n = b_re.shape[-1]
    sg = g_all // SUPER
    ks = jnp.arange(t + 1, dtype=F32)[:, None, None]
    mats = []
    for dr in range(2):
        lre, lim = lam_re[dr], lam_im[dr]
        dt = jnp.exp(log_dt[dr])[:, None]
        mag = jnp.exp(ks * (lre * dt))
        akr = mag * jnp.cos(ks * (lim * dt))
        aki = mag * jnp.sin(ks * (lim * dt))
        ar, ai = akr[1], aki[1]
        den = lre * lre + lim * lim
        nr = ar - 1.0
        fr = (nr * lre + ai * lim) / den
        fi = (ai * lre - nr * lim) / den
        bbr = fr[..., None] * b_re[dr] - fi[..., None] * b_im[dr]
        bbi = fr[..., None] * b_im[dr] + fi[..., None] * b_re[dr]
        abr = akr[..., None] * bbr - aki[..., None] * bbi
        abi = akr[..., None] * bbi + aki[..., None] * bbr
        kern = (jnp.einsum('gdp,kgpc->kgdc', c_re[dr], abr)
                - jnp.einsum('gdp,kgpc->kgdc', c_im[dr], abi))
        car = c_re[dr][None] * akr[:, :, None, :] - c_im[dr][None] * aki[:, :, None, :]
        cai = c_re[dr][None] * aki[:, :, None, :] + c_im[dr][None] * akr[:, :, None, :]
        mats.append((akr, aki, abr, abi, kern, car, cai))

    s_idx = jnp.arange(t)
    kf, kb = mats[0][4], mats[1][4]
    d_diag = jnp.eye(c_n, dtype=F32)[None] * d_skip.reshape(g_all, 1, c_n)
    lagk = jnp.concatenate([kb[1:t][::-1], (kf[0] + kb[0] + d_diag)[None], kf[1:t]], axis=0)
    lagk = lagk.transpose(1, 3, 0, 2).reshape(g_all, c_n, (2 * t - 1) * c_n)
    li, ci = np.divmod(np.arange((2 * t - 1) * c_n), c_n)
    ti, cj = np.divmod(np.arange(t * c_n), c_n)
    pick = ((li[None, :, None] == ti[None, None, :] - np.arange(t)[:, None, None] + t - 1)
            & (ci[None, :, None] == cj[None, None, :]))
    m = jnp.einsum('gcx,sxy->gscy', lagk, jnp.asarray(pick, F32))
    m = m.reshape(sg, SUPER, t, c_n, t * c_n).transpose(0, 2, 1, 3, 4)
    m = m.reshape(sg, t * SUPER * c_n, t * c_n)

    eye = jnp.eye(2 * p_n, dtype=F32)

    def state_in(abr, abi, order):
        w = jnp.concatenate([abr, abi], axis=2)[order]
        w = jnp.einsum('sgxc,xy->sgcy', w, eye)
        w = w.reshape(t, sg, SUPER, c_n, 2 * p_n).transpose(1, 0, 2, 3, 4)
        return w.reshape(sg, t * SUPER * c_n, 2 * p_n)

    def state_out(car, cai, order):
        v = jnp.concatenate([car, -cai], axis=3)[order]
        v = v.transpose(1, 0, 2, 3).reshape(g_all, t * c_n, 2 * p_n)
        v = jnp.einsum('gnx,nm->gxm', v, jnp.eye(t * c_n, dtype=F32))
        v = v.reshape(sg, SUPER, 2, p_n, t * c_n).transpose(0, 2, 1, 3, 4)
        return v.reshape(sg, 2 * SUPER * p_n, t * c_n)

    wf = state_in(mats[0][2], mats[0][3], (t - 1) - s_idx)
    wb = state_in(mats[1][2], mats[1][3], s_idx)
    vf = state_out(mats[0][5], mats[0][6], s_idx + 1)
    vb = state_out(mats[1][5], mats[1][6], t - s_idx)

    def decay_table(dr, powers):
        lre, lim = lam_re[dr], lam_im[dr]
        dt = jnp.exp(log_dt[dr])[:, None]
        kk = (t * powers).astype(F32)[:, None, None]
        mag = jnp.exp(kk * (lre * dt))
        tab = jnp.stack([mag * jnp.cos(kk * (lim * dt)), mag * jnp.sin(kk * (lim * dt))], axis=0)
        return tab.reshape(2, -1, sg, SUPER * p_n).transpose(2, 0, 1, 3)

    j = jnp.arange(seg)
    bf = lambda a: a.astype(BF16)
    return (bf(m), bf(wf), bf(wb), bf(vf), bf(vb), decay_table(0, j + 1), decay_table(1, seg - j))


def _s5_segment(n):
    return 8 * (-(-n // 64))


def _cmul_add(ar, ai, xr, xi, br, bi):
    return ar * xr - ai * xi + br, ar * xi + ai * xr + bi


def _group_replicator(outer, inner):
    col = np.arange(outer * SUPER * inner)
    src = (col // (SUPER * inner)) * inner + col % inner
    return jnp.asarray(np.arange(outer * inner)[:, None] == src[None, :], dtype=BF16)


def _expand_groups(xc, rep, row_block, col_block):
    dense = jnp.dot(xc, rep, preferred_element_type=F32)
    r = lax.broadcasted_iota(jnp.int32, dense.shape, 0)
    c = lax.broadcasted_iota(jnp.int32, dense.shape, 1)
    own = (r // row_block) % SUPER == (c // col_block) % SUPER
    return jnp.where(own, dense, 0.0).astype(BF16)


def _s5_kernel(u_ref, uc_ref, mc_ref, wfc_ref, wbc_ref, vfc_ref, vbc_ref, rep_out_ref, rep_state_ref, pf_ref,
               pb_ref, y_ref, m_ref, wf_ref, wb_ref, vf_ref, vb_ref, sf_ref, sb_ref, *, nx, nc):
    @pl.when(pl.program_id(1) == 0)
    def _build_matrices():
        rep_out, rep_state = rep_out_ref[...], rep_state_ref[...]
        m_ref[...] = _expand_groups(mc_ref[0], rep_out, S5_GROUP, S5_GROUP)
        wf_ref[...] = _expand_groups(wfc_ref[0], rep_state, S5_GROUP, S5_STATE)
        wb_ref[...] = _expand_groups(wbc_ref[0], rep_state, S5_GROUP, S5_STATE)
        vf_ref[...] = _expand_groups(vfc_ref[0], rep_out, S5_STATE, S5_GROUP)
        vb_ref[...] = _expand_groups(vbc_ref[0], rep_out, S5_STATE, S5_GROUP)

    n = nx + nc
    tiles, npad = sf_ref.shape[1:3]
    seg = npad // 8
    pad = npad - n
    half = tiles * LANES

    def load(ref, rows):
        return tuple(jnp.concatenate([ref[ri, c, rows, :] for c in range(tiles)], axis=-1) for ri in range(2))

    def store(ref, rows, re, im):
        for c in range(tiles):
            ref[0, c, rows, :] = re[:, c * LANES:(c + 1) * LANES]
            ref[1, c, rows, :] = im[:, c * LANES:(c + 1) * LANES]

    chunks = lambda ref, rows: jnp.concatenate(
        [ref[0, 0, pl.ds(t, rows, stride=CHUNK), :] for t in range(CHUNK)], axis=-1).astype(BF16)
    zl, zc = chunks(u_ref, nx), chunks(uc_ref, nc)
    zpad = [jnp.zeros((pad, zl.shape[1]), BF16)] if pad else []
    s = jnp.dot(jnp.concatenate([zc, zl] + zpad, axis=0), wf_ref[...], preferred_element_type=F32)
    store(sf_ref, slice(None), s[:, :half], s[:, half:])
    s = jnp.dot(jnp.concatenate(zpad + [zl, zc], axis=0), wb_ref[...], preferred_element_type=F32)
    store(sb_ref, slice(None), s[:, :half], s[:, half:])

    afr, afi = pf_ref[0, 0, 0:1, :], pf_ref[0, 1, 0:1, :]
    abr, abi = pb_ref[0, 0, seg - 1:seg, :], pb_ref[0, 1, seg - 1:seg, :]

    def step(i, carry):
        hr, hi, gr, gi = carry
        rows = pl.ds(i, 8, stride=seg)
        hr, hi = _cmul_add(afr, afi, hr, hi, *load(sf_ref, rows))
        store(sf_ref, rows, hr, hi)
        rows = pl.ds(seg - 1 - i, 8, stride=seg)
        gr, gi = _cmul_add(abr, abi, gr, gi, *load(sb_ref, rows))
        store(sb_ref, rows, gr, gi)
        return hr, hi, gr, gi

    zero = jnp.zeros((8, half), F32)
    hr, hi, gr, gi = lax.fori_loop(0, seg, step, (zero, zero, zero, zero), unroll=True)

    tfr, tfi = pf_ref[0, 0], pf_ref[0, 1]
    tbr, tbi = pb_ref[0, 0], pb_ref[0, 1]
    cr = ci = None
    for s in range(1, 8):
        er, ei = hr[s - 1:s], hi[s - 1:s]
        if cr is None:
            cr, ci = er, ei
        else:
            cr, ci = _cmul_add(tfr[seg - 1:seg], tfi[seg - 1:seg], cr, ci, er, ei)
        rows = slice(s * seg, (s + 1) * seg)
        store(sf_ref, rows, *_cmul_add(tfr, tfi, cr, ci, *load(sf_ref, rows)))
    cr = ci = None
    for s in range(6, -1, -1):
        er, ei = gr[s + 1:s + 2], gi[s + 1:s + 2]
        if cr is None:
            cr, ci = er, ei
        else:
            cr, ci = _cmul_add(tbr[0:1], tbi[0:1], cr, ci, er, ei)
        rows = slice(s * seg, (s + 1) * seg)
        store(sb_ref, rows, *_cmul_add(tbr, tbi, cr, ci, *load(sb_ref, rows)))

    def entering(ref, shift, first):
        cols = [pltpu.roll(ref[ri, c], shift, 0)[first:first + nx] for ri in range(2) for c in range(tiles)]
        return jnp.concatenate(cols, axis=-1).astype(BF16)

    y = (jnp.dot(zl, m_ref[...], preferred_element_type=F32)
         + jnp.dot(entering(sf_ref, 1, nc), vf_ref[...], preferred_element_type=F32)
         + jnp.dot(entering(sb_ref, npad - 1, pad), vb_ref[...], preferred_element_type=F32))
    for t in range(CHUNK):
        y_ref[0, 0, pl.ds(t, nx, stride=CHUNK), :] = y[:, t * LANES:(t + 1) * LANES]


def _s5_call(u, uc, mats):
    b, sg, l, _ = u.shape
    lc = uc.shape[2]
    t = CHUNK
    nx, nc = l // t, lc // t
    npad = 8 * _s5_segment(nx + nc)
    width = t * LANES
    tiles = width // (2 * LANES)
    mc, wfc, wbc, vfc, vbc, pf, pb = mats
    rep_out = _group_replicator(t, S5_GROUP)
    rep_state = _group_replicator(2, S5_STATE)
    wspec = lambda a: pl.BlockSpec((1,) + a.shape[1:], lambda g, bi: (g,) + (0,) * (a.ndim - 1))
    dense = pltpu.VMEM((width, width), BF16)
    states = pltpu.VMEM((2, tiles, npad, LANES), F32)
    return pl.pallas_call(
        functools.partial(_s5_kernel, nx=nx, nc=nc),
        out_shape=jax.ShapeDtypeStruct((b, sg, nx * t, LANES), F32),
        grid=(sg, b),
        in_specs=[pl.BlockSpec((1, 1, l, LANES), lambda g, bi: (bi, g, 0, 0)),
                  pl.BlockSpec((1, 1, lc, LANES), lambda g, bi: (bi, g, 0, 0))]
                 + [wspec(a) for a in (mc, wfc, wbc, vfc, vbc)]
                 + [_resident(rep_out.shape), _resident(rep_state.shape), wspec(pf), wspec(pb)],
        out_specs=pl.BlockSpec((1, 1, nx * t, LANES), lambda g, bi: (bi, g, 0, 0)),
        scratch_shapes=[dense] * 5 + [states] * 2,
        compiler_params=_cparams(("arbitrary", "arbitrary")),
        name="s5",
    )(u, uc, mc, wfc, wbc, vfc, vbc, rep_out, rep_state, pf, pb)


def _gelu_tanh(y):
    return 0.5 * y * (1.0 + jnp.tanh(math.sqrt(2.0 / math.pi) * (y + 0.044715 * (y * y * y))))


def _merge_ffn_kernel(x_ref, mod_ref, y_ref, oxt_ref, ga_ref, gs_ref, wglu_ref, bglu_ref, wpa_ref, wps_ref,
                      wout_ref, g2_ref, w13_ref, w2_ref, o_ref, *, base):
    mod = mod_ref[0]
    for r in range(x_ref.shape[1] // FFN_ROWS):
        rs = slice(r * FFN_ROWS, (r + 1) * FFN_ROWS)
        ge =_gelu_tanh(jnp.concatenate([y_ref[0, g, rs, :] for g in range(y_ref.shape[1])], axis=-1))
        z = jnp.dot(ge.astype(BF16), wglu_ref[...], preferred_element_type=F32) + bglu_ref[...]
        sx = (ge * jax.nn.sigmoid(z)).astype(BF16)
        attn = lax.dot_general(oxt_ref[0, :, rs], wpa_ref[...], (((0,), (0,)), ((), ())),
                               preferred_element_type=F32)
        mix = (ga_ref[0, rs, :].astype(F32) * attn
               + gs_ref[0, rs, :].astype(F32) * jnp.dot(sx, wps_ref[...], preferred_element_type=F32))
        out = jnp.dot(mix.astype(BF16), wout_ref[...], preferred_element_type=F32)
        x2 = x_ref[0, rs, :] + mod[base + 2:base + 3, :] * out
        o_ref[0, rs, :] = _ffn_rows(x2, g2_ref[...], mod, base + 3, w13_ref, w2_ref)


def _merge_ffn_call(x, mod, y, oxt, ga, gs, w_glu, b_glu, w_pa, w_ps, w_out, g2, w13, w2, base, tm):
    b, l, d = x.shape
    sg = y.shape[1]
    tok = lambda width: pl.BlockSpec((1, tm, width), lambda bi, i: (bi, i, 0))
    weights = (w_glu, b_glu, w_pa, w_ps, w_out, g2, w13, w2)
    return pl.pallas_call(
        functools.partial(_merge_ffn_kernel, base=base),
        out_shape=jax.ShapeDtypeStruct(x.shape, F32),
        grid=(b, l // tm),
        in_specs=[tok(d), pl.BlockSpec((1, N_MOD, d), lambda bi, i: (bi, 0, 0)),
                  pl.BlockSpec((1, sg, tm, LANES), lambda bi, i: (bi, 0, i, 0)),
                  pl.BlockSpec((1, d, tm), lambda bi, i: (bi, 0, i)), tok(d), tok(d)]
                 + [_resident(w.shape) for w in weights],
        out_specs=tok(d),
        compiler_params=_cparams(("parallel", "parallel")),
        name="merge_ffn2",
    )(x, mod, y, oxt, ga, gs, *weights)


def _rope_tables(l):
    n_freq = HEAD_DIM // 4
    inv = ROPE_BASE ** (-jnp.arange(n_freq, dtype=F32) / n_freq)
    pos = jnp.arange(l, dtype=jnp.int32)
    row = (pos // GRID_W).astype(F32)
    col = (pos % GRID_W).astype(F32)
    ang = jnp.concatenate([row[:, None] * inv, col[:, None] * inv], axis=-1)
    cos, sin = jnp.cos(ang), jnp.sin(ang)
    return jnp.concatenate([cos] * 4, axis=-1), jnp.concatenate([-sin, sin, -sin, sin], axis=-1)


def _group_ones():
    grp = np.arange(MXU_WIDTH) // HEAD_DIM
    return jnp.asarray(grp[:, None] == grp[None, :], dtype=BF16)


def kernel(x, c, ctx, c_ctx, w_mod, b_mod, norm_ffn1, w13_ffn1, w2_ffn1, norm_mix, w_in, q_norm, k_norm, lam_q1, lam_k1, lam_q2, lam_k2, subln, s5_lam_re, s5_lam_im, s5_log_dt, s5_b_re, s5_b_im, s5_c_re, s5_c_im, s5_d, w_glu, b_glu, w_pa, w_ps, w_out, norm_ffn2, w13_ffn2, w2_ffn2):
    assert w_mod.shape[0] == 1, "single-layer block"
    b, l, d = x.shape
    lc = ctx.shape[1]
    assert l % max(TILE_FFN1, TILE_PROJ, TILE_MERGE, TILE_Q_BOUNDED, TILE_K_ONLINE) == 0 and l % lc == 0
    assert lc % (2 * CHUNK) == 0 and d == N_HEADS * V_DIM
    lam_init = 0.8 - 0.6 * math.exp(-0.3 * 0)
    bf = lambda a: a.astype(BF16)
    row = lambda a: a.reshape(1, -1)

    rows = b + 1
    pad = (-rows) % 8
    cc = jnp.concatenate([c, c_ctx[None, :], jnp.zeros((pad, d), F32)], axis=0)
    mod = _mod_call(cc, w_mod[0], row(b_mod[0]))
    modx = mod[:b].reshape(b, N_MOD, d)
    modc = mod[b:b + 1].reshape(1, N_MOD, d)

    w13_1, w2_1 = bf(w13_ffn1[0]), bf(w2_ffn1[0])
    x1 = _ffn_call(x, modx, row(norm_ffn1[0]), w13_1, w2_1, 0, TILE_FFN1, "ffn1_x")

    ones = _group_ones()
    cos, sin = _rope_tables(l)
    qg = row(jnp.tile(q_norm[0], d // HEAD_DIM)) * (LOG2E / math.sqrt(HEAD_DIM))
    kg = row(jnp.tile(k_norm[0], d // HEAD_DIM))
    w_inb = bf(w_in[0])
    s5w = s5_d.shape[-1]
    q, k, vt, u, ga, gs = _inproj_call(x1, modx, row(norm_mix[0]), w_inb, qg, kg, ones,
                                       cos, sin, s5w, 3, TILE_PROJ)
    kc, vtc, uc = _ctx_call(ctx, modc, row(norm_ffn1[0]), w13_1, w2_1, row(norm_mix[0]),
                            w_inb[:, d:3 * d + s5w], kg, ones, s5w, 3)

    lamv = jnp.stack([lam_q1[0], lam_k1[0], lam_q2[0], lam_k2[0]], axis=0)
    bound = (LOG2E * math.sqrt(HEAD_DIM)) * jnp.max(jnp.abs(q_norm[0])) * jnp.max(jnp.abs(k_norm[0]))
    attn = lambda bounded, tq, tk: functools.partial(
        _attn_call, lamv=lamv, subln=subln[0].reshape(-1, 1), q=q, k=k, kc=kc, vt=vt, vtc=vtc,
        lam_init=lam_init, tq=tq, tk=tk, bounded=bounded)
    oxt = lax.cond(2.0 * bound <= MAX_EXP2_SPAN, attn(True, TILE_Q_BOUNDED, None),
                   attn(False, TILE_Q_ONLINE, TILE_K_ONLINE), bound.reshape(1, 1))

    mats = _s5_matrices(s5_lam_re[0], s5_lam_im[0], s5_log_dt[0], s5_b_re[0], s5_b_im[0],
                        s5_c_re[0], s5_c_im[0], s5_d[0], _s5_segment((l + lc) // CHUNK))
    y = _s5_call(u, uc, mats)

    return _merge_ffn_call(x1, modx, y, oxt, ga, gs, bf(w_glu[0]), row(b_glu[0]), bf(w_pa[0]), bf(w_ps[0]),
                           bf(w_out[0]), row(norm_ffn2[0]), bf(w13_ffn2[0]), bf(w2_ffn2[0]), 3, TILE_MERGE)
```

```python
import functools
import math

import jax
import jax.numpy as jnp
import numpy as np
from jax import lax
from jax.experimental import pallas as pl
from jax.experimental.pallas import tpu as pltpu

F32 = jnp.float32
BF16 = jnp.bfloat16

EPS = 1e-6
ROPE_BASE = 10000.0
GRID_W = 64
N_HEADS = 8
HEAD_DIM = 64
V_DIM = 2 * HEAD_DIM
S5_GROUP = 16
S5_STATE = 64
N_MOD = 9
LANES = 128
MXU_WIDTH = 256
CHUNK = 8
SUPER = LANES // S5_GROUP
VMEM_CAPACITY = 64 * 1024 * 1024
VMEM_LIMIT = VMEM_CAPACITY - 8 * 1024 * 1024
LOG2E = math.log2(math.e)
MAX_EXP2_SPAN = 100.0

FFN_ROWS = 256
PROJ_ROWS = 256
TILE_FFN1 = 1024
TILE_PROJ = 1024
TILE_MERGE = 512
TILE_Q_BOUNDED = 1024
ATTN_HEADS_PER_STEP = 2
TILE_Q_ONLINE, TILE_K_ONLINE = 256, 1024


def _cparams(sem):
    return pltpu.CompilerParams(dimension_semantics=sem, vmem_limit_bytes=VMEM_LIMIT)


def _resident(shape):
    nd = len(shape)
    return pl.BlockSpec(shape, lambda *_: (0,) * nd, pipeline_mode=pl.Buffered(1))


def _silu(a):
    return a * jax.nn.sigmoid(a)


def _modulated(x, g, mod, base):
    ms = jnp.mean(x * x, axis=-1, keepdims=True)
    xn = x * lax.rsqrt(ms + EPS) * g
    return xn * (1.0 + mod[base + 1:base + 2, :]) + mod[base:base + 1, :]


def _mod_kernel(c_ref, w_ref, b_ref, o_ref):
    a = _silu(c_ref[...]).astype(BF16)
    o_ref[...] = jnp.dot(a, w_ref[...].astype(BF16), preferred_element_type=F32) + b_ref[...]


def _mod_call(cc, w_mod, b_mod):
    rows, d = cc.shape
    n = w_mod.shape[1]
    tn = d
    return pl.pallas_call(
        _mod_kernel,
        out_shape=jax.ShapeDtypeStruct((rows, n), F32),
        grid=(n // tn,),
        in_specs=[pl.BlockSpec((rows, d), lambda j: (0, 0)),
                  pl.BlockSpec((d, tn), lambda j: (0, j)),
                  pl.BlockSpec((1, tn), lambda j: (0, j))],
        out_specs=pl.BlockSpec((rows, tn), lambda j: (0, j)),
        compiler_params=_cparams(("arbitrary",)),
        name="mod",
    )(cc, w_mod, b_mod)


def _ffn_rows(x, g, mod, base, w13_ref, w2_ref):
    dff = w2_ref.shape[0]
    xm = _modulated(x, g, mod, base).astype(BF16)
    h = jnp.dot(xm, w13_ref[...], preferred_element_type=F32)
    act = (_silu(h[:, :dff]) * h[:, dff:]).astype(BF16)
    return x + (0.5 * mod[base + 2:base + 3, :]) * jnp.dot(act, w2_ref[...], preferred_element_type=F32)


def _ffn_kernel(x_ref, mod_ref, g_ref, w13_ref, w2_ref, o_ref, *, base):
    for r in range(x_ref.shape[1] // FFN_ROWS):
        rs = slice(r * FFN_ROWS, (r + 1) * FFN_ROWS)
        o_ref[0, rs, :] = _ffn_rows(x_ref[0, rs, :], g_ref[...], mod_ref[0], base, w13_ref, w2_ref)


def _ffn_call(x, mod, g, w13, w2, base, tm, name):
    b, l, d = x.shape
    return pl.pallas_call(
        functools.partial(_ffn_kernel, base=base),
        out_shape=jax.ShapeDtypeStruct(x.shape, F32),
        grid=(b, l // tm),
        in_specs=[pl.BlockSpec((1, tm, d), lambda bi, i: (bi, i, 0)),
                  pl.BlockSpec((1, N_MOD, d), lambda bi, i: (bi, 0, 0)),
                  _resident(g.shape), _resident(w13.shape), _resident(w2.shape)],
        out_specs=pl.BlockSpec((1, tm, d), lambda bi, i: (bi, i, 0)),
        compiler_params=_cparams(("parallel", "parallel")),
        name=name,
    )(x, mod, g, w13, w2)


def _qk_norm(t, gain, ones_ref):
    sq = (t * t).astype(BF16)
    w = ones_ref.shape[0]
    ss = jnp.concatenate([jnp.dot(sq[:, j:j + w], ones_ref[...], preferred_element_type=F32)
                          for j in range(0, t.shape[-1], w)], axis=-1)
    return t * lax.rsqrt(ss * (1.0 / HEAD_DIM) + EPS) * gain


def _rope(t, cos, sin_signed):
    n = t.shape[-1]
    half = HEAD_DIM // 2
    lane = lax.broadcasted_iota(jnp.int32, t.shape, 1)
    swapped = jnp.where(lane % HEAD_DIM < half, pltpu.roll(t, n - half, 1), pltpu.roll(t, half, 1))
    reps = n // cos.shape[-1]
    cosf = jnp.concatenate([cos] * reps, axis=-1)
    sinf = jnp.concatenate([sin_signed] * reps, axis=-1)
    return t * cosf + swapped * sinf


def _store_supergroups(u_ref, rs, u):
    for g in range(u_ref.shape[1]):
        u_ref[0, g, rs, :] = u[:, g * LANES:(g + 1) * LANES]


def _inproj_kernel(x_ref, mod_ref, g_ref, w_ref, qg_ref, kg_ref, ones_ref, cos_ref, sin_ref,
                   q_ref, k_ref, vt_ref, u_ref, ga_ref, gs_ref, *, base):
    d = x_ref.shape[-1]
    qc, kc, vc = d, 2 * d, 3 * d
    uc = vc + u_ref.shape[1] * LANES
    for r in range(x_ref.shape[1] // PROJ_ROWS):
        rs = slice(r * PROJ_ROWS, (r + 1) * PROJ_ROWS)
        xm = _modulated(x_ref[0, rs, :], g_ref[...], mod_ref[0], base).astype(BF16)
        proj = lambda lo, hi: jnp.dot(xm, w_ref[:, lo:hi], preferred_element_type=F32)
        cos = cos_ref[rs, :]
        sin = sin_ref[rs, :]
        tq = proj(0, qc)
        tk = proj(qc, kc)
        q_ref[0, rs, :] = _rope(_qk_norm(tq, qg_ref[...], ones_ref), cos, sin).astype(BF16)
        vt_ref[0, :, rs] = proj(kc, vc).astype(BF16).T
        k_ref[0, rs, :] = _rope(_qk_norm(tk, kg_ref[...], ones_ref), cos, sin).astype(BF16)
        _store_supergroups(u_ref, rs, proj(vc, uc))
        ga_ref[0, rs, :] = jax.nn.sigmoid(proj(uc, uc + d)).astype(BF16)
        gs_ref[0, rs, :] = jax.nn.sigmoid(proj(uc + d, uc + 2 * d)).astype(BF16)


def _inproj_call(x, modx, g, w_in, qg, kg, ones, cos, sin, s5w, base, tm):
    b, l, d = x.shape
    tok = lambda bi, i: (bi, i, 0)
    out_shape = (jax.ShapeDtypeStruct((b, l, d), BF16),
                 jax.ShapeDtypeStruct((b, l, d), BF16),
                 jax.ShapeDtypeStruct((b, d, l), BF16),
                 jax.ShapeDtypeStruct((b, s5w // LANES, l, LANES), F32),
                 jax.ShapeDtypeStruct((b, l, d), BF16),
                 jax.ShapeDtypeStruct((b, l, d), BF16))
    return pl.pallas_call(
        functools.partial(_inproj_kernel, base=base),
        out_shape=out_shape,
        grid=(b, l // tm),
        in_specs=[pl.BlockSpec((1, tm, d), tok),
                  pl.BlockSpec((1, N_MOD, d), lambda bi, i: (bi, 0, 0)),
                  _resident(g.shape), _resident(w_in.shape), _resident(qg.shape), _resident(kg.shape),
                  _resident(ones.shape),
                  pl.BlockSpec((tm, LANES), lambda bi, i: (i, 0)),
                  pl.BlockSpec((tm, LANES), lambda bi, i: (i, 0))],
        out_specs=(pl.BlockSpec((1, tm, d), tok),
                   pl.BlockSpec((1, tm, d), tok),
                   pl.BlockSpec((1, d, tm), lambda bi, i: (bi, 0, i)),
                   pl.BlockSpec((1, s5w // LANES, tm, LANES), lambda bi, i: (bi, 0, i, 0)),
                   pl.BlockSpec((1, tm, d), tok),
                   pl.BlockSpec((1, tm, d), tok)),
        compiler_params=_cparams(("parallel", "parallel")),
        name="inproj",
    )(x, modx, g, w_in, qg, kg, ones, cos, sin)


def _ctx_kernel(ctx_ref, mod_ref, g1_ref, w13_ref, w2_ref, gm_ref, w_ref, kg_ref, ones_ref,
                k_ref, vt_ref, u_ref, *, base):
    d = ctx_ref.shape[-1]
    mod = mod_ref[0]
    x1 = _ffn_rows(ctx_ref[0], g1_ref[...], mod, 0, w13_ref, w2_ref)
    xm = _modulated(x1, gm_ref[...], mod, base).astype(BF16)
    p = jnp.dot(xm, w_ref[...], preferred_element_type=F32)
    k_ref[0] = _qk_norm(p[:, :d], kg_ref[...], ones_ref).astype(BF16)
    vt_ref[0] = p[:, d:2 * d].astype(BF16).T
    _store_supergroups(u_ref, slice(None), p[:, 2 * d:])


def _ctx_call(ctx, modc, g1, w13, w2, gm, w_kvu, kg, ones, s5w, base):
    b, lc, d = ctx.shape
    sg = s5w // LANES
    return pl.pallas_call(
        functools.partial(_ctx_kernel, base=base),
        out_shape=(jax.ShapeDtypeStruct((b, lc, d), BF16), jax.ShapeDtypeStruct((b, d, lc), BF16),
                   jax.ShapeDtypeStruct((b, sg, lc, LANES), F32)),
        grid=(b,),
        in_specs=[pl.BlockSpec((1, lc, d), lambda bi: (bi, 0, 0)),
                  pl.BlockSpec((1, N_MOD, d), lambda bi: (0, 0, 0)),
                  _resident(g1.shape), _resident(w13.shape), _resident(w2.shape), _resident(gm.shape),
                  _resident(w_kvu.shape), _resident(kg.shape), _resident(ones.shape)],
        out_specs=(pl.BlockSpec((1, lc, d), lambda bi: (bi, 0, 0)),
                   pl.BlockSpec((1, d, lc), lambda bi: (bi, 0, 0)),
                   pl.BlockSpec((1, sg, lc, LANES), lambda bi: (bi, 0, 0, 0))),
        compiler_params=_cparams(("parallel",)),
        name="ctx",
    )(ctx, modc, g1, w13, w2, gm, w_kvu, kg, ones)


def _attn_kernel(bound_ref, lam_ref, subln_ref, q_ref, k_ref, kc_ref, vt_ref, vtc_ref, o_ref, *, tk, lam_init,
                 bounded):
    lv = lam_ref[...]
    lam = (jnp.exp(jnp.sum(lv[0:1] * lv[1:2], axis=-1, keepdims=True))
           - jnp.exp(jnp.sum(lv[2:3] * lv[3:4], axis=-1, keepdims=True)) + lam_init)
    for hh in range(q_ref.shape[2] // V_DIM):
        hs = slice(hh * V_DIM, (hh + 1) * V_DIM)
        k_all = jnp.concatenate([k_ref[0, :, hs], kc_ref[0, :, hs]], axis=0)
        vt_all = jnp.concatenate([vt_ref[0, hs, :], vtc_ref[0, hs, :]], axis=1)
        o_ref[0, hs, :] = _attn_head(q_ref[0, :, hs], k_all, vt_all, kc_ref.shape[1], lam, bound_ref, subln_ref,
                                     tk, lam_init, bounded)


def _attn_head(q, k_all, vt_all, lc, lam, bound_ref, subln_ref, tk, lam_init, bounded):
    la = k_all.shape[0]
    lane = lax.broadcasted_iota(jnp.int32, q.shape, 1)
    zero = jnp.zeros_like(q)
    q_comp = (jnp.where(lane < HEAD_DIM, q, zero), jnp.where(lane >= HEAD_DIM, q, zero))
    blocks = [(0, la)] if bounded else [(st, tk) for st in range(0, la - lc, tk)] + [(la - lc, lc)]
    nt_dims = (((1,), (1,)), ((), ()))
    heads = []
    for qz in q_comp:
        m = l = acc = None
        for st, sz in blocks:
            kb = k_all[st:st + sz, :]
            vb = vt_all[:, st:st + sz]
            s = lax.dot_general(kb, qz, nt_dims, preferred_element_type=F32)
            if bounded:
                p = jnp.exp2(s - bound_ref[...])
                l = jnp.sum(p, axis=0, keepdims=True)
                acc = jnp.dot(vb, p.astype(BF16), preferred_element_type=F32)
                continue
            bm = jnp.max(s, axis=0, keepdims=True)
            if m is None:
                m_new = bm
                p = jnp.exp2(s - m_new)
                l = jnp.sum(p, axis=0, keepdims=True)
                acc = jnp.dot(vb, p.astype(BF16), preferred_element_type=F32)
            else:
                m_new = jnp.maximum(m, bm)
                alpha = jnp.exp2(m - m_new)
                p = jnp.exp2(s - m_new)
                l = alpha * l + jnp.sum(p, axis=0, keepdims=True)
                acc = alpha * acc + jnp.dot(vb, p.astype(BF16), preferred_element_type=F32)
            m = m_new
        heads.append(acc * (1.0 / l))
    o = heads[0] - lam * heads[1]
    ms = jnp.mean(o * o, axis=0, keepdims=True)
    on = o * lax.rsqrt(ms + EPS) * (subln_ref[...] * (1.0 - lam_init))
    return on.astype(BF16)


def _attn_call(bound, lamv, subln, q, k, kc, vt, vtc, lam_init, tq, tk, bounded):
    b, l, d = q.shape
    lc = kc.shape[1]
    hw = ATTN_HEADS_PER_STEP * V_DIM
    return pl.pallas_call(
        functools.partial(_attn_kernel, tk=tk, lam_init=lam_init, bounded=bounded),
        out_shape=jax.ShapeDtypeStruct((b, d, l), BF16),
        grid=(b, N_HEADS // ATTN_HEADS_PER_STEP, l // tq),
        in_specs=[pl.BlockSpec(bound.shape, lambda bi, h, i: (0, 0)),
                  pl.BlockSpec(lamv.shape, lambda bi, h, i: (0, 0)),
                  pl.BlockSpec(subln.shape, lambda bi, h, i: (0, 0)),
                  pl.BlockSpec((1, tq, hw), lambda bi, h, i: (bi, i, h)),
                  pl.BlockSpec((1, l, hw), lambda bi, h, i: (bi, 0, h)),
                  pl.BlockSpec((1, lc, hw), lambda bi, h, i: (bi, 0, h)),
                  pl.BlockSpec((1, hw, l), lambda bi, h, i: (bi, h, 0)),
                  pl.BlockSpec((1, hw, lc), lambda bi, h, i: (bi, h, 0))],
        out_specs=pl.BlockSpec((1, hw, tq), lambda bi, h, i: (bi, h, i)),
        compiler_params=_cparams(("parallel", "parallel", "arbitrary")),
        name="attn_bounded" if bounded else "attn_online",
    )(bound, lamv, subln, q, k, kc, vt, vtc)


def _s5_matrices(lam_re, lam_im, log_dt, b_re, b_im, c_re, c_im, d_skip, seg):
    t = CHUNK
    g_all, p_n = lam_re.shape[1:]
    c_n = b_re.shape[-1]
    sg = g_all // SUPER
    ks = jnp.arange(t + 1, dtype=F32)[:, None, None]
    mats = []
    for dr in range(2):
        lre, lim = lam_re[dr], lam_im[dr]
        dt = jnp.exp(log_dt[dr])[:, None]
        mag = jnp.exp(ks * (lre * dt))
        akr = mag * jnp.cos(ks * (lim * dt))
        aki = mag * jnp.sin(ks * (lim * dt))
        ar, ai = akr[1], aki[1]
        den = lre * lre + lim * lim
        nr = ar - 1.0
        fr = (nr * lre + ai * lim) / den
        fi = (ai * lre - nr * lim) / den
        bbr = fr[..., None] * b_re[dr] - fi[..., None] * b_im[dr]
        bbi = fr[..., None] * b_im[dr] + fi[..., None] * b_re[dr]
        abr = akr[..., None] * bbr - aki[..., None] * bbi
        abi = akr[..., None] * bbi + aki[..., None] * bbr
        kern = (jnp.einsum('gdp,kgpc->kgdc', c_re[dr], abr)
                - jnp.einsum('gdp,kgpc->kgdc', c_im[dr], abi))
        car = c_re[dr][None] * akr[:, :, None, :] - c_im[dr][None] * aki[:, :, None, :]
        cai = c_re[dr][None] * aki[:, :, None, :] + c_im[dr][None] * akr[:, :, None, :]
        mats.append((akr, aki, abr, abi, kern, car, cai))

    s_idx = jnp.arange(t)
    kf, kb = mats[0][4], mats[1][4]
    d_diag = jnp.eye(c_n, dtype=F32)[None] * d_skip.reshape(g_all, 1, c_n)
    lagk = jnp.concatenate([kb[1:t][::-1], (kf[0] + kb[0] + d_diag)[None], kf[1:t]], axis=0)
    lagk = lagk.transpose(1, 3, 0, 2).reshape(g_all, c_n, (2 * t - 1) * c_n)
    li, ci = np.divmod(np.arange((2 * t - 1) * c_n), c_n)
    ti, cj = np.divmod(np.arange(t * c_n), c_n)
    pick = ((li[None, :, None] == ti[None, None, :] - np.arange(t)[:, None, None] + t - 1)
            & (ci[None, :, None] == cj[None, None, :]))
    m = jnp.einsum('gcx,sxy->gscy', lagk, jnp.asarray(pick, F32))
    m = m.reshape(sg, SUPER, t, c_n, t * c_n).transpose(0, 2, 1, 3, 4)
    m = m.reshape(sg, t * SUPER * c_n, t * c_n)

    def state_in(abr, abi, order):
        wr = abr[order].reshape(t, sg, SUPER, p_n, c_n)
        wi = abi[order].reshape(t, sg, SUPER, p_n, c_n)
        w = jnp.stack([wr, wi], axis=0)
        return w.transpose(2, 1, 3, 5, 0, 4).reshape(sg, t * SUPER * c_n, 2 * p_n)

    def state_out(car, cai, order):
        vr = car[order].reshape(t, sg, SUPER, c_n, p_n)
        vi = -cai[order].reshape(t, sg, SUPER, c_n, p_n)
        v = jnp.stack([vr, vi], axis=0)
        return v.transpose(2, 0, 3, 5, 1, 4).reshape(sg, 2 * SUPER * p_n, t * c_n)

    wf = state_in(mats[0][2], mats[0][3], (t - 1) - s_idx)
    wb = state_in(mats[1][2], mats[1][3], s_idx)
    vf = state_out(mats[0][5], mats[0][6], s_idx + 1)
    vb = state_out(mats[1][5], mats[1][6], t - s_idx)

    def decay_table(dr, powers):
        lre, lim = lam_re[dr], lam_im[dr]
        dt = jnp.exp(log_dt[dr])[:, None]
        kk = (t * powers).astype(F32)[:, None, None]
        mag = jnp.exp(kk * (lre * dt))
        tab = jnp.stack([mag * jnp.cos(kk * (lim * dt)), mag * jnp.sin(kk * (lim * dt))], axis=0)
        return tab.reshape(2, -1, sg, SUPER * p_n).transpose(2, 0, 1, 3)

    j = jnp.arange(seg)
    bf = lambda a: a.astype(BF16)
    return (bf(m), bf(wf), bf(wb), bf(vf), bf(vb), decay_table(0, j + 1), decay_table(1, seg - j))


def _s5_segment(n):
    return 8 * (-(-n // 64))


def _cmul_add(ar, ai, xr, xi, br, bi):
    return ar * xr - ai * xi + br, ar * xi + ai * xr + bi


def _group_replicator(outer, inner):
    col = np.arange(outer * SUPER * inner)
    src = (col // (SUPER * inner)) * inner + col % inner
    return jnp.asarray(np.arange(outer * inner)[:, None] == src[None, :], dtype=BF16)


def _expand_groups(xc, rep, row_block, col_block):
    dense = jnp.dot(xc, rep, preferred_element_type=F32)
    r = lax.broadcasted_iota(jnp.int32, dense.shape, 0)
    c = lax.broadcasted_iota(jnp.int32, dense.shape, 1)
    own = (r // row_block) % SUPER == (c // col_block) % SUPER
    return jnp.where(own, dense, 0.0).astype(BF16)


def _s5_kernel(u_ref, uc_ref, mc_ref, wfc_ref, wbc_ref, vfc_ref, vbc_ref, rep_out_ref, rep_state_ref, pf_ref,
               pb_ref, y_ref, m_ref, wf_ref, wb_ref, vf_ref, vb_ref, sf_ref, sb_ref, *, nx, nc):
    @pl.when(pl.program_id(1) == 0)
    def _build_matrices():
        rep_out, rep_state = rep_out_ref[...], rep_state_ref[...]
        m_ref[...] = _expand_groups(mc_ref[0], rep_out, S5_GROUP, S5_GROUP)
        wf_ref[...] = _expand_groups(wfc_ref[0], rep_state, S5_GROUP, S5_STATE)
        wb_ref[...] = _expand_groups(wbc_ref[0], rep_state, S5_GROUP, S5_STATE)
        vf_ref[...] = _expand_groups(vfc_ref[0], rep_out, S5_STATE, S5_GROUP)
        vb_ref[...] = _expand_groups(vbc_ref[0], rep_out, S5_STATE, S5_GROUP)

    n = nx + nc
    tiles, npad = sf_ref.shape[1:3]
    seg = npad // 8
    pad = npad - n
    half = tiles * LANES

    def load(ref, rows):
        return tuple(jnp.concatenate([ref[ri, c, rows, :] for c in range(tiles)], axis=-1) for ri in range(2))

    def store(ref, rows, re, im):
        for c in range(tiles):
            ref[0, c, rows, :] = re[:, c * LANES:(c + 1) * LANES]
            ref[1, c, rows, :] = im[:, c * LANES:(c + 1) * LANES]

    chunks = lambda ref, rows: jnp.concatenate(
        [ref[0, 0, pl.ds(t, rows, stride=CHUNK), :] for t in range(CHUNK)], axis=-1).astype(BF16)
    zl, zc = chunks(u_ref, nx), chunks(uc_ref, nc)
    zpad = [jnp.zeros((pad, zl.shape[1]), BF16)] if pad else []
    s = jnp.dot(jnp.concatenate([zc, zl] + zpad, axis=0), wf_ref[...], preferred_element_type=F32)
    store(sf_ref, slice(None), s[:, :half], s[:, half:])
    s = jnp.dot(jnp.concatenate(zpad + [zl, zc], axis=0), wb_ref[...], preferred_element_type=F32)
    store(sb_ref, slice(None), s[:, :half], s[:, half:])

    afr, afi = pf_ref[0, 0, 0:1, :], pf_ref[0, 1, 0:1, :]
    abr, abi = pb_ref[0, 0, seg - 1:seg, :], pb_ref[0, 1, seg - 1:seg, :]

    def step(i, carry):
        hr, hi, gr, gi = carry
        rows = pl.ds(i, 8, stride=seg)
        hr, hi = _cmul_add(afr, afi, hr, hi, *load(sf_ref, rows))
        store(sf_ref, rows, hr, hi)
        rows = pl.ds(seg - 1 - i, 8, stride=seg)
        gr, gi = _cmul_add(abr, abi, gr, gi, *load(sb_ref, rows))
        store(sb_ref, rows, gr, gi)
        return hr, hi, gr, gi

    zero = jnp.zeros((8, half), F32)
    hr, hi, gr, gi = lax.fori_loop(0, seg, step, (zero, zero, zero, zero), unroll=True)

    tfr, tfi = pf_ref[0, 0], pf_ref[0, 1]
    tbr, tbi = pb_ref[0, 0], pb_ref[0, 1]
    cr = ci = None
    for s in range(1, 8):
        er, ei = hr[s - 1:s], hi[s - 1:s]
        if cr is None:
            cr, ci = er, ei
        else:
            cr, ci = _cmul_add(tfr[seg - 1:seg], tfi[seg - 1:seg], cr, ci, er, ei)
        rows = slice(s * seg, (s + 1) * seg)
        store(sf_ref, rows, *_cmul_add(tfr, tfi, cr, ci, *load(sf_ref, rows)))
    cr = ci = None
    for s in range(6, -1, -1):
        er, ei = gr[s + 1:s + 2], gi[s + 1:s + 2]
        if cr is None:
            cr, ci = er, ei
        else:
            cr, ci = _cmul_add(tbr[0:1], tbi[0:1], cr, ci, er, ei)
        rows = slice(s * seg, (s + 1) * seg)
        store(sb_ref, rows, *_cmul_add(tbr, tbi, cr, ci, *load(sb_ref, rows)))

    def entering(ref, shift, first):
        cols = [pltpu.roll(ref[ri, c], shift, 0)[first:first + nx] for ri in range(2) for c in range(tiles)]
        return jnp.concatenate(cols, axis=-1).astype(BF16)

    y = (jnp.dot(zl, m_ref[...], preferred_element_type=F32)
         + jnp.dot(entering(sf_ref, 1, nc), vf_ref[...], preferred_element_type=F32)
         + jnp.dot(entering(sb_ref, npad - 1, pad), vb_ref[...], preferred_element_type=F32))
    for t in range(CHUNK):
        y_ref[0, 0, pl.ds(t, nx, stride=CHUNK), :] = y[:, t * LANES:(t + 1) * LANES]


def _s5_call(u, uc, mats):
    b, sg, l, _ = u.shape
    lc = uc.shape[2]
    t = CHUNK
    nx, nc = l // t, lc // t
    npad = 8 * _s5_segment(nx + nc)
    width = t * LANES
    tiles = width // (2 * LANES)
    mc, wfc, wbc, vfc, vbc, pf, pb = mats
    rep_out = _group_replicator(t, S5_GROUP)
    rep_state = _group_replicator(2, S5_STATE)
    wspec = lambda a: pl.BlockSpec((1,) + a.shape[1:], lambda g, bi: (g,) + (0,) * (a.ndim - 1))
    dense = pltpu.VMEM((width, width), BF16)
    states = pltpu.VMEM((2, tiles, npad, LANES), F32)
    return pl.pallas_call(
        functools.partial(_s5_kernel, nx=nx, nc=nc),
        out_shape=jax.ShapeDtypeStruct((b, sg, nx * t, LANES), F32),
        grid=(sg, b),
        in_specs=[pl.BlockSpec((1, 1, l, LANES), lambda g, bi: (bi, g, 0, 0)),
                  pl.BlockSpec((1, 1, lc, LANES), lambda g, bi: (bi, g, 0, 0))]
                 + [wspec(a) for a in (mc, wfc, wbc, vfc, vbc)]
                 + [_resident(rep_out.shape), _resident(rep_state.shape), wspec(pf), wspec(pb)],
        out_specs=pl.BlockSpec((1, 1, nx * t, LANES), lambda g, bi: (bi, g, 0, 0)),
        scratch_shapes=[dense] * 5 + [states] * 2,
        compiler_params=_cparams(("arbitrary", "arbitrary")),
        name="s5",
    )(u, uc, mc, wfc, wbc, vfc, vbc, rep_out, rep_state, pf, pb)


def _gelu_tanh(y):
    return 0.5 * y * (1.0 + jnp.tanh(math.sqrt(2.0 / math.pi) * (y + 0.044715 * (y * y * y))))


def _merge_ffn_kernel(x_ref, mod_ref, y_ref, oxt_ref, ga_ref, gs_ref, wglu_ref, bglu_ref, wpa_ref, wps_ref,
                      wout_ref, g2_ref, w13_ref, w2_ref, o_ref, *, base):
    mod = mod_ref[0]
    for r in range(x_ref.shape[1] // FFN_ROWS):
        rs = slice(r * FFN_ROWS, (r + 1) * FFN_ROWS)
        ge =_gelu_tanh(jnp.concatenate([y_ref[0, g, rs, :] for g in range(y_ref.shape[1])], axis=-1))
        z = jnp.dot(ge.astype(BF16), wglu_ref[...], preferred_element_type=F32) + bglu_ref[...]
        sx = (ge * jax.nn.sigmoid(z)).astype(BF16)
        attn = lax.dot_general(oxt_ref[0, :, rs], wpa_ref[...], (((0,), (0,)), ((), ())),
                               preferred_element_type=F32)
        mix = (ga_ref[0, rs, :].astype(F32) * attn
               + gs_ref[0, rs, :].astype(F32) * jnp.dot(sx, wps_ref[...], preferred_element_type=F32))
        out = jnp.dot(mix.astype(BF16), wout_ref[...], preferred_element_type=F32)
        x2 = x_ref[0, rs, :] + mod[base + 2:base + 3, :] * out
        o_ref[0, rs, :] = _ffn_rows(x2, g2_ref[...], mod, base + 3, w13_ref, w2_ref)


def _merge_ffn_call(x, mod, y, oxt, ga, gs, w_glu, b_glu, w_pa, w_ps, w_out, g2, w13, w2, base, tm):
    b, l, d = x.shape
    sg = y.shape[1]
    tok = lambda width: pl.BlockSpec((1, tm, width), lambda bi, i: (bi, i, 0))
    weights = (w_glu, b_glu, w_pa, w_ps, w_out, g2, w13, w2)
    return pl.pallas_call(
        functools.partial(_merge_ffn_kernel, base=base),
        out_shape=jax.ShapeDtypeStruct(x.shape, F32),
        grid=(b, l // tm),
        in_specs=[tok(d), pl.BlockSpec((1, N_MOD, d), lambda bi, i: (bi, 0, 0)),
                  pl.BlockSpec((1, sg, tm, LANES), lambda bi, i: (bi, 0, i, 0)),
                  pl.BlockSpec((1, d, tm), lambda bi, i: (bi, 0, i)), tok(d), tok(d)]
                 + [_resident(w.shape) for w in weights],
        out_specs=tok(d),
        compiler_params=_cparams(("parallel", "parallel")),
        name="merge_ffn2",
    )(x, mod, y, oxt, ga, gs, *weights)


def _rope_tables(l):
    n_freq = HEAD_DIM // 4
    inv = ROPE_BASE ** (-jnp.arange(n_freq, dtype=F32) / n_freq)
    pos = jnp.arange(l, dtype=jnp.int32)
    row = (pos // GRID_W).astype(F32)
    col = (pos % GRID_W).astype(F32)
    ang = jnp.concatenate([row[:, None] * inv, col[:, None] * inv], axis=-1)
    cos, sin = jnp.cos(ang), jnp.sin(ang)
    return jnp.concatenate([cos] * 4, axis=-1), jnp.concatenate([-sin, sin, -sin, sin], axis=-1)


def _group_ones():
    grp = np.arange(MXU_WIDTH) // HEAD_DIM
    return jnp.asarray(grp[:, None] == grp[None, :], dtype=BF16)


def kernel(x, c, ctx, c_ctx, w_mod, b_mod, norm_ffn1, w13_ffn1, w2_ffn1, norm_mix, w_in, q_norm, k_norm, lam_q1, lam_k1, lam_q2, lam_k2, subln, s5_lam_re, s5_lam_im, s5_log_dt, s5_b_re, s5_b_im, s5_c_re, s5_c_im, s5_d, w_glu, b_glu, w_pa, w_ps, w_out, norm_ffn2, w13_ffn2, w2_ffn2):
    assert w_mod.shape[0] == 1, "single-layer block"
    b, l, d = x.shape
    lc = ctx.shape[1]
    assert l % max(TILE_FFN1, TILE_PROJ, TILE_MERGE, TILE_Q_BOUNDED, TILE_K_ONLINE) == 0 and l % lc == 0
    assert lc % (2 * CHUNK) == 0 and d == N_HEADS * V_DIM
    lam_init = 0.8 - 0.6 * math.exp(-0.3 * 0)
    bf = lambda a: a.astype(BF16)
    row = lambda a: a.reshape(1, -1)

    rows = b + 1
    pad = (-rows) % 8
    cc = jnp.concatenate([c, c_ctx[None, :], jnp.zeros((pad, d), F32)], axis=0)
    mod = _mod_call(cc, w_mod[0], row(b_mod[0]))
    modx = mod[:b].reshape(b, N_MOD, d)
    modc = mod[b:b + 1].reshape(1, N_MOD, d)

    w13_1, w2_1 = bf(w13_ffn1[0]), bf(w2_ffn1[0])
    x1 = _ffn_call(x, modx, row(norm_ffn1[0]), w13_1, w2_1, 0, TILE_FFN1, "ffn1_x")

    ones = _group_ones()
    cos, sin = _rope_tables(l)
    qg = row(jnp.tile(q_norm[0], d // HEAD_DIM)) * (LOG2E / math.sqrt(HEAD_DIM))
    kg = row(jnp.tile(k_norm[0], d // HEAD_DIM))
    w_inb = bf(w_in[0])
    s5w = s5_d.shape[-1]
    q, k, vt, u, ga, gs = _inproj_call(x1, modx, row(norm_mix[0]), w_inb, qg, kg, ones,
                                       cos, sin, s5w, 3, TILE_PROJ)
    kc, vtc, uc = _ctx_call(ctx, modc, row(norm_ffn1[0]), w13_1, w2_1, row(norm_mix[0]),
                            w_inb[:, d:3 * d + s5w], kg, ones, s5w, 3)

    lamv = jnp.stack([lam_q1[0], lam_k1[0], lam_q2[0], lam_k2[0]], axis=0)
    bound = (LOG2E * math.sqrt(HEAD_DIM)) * jnp.max(jnp.abs(q_norm[0])) * jnp.max(jnp.abs(k_norm[0]))
    attn = lambda bounded, tq, tk: functools.partial(
        _attn_call, lamv=lamv, subln=subln[0].reshape(-1, 1), q=q, k=k, kc=kc, vt=vt, vtc=vtc,
        lam_init=lam_init, tq=tq, tk=tk, bounded=bounded)
    oxt = lax.cond(2.0 * bound <= MAX_EXP2_SPAN, attn(True, TILE_Q_BOUNDED, None),
                   attn(False, TILE_Q_ONLINE, TILE_K_ONLINE), bound.reshape(1, 1))

    mats = _s5_matrices(s5_lam_re[0], s5_lam_im[0], s5_log_dt[0], s5_b_re[0], s5_b_im[0],
                        s5_c_re[0], s5_c_im[0], s5_d[0], _s5_segment((l + lc) // CHUNK))
    y = _s5_call(u, uc, mats)

    return _merge_ffn_call(x1, modx, y, oxt, ga, gs, bf(w_glu[0]), row(b_glu[0]), bf(w_pa[0]), bf(w_ps[0]),
                           bf(w_out[0]), row(norm_ffn2[0]), bf(w13_ffn2[0]), bf(w2_ffn2[0]), 3, TILE_MERGE)
```
